```python
import math
import jax
import jax.numpy as jnp
from jax import lax
import numpy as np

D_MODEL = 1024
BATCH = 8
SEQ = 2048
DEPTH = 4
DEC_BATCH = 128
DEC_SEQ = 8
PAST_LEN = 16384
PAGE_SIZE = 128

N_MIXERS = 3
N_ML = len(range(0, DEPTH, N_MIXERS))
N_GLA = len(range(1, DEPTH, N_MIXERS))
N_S5 = len(range(2, DEPTH, N_MIXERS))
ML_INNER = 2 * D_MODEL
ML_HEADS = 4
ML_DH = ML_INNER // ML_HEADS
ML_QKV_BLOCK = 4
ML_CONV = 4
ML_CHUNK = 64
GLA_HEADS = 4
GLA_KT = D_MODEL // 2
GLA_VT = D_MODEL
GLA_DK = GLA_KT // GLA_HEADS
GLA_DV = GLA_VT // GLA_HEADS
GLA_RANK = 16
GLA_TAU = 16.0
GLA_CHUNK = 32
S5_GC = 16
S5_GROUPS = D_MODEL // S5_GC
S5_P = 64
S5_DT_MIN = 1e-3
S5_DT_MAX = 1e-1
N_EXPERTS = 64
TOP_K = 8
EXPERT_FF = 256
SHARED_FF = 256
ROUTE_SCALE = 2.5
EXPERT_BLOCK = 128
DEEPNORM_ALPHA = (2.0 * DEPTH) ** 0.25
DEEPNORM_BETA = (8.0 * DEPTH) ** -0.25
NORM_EPS = 1e-5

kernel_name = 'hybrid_mlstm_gla_s5_moe_step'


def _to_chunks(t, L):
    b, s = t.shape[0], t.shape[1]
    return jnp.moveaxis(t.reshape((b, s // L, L) + t.shape[2:]), 1, 0)


def _from_chunks(t):
    nc, b, L = t.shape[0], t.shape[1], t.shape[2]
    return jnp.moveaxis(t, 0, 1).reshape((b, nc * L) + t.shape[3:])


def layer_norm(x, g, b):
    xf = x.astype(jnp.float32)
    mu = xf.mean(-1, keepdims=True)
    var = jnp.mean(jnp.square(xf - mu), -1, keepdims=True)
    y = (xf - mu) * lax.rsqrt(var + NORM_EPS) * g.astype(jnp.float32) + b.astype(jnp.float32)
    return y.astype(x.dtype)


def head_norm(h, g):
    mu = h.mean(-1, keepdims=True)
    var = jnp.mean(jnp.square(h - mu), -1, keepdims=True)
    hn = (h - mu) * lax.rsqrt(var + NORM_EPS)
    return hn.reshape(h.shape[:2] + (-1,)) * g.astype(jnp.float32)


def causal_conv(x, buf, w, b):
    seq = x.shape[1]
    xp = jnp.concatenate([buf.astype(x.dtype), x], axis=1)
    out = b + xp[:, 0:seq] * w[0]
    for j in range(1, ML_CONV):
        out = out + xp[:, j:j + seq] * w[j]
    return out, xp[:, seq:]


def mlstm_cell(q, k, v, i_pre, logf, C0, n0, m0):
    seq = q.shape[1]
    L = math.gcd(seq, ML_CHUNK)
    causal = jnp.tril(jnp.ones((L, L), dtype=bool))

    def chunk_step(carry, inp):
        C, n, m = carry
        qc, kc, vc, ic, fc = inp
        b = jnp.cumsum(fc, axis=1)
        d_ts = b[:, :, None, :] - b[:, None, :, :] + ic[:, None, :, :]
        d_ts = jnp.where(causal[None, :, :, None], d_ts, -jnp.inf)
        m_inter = b + m[:, None, :]
        m_t = jnp.maximum(m_inter, d_ts.max(axis=2))
        a_inter = jnp.exp(m_inter - m_t)
        sc = jnp.einsum('bthd,bshd->btsh', qc, kc) * jnp.exp(d_ts - m_t[:, :, None, :])
        num = jnp.einsum('btsh,bshv->bthv', sc, vc) + a_inter[..., None] * jnp.einsum('bthk,bhkv->bthv', qc, C)
        den = sc.sum(2) + a_inter * jnp.einsum('bthk,bhk->bth', qc, n)
        h = num / jnp.maximum(jnp.abs(den), jnp.exp(-m_t))[..., None]
        m_new = m_t[:, -1]
        w_end = jnp.exp(b[:, -1:] - b + ic - m_new[:, None])
        a_end = jnp.exp(b[:, -1] + m - m_new)
        kw = kc * w_end[..., None]
        C_new = a_end[..., None, None] * C + jnp.einsum('bshk,bshv->bhkv', kw, vc)
        n_new = a_end[..., None] * n + kw.sum(1)
        return (C_new, n_new, m_new), h

    xs = tuple(_to_chunks(t, L) for t in (q, k, v, i_pre, logf))
    (C, n, m), h = lax.scan(chunk_step, (C0, n0, m0), xs)
    return _from_chunks(h), C, n, m


def mlstm_mixer(x, conv_buf, C0, n0, m0, w_in, conv_w, conv_b, wq, wk, wv, w_i, b_i, w_f, b_f, gn_g, skip, w_out):
    f32 = jnp.float32
    B, S, _ = x.shape
    xm, z = jnp.split(x @ w_in, 2, axis=-1)
    xc_pre, new_buf = causal_conv(xm, conv_buf, conv_w, conv_b)
    xc = jax.nn.silu(xc_pre)

    def headwise(t, w):
        tb = t.reshape(B, S, ML_INNER // ML_QKV_BLOCK, ML_QKV_BLOCK)
        return jnp.einsum('bsnc,ncd->bsnd', tb, w).reshape(B, S, ML_INNER)

    q = headwise(xc, wq)
    k = headwise(xc, wk)
    v = headwise(xm, wv)
    qkv = jnp.concatenate([q, k, v], axis=-1)
    i_pre = (qkv @ w_i + b_i).astype(f32)
    logf = jax.nn.log_sigmoid((qkv @ w_f + b_f).astype(f32))
    heads = lambda t: t.reshape(B, S, ML_HEADS, ML_DH).astype(f32)
    h, C, n, m = mlstm_cell(heads(q), heads(k) * (ML_DH ** -0.5), heads(v), i_pre, logf,
                            C0.astype(f32), n0.astype(f32), m0.astype(f32))
    hn = head_norm(h, gn_g).astype(x.dtype)
    out = ((hn + skip * xc) * jax.nn.silu(z)) @ w_out
    return out, new_buf, C, n, m


def gla_cell(q, k, v, log_a, S0):
    seq = q.shape[1]
    L = math.gcd(seq, GLA_CHUNK)
    causal = jnp.tril(jnp.ones((L, L), dtype=bool))

    def chunk_step(st, inp):
        qc, kc, vc, ac = inp
        b = jnp.cumsum(ac, axis=1)
        rel = jnp.where(causal[None, :, :, None, None], b[:, :, None] - b[:, None, :], -jnp.inf)
        att = jnp.sum(qc[:, :, None] * kc[:, None, :] * jnp.exp(rel), axis=-1)
        o = jnp.einsum('btsh,bshv->bthv', att, vc) + jnp.einsum('bthk,bhkv->bthv', qc * jnp.exp(b), st)
        b_end = b[:, -1]
        k_dec = kc * jnp.exp(b_end[:, None] - b)
        st_new = jnp.exp(b_end)[..., None] * st + jnp.einsum('bshk,bshv->bhkv', k_dec, vc)
        return st_new, o

    xs = tuple(_to_chunks(t, L) for t in (q, k, v, log_a))
    S_fin, o = lax.scan(chunk_step, S0, xs)
    return _from_chunks(o), S_fin


def gla_mixer(x, S0, w_in, w_a2, b_a, gn_g, w_out):
    f32 = jnp.float32
    B, S, _ = x.shape
    q, k, v, r, a_lr = jnp.split(x @ w_in, [GLA_KT, 2 * GLA_KT, 2 * GLA_KT + GLA_VT, 2 * GLA_KT + 2 * GLA_VT], axis=-1)
    q, v = q, v
    log_a = jax.nn.log_sigmoid((a_lr @ w_a2 + b_a).astype(f32)) / GLA_TAU
    hk = lambda t, d: t.reshape(B, S, GLA_HEADS, d).astype(f32)
    o, S_new = gla_cell(hk(q, GLA_DK) * (GLA_DK ** -0.5), hk(k, GLA_DK), hk(v, GLA_DV),
                        hk(log_a, GLA_DK), S0.astype(f32))
    on = head_norm(o, gn_g).astype(x.dtype)
    return (jax.nn.silu(r) * on) @ w_out, S_new


def _lin_combine(e1, e2):
    a1, b1 = e1
    a2, b2 = e2
    return a2 * a1, a2 * b1 + b2


def s5_mixer(x, s_re, s_im, w_in, a_re, a_im, log_dt, b_re, b_im, c_re, c_im, d_skip, w_glu):
    f32 = jnp.float32
    B, S, _ = x.shape
    u = x @ w_in
    lam = lax.complex(a_re.astype(f32), a_im.astype(f32))
    dt = jnp.exp(log_dt.astype(f32))[:, None]
    lam_bar = jnp.exp(lam * dt)
    b_bar = ((lam_bar - 1.0) / lam)[..., None] * lax.complex(b_re.astype(f32), b_im.astype(f32))
    ug = u.reshape(B, S, S5_GROUPS, S5_GC).astype(f32)
    bu = lax.complex(jnp.einsum('gpc,bsgc->bsgp', b_bar.real, ug),
                     jnp.einsum('gpc,bsgc->bsgp', b_bar.imag, ug))
    s0 = lax.complex(s_re.astype(f32), s_im.astype(f32))
    bu = bu.at[:, 0].add(lam_bar * s0)
    a = jnp.broadcast_to(lam_bar, bu.shape)
    _, st = lax.associative_scan(_lin_combine, (a, bu), axis=1)
    y = (jnp.einsum('gcp,bsgp->bsgc', c_re.astype(f32), st.real)
         - jnp.einsum('gcp,bsgp->bsgc', c_im.astype(f32), st.imag))
    y = y.reshape(B, S, D_MODEL) + d_skip.astype(f32) * u.astype(f32)
    y = jax.nn.gelu(y).astype(x.dtype)
    val, gate = jnp.split(y @ w_glu, 2, axis=-1)
    return val * jax.nn.sigmoid(gate), st[:, -1].real, st[:, -1].imag


def moe_ffn(x, w_router, b_router, w_g, w_u, w_d, ws_g, ws_u, ws_d):
    f32 = jnp.float32
    B, S, D = x.shape
    x2 = x.reshape(B * S, D)
    T = B * S
    s = jax.nn.sigmoid((x2 @ w_router).astype(f32))
    _, idx = lax.top_k(s + b_router.astype(f32), TOP_K)
    sel = jnp.take_along_axis(s, idx, axis=1)
    gate = sel / sel.sum(-1, keepdims=True) * ROUTE_SCALE
    N = T * TOP_K
    flat_e = idx.reshape(-1)
    order = jnp.argsort(flat_e)
    se = flat_e[order]
    stok = (order // TOP_K).astype(jnp.int32)
    sg = gate.reshape(-1)[order]
    counts = jnp.bincount(flat_e, length=N_EXPERTS)
    padded = (counts + EXPERT_BLOCK - 1) // EXPERT_BLOCK * EXPERT_BLOCK
    pad_end = jnp.cumsum(padded)
    pad_start = pad_end - padded
    start = jnp.cumsum(counts) - counts
    dest = pad_start[se] + jnp.arange(N) - start[se]
    n_blocks = -(-N // EXPERT_BLOCK) + N_EXPERTS
    Lb = n_blocks * EXPERT_BLOCK
    buf_tok = jnp.full((Lb,), T, jnp.int32).at[dest].set(stok)
    buf_g = jnp.zeros((Lb,), f32).at[dest].set(sg)
    blk_e = jnp.minimum(jnp.searchsorted(pad_end, jnp.arange(n_blocks) * EXPERT_BLOCK, side='right'), N_EXPERTS - 1)
    x_pad = jnp.concatenate([x2, jnp.zeros((1, D), x2.dtype)], axis=0)

    def expert_block(args):
        tok, e = args
        xb = x_pad[tok]
        h = jax.nn.silu(xb @ w_g[e]) * (xb @ w_u[e])
        return h @ w_d[e]

    yb = lax.map(expert_block, (buf_tok.reshape(n_blocks, EXPERT_BLOCK), blk_e))
    yb = yb.reshape(Lb, D) * buf_g[:, None].astype(yb.dtype)
    routed = jax.ops.segment_sum(yb, buf_tok, num_segments=T + 1)[:T]
    shared = (jax.nn.silu(x2 @ ws_g) * (x2 @ ws_u)) @ ws_d
    return (routed + shared).reshape(B, S, D)


def setup_inputs(seed: int = 0) -> dict:
    key = jax.random.key(seed)
    ks = iter(jax.random.split(key, 64))
    f32 = jnp.float32
    nrm = lambda shape, scale: jax.random.normal(next(ks), shape, f32) * scale
    beta = DEEPNORM_BETA
    D = D_MODEL
    d = {}
    d['x_prompt'] = nrm((BATCH, SEQ, D), 1.0)
    d['x_sample'] = nrm((DEC_BATCH, DEC_SEQ, D), 1.0)
    d['state_mlstm_C'] = nrm((N_ML, DEC_BATCH, ML_HEADS, ML_DH, ML_DH), 1.0)
    d['state_mlstm_n'] = nrm((N_ML, DEC_BATCH, ML_HEADS, ML_DH), 1.0)
    d['state_mlstm_m'] = nrm((N_ML, DEC_BATCH, ML_HEADS), 1.0)
    d['state_mlstm_conv'] = nrm((N_ML, DEC_BATCH, ML_CONV - 1, ML_INNER), 1.0)
    d['state_gla_S'] = nrm((N_GLA, DEC_BATCH, GLA_HEADS, GLA_DK, GLA_DV), 1.0)
    d['state_s5_re'] = nrm((N_S5, DEC_BATCH, S5_GROUPS, S5_P), 1.0)
    d['state_s5_im'] = nrm((N_S5, DEC_BATCH, S5_GROUPS, S5_P), 1.0)
    d['ln1_g'] = 1.0 + nrm((DEPTH, D), 0.02)
    d['ln1_b'] = nrm((DEPTH, D), 0.02)
    d['ln2_g'] = 1.0 + nrm((DEPTH, D), 0.02)
    d['ln2_b'] = nrm((DEPTH, D), 0.02)
    nb = ML_INNER // ML_QKV_BLOCK
    d['ml_w_in'] = nrm((N_ML, D, 2 * ML_INNER), D ** -0.5)
    d['ml_conv_w'] = nrm((N_ML, ML_CONV, ML_INNER), ML_CONV ** -0.5)
    d['ml_conv_b'] = nrm((N_ML, ML_INNER), 0.02)
    d['ml_wq'] = nrm((N_ML, nb, ML_QKV_BLOCK, ML_QKV_BLOCK), ML_QKV_BLOCK ** -0.5)
    d['ml_wk'] = nrm((N_ML, nb, ML_QKV_BLOCK, ML_QKV_BLOCK), ML_QKV_BLOCK ** -0.5)
    d['ml_wv'] = nrm((N_ML, nb, ML_QKV_BLOCK, ML_QKV_BLOCK), ML_QKV_BLOCK ** -0.5)
    d['ml_w_i'] = nrm((N_ML, 3 * ML_INNER, ML_HEADS), (3 * ML_INNER) ** -0.5)
    d['ml_b_i'] = nrm((N_ML, ML_HEADS), 0.1)
    d['ml_w_f'] = nrm((N_ML, 3 * ML_INNER, ML_HEADS), (3 * ML_INNER) ** -0.5)
    d['ml_b_f'] = jnp.linspace(3.0, 6.0, ML_HEADS, dtype=f32)[None, :] + nrm((N_ML, ML_HEADS), 0.1)
    d['ml_gn_g'] = 1.0 + nrm((N_ML, ML_INNER), 0.02)
    d['ml_skip'] = 1.0 + nrm((N_ML, ML_INNER), 0.02)
    d['ml_w_out'] = nrm((N_ML, ML_INNER, D), ML_INNER ** -0.5 * beta)
    d['gla_w_in'] = nrm((N_GLA, D, 2 * GLA_KT + 2 * GLA_VT + GLA_RANK), D ** -0.5)
    d['gla_w_a2'] = nrm((N_GLA, GLA_RANK, GLA_KT), GLA_RANK ** -0.5)
    d['gla_b_a'] = nrm((N_GLA, GLA_KT), 0.1)
    d['gla_gn_g'] = 1.0 + nrm((N_GLA, GLA_VT), 0.02)
    d['gla_w_out'] = nrm((N_GLA, GLA_VT, D), GLA_VT ** -0.5 * beta)
    d['s5_w_in'] = nrm((N_S5, D, D), D ** -0.5)
    d['s5_a_re'] = -0.5 + nrm((N_S5, S5_GROUPS, S5_P), 0.01)
    d['s5_a_im'] = jnp.broadcast_to(math.pi * jnp.arange(S5_P, dtype=f32), (N_S5, S5_GROUPS, S5_P)) + nrm((N_S5, S5_GROUPS, S5_P), 0.01)
    d['s5_log_dt'] = jax.random.uniform(next(ks), (N_S5, S5_GROUPS), f32, minval=math.log(S5_DT_MIN), maxval=math.log(S5_DT_MAX))
    d['s5_b_re'] = nrm((N_S5, S5_GROUPS, S5_P, S5_GC), (2.0 * S5_GC) ** -0.5)
    d['s5_b_im'] = nrm((N_S5, S5_GROUPS, S5_P, S5_GC), (2.0 * S5_GC) ** -0.5)
    d['s5_c_re'] = nrm((N_S5, S5_GROUPS, S5_GC, S5_P), (2.0 * S5_P) ** -0.5)
    d['s5_c_im'] = nrm((N_S5, S5_GROUPS, S5_GC, S5_P), (2.0 * S5_P) ** -0.5)
    d['s5_d'] = nrm((N_S5, D), 1.0)
    d['s5_w_glu'] = nrm((N_S5, D, 2 * D), D ** -0.5 * beta)
    d['moe_w_router'] = nrm((DEPTH, D, N_EXPERTS), D ** -0.5)
    d['moe_b_router'] = nrm((DEPTH, N_EXPERTS), 0.01)
    d['moe_w_gate'] = nrm((DEPTH, N_EXPERTS, D, EXPERT_FF), D ** -0.5)
    d['moe_w_up'] = nrm((DEPTH, N_EXPERTS, D, EXPERT_FF), D ** -0.5)
    d['moe_w_down'] = nrm((DEPTH, N_EXPERTS, EXPERT_FF, D), EXPERT_FF ** -0.5 * beta)
    d['moe_ws_gate'] = nrm((DEPTH, D, SHARED_FF), D ** -0.5)
    d['moe_ws_up'] = nrm((DEPTH, D, SHARED_FF), D ** -0.5)
    d['moe_ws_down'] = nrm((DEPTH, SHARED_FF, D), SHARED_FF ** -0.5 * beta)
    return d


def reference(x_prompt, x_sample, state_mlstm_C, state_mlstm_n, state_mlstm_m, state_mlstm_conv,
              state_gla_S, state_s5_re, state_s5_im, ln1_g, ln1_b, ln2_g, ln2_b,
              ml_w_in, ml_conv_w, ml_conv_b, ml_wq, ml_wk, ml_wv, ml_w_i, ml_b_i, ml_w_f, ml_b_f,
              ml_gn_g, ml_skip, ml_w_out, gla_w_in, gla_w_a2, gla_b_a, gla_gn_g, gla_w_out,
              s5_w_in, s5_a_re, s5_a_im, s5_log_dt, s5_b_re, s5_b_im, s5_c_re, s5_c_im, s5_d, s5_w_glu,
              moe_w_router, moe_b_router, moe_w_gate, moe_w_up, moe_w_down,
              moe_ws_gate, moe_ws_up, moe_ws_down):
    f32 = jnp.float32
    Bp = x_prompt.shape[0]
    hp, hs = x_prompt, x_sample
    p_C, p_n, p_m, p_conv, p_S, p_re, p_im = [], [], [], [], [], [], []
    s_C, s_n, s_m, s_conv, s_S, s_re, s_im = [], [], [], [], [], [], []
    for i in range(DEPTH):
        j = i // N_MIXERS
        if i % N_MIXERS == 0:
            w = (ml_w_in[j], ml_conv_w[j], ml_conv_b[j], ml_wq[j], ml_wk[j], ml_wv[j], ml_w_i[j], ml_b_i[j],
                 ml_w_f[j], ml_b_f[j], ml_gn_g[j], ml_skip[j], ml_w_out[j])
            mix_p, cv, C, n, m = mlstm_mixer(hp, jnp.zeros((Bp, ML_CONV - 1, ML_INNER), hp.dtype),
                                             jnp.zeros((Bp, ML_HEADS, ML_DH, ML_DH), f32),
                                             jnp.zeros((Bp, ML_HEADS, ML_DH), f32),
                                             jnp.zeros((Bp, ML_HEADS), f32), *w)
            p_conv.append(cv); p_C.append(C); p_n.append(n); p_m.append(m)
            mix_s, cv, C, n, m = mlstm_mixer(hs, state_mlstm_conv[j], state_mlstm_C[j], state_mlstm_n[j],
                                             state_mlstm_m[j], *w)
            s_conv.append(cv); s_C.append(C); s_n.append(n); s_m.append(m)
        elif i % N_MIXERS == 1:
            w = (gla_w_in[j], gla_w_a2[j], gla_b_a[j], gla_gn_g[j], gla_w_out[j])
            mix_p, S_new = gla_mixer(hp, jnp.zeros((Bp, GLA_HEADS, GLA_DK, GLA_DV), f32), *w)
            p_S.append(S_new)
            mix_s, S_new = gla_mixer(hs, state_gla_S[j], *w)
            s_S.append(S_new)
        else:
            w = (s5_w_in[j], s5_a_re[j], s5_a_im[j], s5_log_dt[j], s5_b_re[j], s5_b_im[j],
                 s5_c_re[j], s5_c_im[j], s5_d[j], s5_w_glu[j])
            zero_s = jnp.zeros((Bp, S5_GROUPS, S5_P), f32)
            mix_p, sre, sim = s5_mixer(hp, zero_s, zero_s, *w)
            p_re.append(sre); p_im.append(sim)
            mix_s, sre, sim = s5_mixer(hs, state_s5_re[j], state_s5_im[j], *w)
            s_re.append(sre); s_im.append(sim)
        hp = layer_norm(DEEPNORM_ALPHA * hp + mix_p, ln1_g[i], ln1_b[i])
        hs = layer_norm(DEEPNORM_ALPHA * hs + mix_s, ln1_g[i], ln1_b[i])
        wm = (moe_w_router[i], moe_b_router[i], moe_w_gate[i], moe_w_up[i], moe_w_down[i],
              moe_ws_gate[i], moe_ws_up[i], moe_ws_down[i])
        hp = layer_norm(DEEPNORM_ALPHA * hp + moe_ffn(hp, *wm), ln2_g[i], ln2_b[i])
        hs = layer_norm(DEEPNORM_ALPHA * hs + moe_ffn(hs, *wm), ln2_g[i], ln2_b[i])
    return (hp, hs,
            jnp.stack(p_C), jnp.stack(p_n), jnp.stack(p_m), jnp.stack(p_conv), jnp.stack(p_S),
            jnp.stack(p_re), jnp.stack(p_im),
            jnp.stack(s_C), jnp.stack(s_n), jnp.stack(s_m), jnp.stack(s_conv), jnp.stack(s_S),
            jnp.stack(s_re), jnp.stack(s_im))
```

```python
import functools
import math

import jax
import jax.numpy as jnp
from jax import lax
from jax.experimental import pallas as pl
from jax.experimental.pallas import tpu as pltpu

F32 = jnp.float32
BF16 = jnp.bfloat16

D_MODEL = 1024
DEPTH = 4
N_MIXERS = 3
ML_INNER = 2 * D_MODEL
ML_HEADS = 4
ML_DH = ML_INNER // ML_HEADS
ML_QKV_BLOCK = 4
ML_CONV = 4
GLA_HEADS = 4
GLA_KT = D_MODEL // 2
GLA_VT = D_MODEL
GLA_DK = GLA_KT // GLA_HEADS
GLA_DV = GLA_VT // GLA_HEADS
GLA_RANK = 16
GLA_TAU = 16.0
S5_GC = 16
S5_GROUPS = D_MODEL // S5_GC
S5_P = 64
S5_STATE = S5_GROUPS * S5_P
N_EXPERTS = 64
TOP_K = 8
EXPERT_FF = 256
ROUTE_SCALE = 2.5
DEEPNORM_ALPHA = (2.0 * DEPTH) ** 0.25
NORM_EPS = 1e-5

V7X_VMEM_BYTES = 64 * 1024 * 1024
V7X_LANES = 128
V7X_SUBLANES = 8
V7X_MXU_DIM = 256

ROW_BLOCK = 256
ML_CHUNK = 256
GLA_CHUNK = 128
S5_CHUNK = 32
S5_TILE_CH = 128
S5_TILE_ST = S5_TILE_CH // S5_GC * S5_P
MOE_ROWS = 1024


def _params(semantics, vmem_mb):
    assert vmem_mb * 1024 * 1024 < V7X_VMEM_BYTES
    return pltpu.CompilerParams(dimension_semantics=semantics, vmem_limit_bytes=vmem_mb * 1024 * 1024)


def _dot(a, b):
    return jnp.dot(a.astype(BF16), b.astype(BF16), preferred_element_type=F32)


def _dot_nt(a, b):
    return lax.dot_general(a.astype(BF16), b.astype(BF16), (((1,), (1,)), ((), ())), preferred_element_type=F32)


def _dot_tn(a, b):
    return lax.dot_general(a.astype(BF16), b.astype(BF16), (((0,), (0,)), ((), ())), preferred_element_type=F32)


def _sigmoid(x):
    return 1.0 / (1.0 + jnp.exp(-x))


def _silu(x):
    return x * _sigmoid(x)


def _log_sigmoid(x):
    return jnp.minimum(x, 0.0) - jnp.log1p(jnp.exp(-jnp.abs(x)))


def _layer_norm(y, g, b):
    mu = jnp.mean(y, axis=-1, keepdims=True)
    d = y - mu
    var = jnp.mean(d * d, axis=-1, keepdims=True)
    return d * lax.rsqrt(var + NORM_EPS) * g + b


def _head_norm(h, g):
    mu = jnp.mean(h, axis=-1, keepdims=True)
    d = h - mu
    var = jnp.mean(d * d, axis=-1, keepdims=True)
    return d * lax.rsqrt(var + NORM_EPS) * g


def _split3(x):
    hi = x.astype(BF16)
    r1 = x - hi.astype(F32)
    mid = r1.astype(BF16)
    lo = (r1 - mid.astype(F32)).astype(BF16)
    return hi, mid, lo


def _mm_body(x_ref, w_ref, o_ref):
    o_ref[...] = _dot(x_ref[...], w_ref[...])


def _mm(x, w, tn):
    m, k = x.shape
    n = w.shape[1]
    tm = ROW_BLOCK
    return pl.pallas_call(
        _mm_body,
        grid=(n // tn, m // tm),
        in_specs=[pl.BlockSpec((tm, k), lambda j, i: (i, 0)), pl.BlockSpec((k, tn), lambda j, i: (0, j))],
        out_specs=pl.BlockSpec((tm, tn), lambda j, i: (i, j)),
        out_shape=jax.ShapeDtypeStruct((m, n), F32),
        compiler_params=_params(("arbitrary", "arbitrary"), 40),
        name="mm",
    )(x, w)


def _mm_ln_body(a_ref, w_ref, r_ref, g_ref, b_ref, o_ref):
    y = _dot(a_ref[...], w_ref[...])
    o_ref[...] = _layer_norm(DEEPNORM_ALPHA * r_ref[...] + y, g_ref[...], b_ref[...])


def _mm_ln(a, w, resid, g, b):
    m, k = a.shape
    n = w.shape[1]
    tm = ROW_BLOCK
    return pl.pallas_call(
        _mm_ln_body,
        grid=(m // tm,),
        in_specs=[pl.BlockSpec((tm, k), lambda i: (i, 0)), pl.BlockSpec((k, n), lambda i: (0, 0)),
                  pl.BlockSpec((tm, n), lambda i: (i, 0)), pl.BlockSpec((1, n), lambda i: (0, 0)),
                  pl.BlockSpec((1, n), lambda i: (0, 0))],
        out_specs=pl.BlockSpec((tm, n), lambda i: (i, 0)),
        out_shape=jax.ShapeDtypeStruct((m, n), F32),
        compiler_params=_params(("arbitrary",), 40),
        name="mm_ln",
    )(a, w, resid, g.reshape(1, n), b.reshape(1, n))


def _ml_pre_body(x_ref, prev_ref, p_ref, cw_ref, cb_ref, wq_ref, wk_ref, wv_ref, wgq_ref, wgk_ref, wgv_ref, gb_ref,
                 q_ref, k_ref, v_ref, xc_ref, g_ref, scx, scp, *, n_prompt_blocks, blocks_per_seq, sample_len):
    i = pl.program_id(0)
    rows = x_ref.shape[0]
    is_prompt = i < n_prompt_blocks
    no_prev = jnp.logical_or(i % blocks_per_seq == 0, jnp.logical_not(is_prompt))
    x = x_ref[...]
    scx[0:V7X_SUBLANES, :] = jnp.where(no_prev, 0.0, prev_ref[...])
    scx[V7X_SUBLANES:V7X_SUBLANES + rows, :] = x
    scp[0:rows, :] = p_ref[...]
    scp[rows:rows + V7X_SUBLANES, :] = jnp.zeros((V7X_SUBLANES, x.shape[1]), F32)
    r = lax.broadcasted_iota(jnp.int32, (rows, 1), 0)
    tpos = jnp.where(is_prompt, (i % blocks_per_seq) * rows + r, r % sample_len)
    acc = cb_ref[...] + x * cw_ref[ML_CONV - 1:ML_CONV, :]
    for s in range(1, ML_CONV):
        xs = scx[V7X_SUBLANES - s:V7X_SUBLANES - s + rows, :]
        ps = scp[ML_CONV - 1 - s:ML_CONV - 1 - s + rows, :]
        acc = acc + jnp.where(tpos >= s, xs, ps) * cw_ref[ML_CONV - 1 - s:ML_CONV - s, :]
    xc = _silu(acc)
    xc_ref[...] = xc
    g = jnp.zeros((2 * ML_HEADS, rows), F32)
    for t in range(ML_INNER // V7X_MXU_DIM):
        sl = slice(t * V7X_MXU_DIM, (t + 1) * V7X_MXU_DIM)
        q_t = _dot(xc[:, sl], wq_ref[t])
        k_t = _dot(xc[:, sl], wk_ref[t])
        v_t = _dot(x[:, sl], wv_ref[t])
        q_ref[:, sl] = q_t
        k_ref[:, sl] = k_t * (ML_DH ** -0.5)
        v_ref[:, sl] = v_t
        g = g + _dot_nt(wgq_ref[:, sl], q_t) + _dot_nt(wgk_ref[:, sl], k_t) + _dot_nt(wgv_ref[:, sl], v_t)
    g = g + gb_ref[...]
    gate_row = lax.broadcasted_iota(jnp.int32, g.shape, 0)
    g_ref[0] = jnp.where(gate_row < ML_HEADS, g, _log_sigmoid(g))


def _ml_pre(xmz, conv_rows, w, n_prompt_rows, seq_len, sample_len):
    t_all = xmz.shape[0]
    rb = ROW_BLOCK
    nblk = t_all // rb
    npb = n_prompt_rows // rb
    bps = seq_len // rb
    sub = rb // V7X_SUBLANES
    body = functools.partial(_ml_pre_body, n_prompt_blocks=npb, blocks_per_seq=bps, sample_len=sample_len)
    full2 = lambda shape: pl.BlockSpec(shape, lambda i: (0, 0))
    full3 = lambda shape: pl.BlockSpec(shape, lambda i: (0, 0, 0))
    nt = ML_INNER // V7X_MXU_DIM
    row_spec = pl.BlockSpec((rb, ML_INNER), lambda i: (i, 0))
    return pl.pallas_call(
        body,
        grid=(nblk,),
        in_specs=[row_spec,
                  pl.BlockSpec((V7X_SUBLANES, ML_INNER), lambda i: (jnp.maximum(i * sub - 1, 0), 0)),
                  pl.BlockSpec((rb, ML_INNER), lambda i: (jnp.maximum(i - npb + 1, 0), 0)),
                  full2((ML_CONV, ML_INNER)), full2((1, ML_INNER)),
                  full3((nt, V7X_MXU_DIM, V7X_MXU_DIM)), full3((nt, V7X_MXU_DIM, V7X_MXU_DIM)),
                  full3((nt, V7X_MXU_DIM, V7X_MXU_DIM)),
                  full2((2 * ML_HEADS, ML_INNER)), full2((2 * ML_HEADS, ML_INNER)), full2((2 * ML_HEADS, ML_INNER)),
                  full2((2 * ML_HEADS, 1))],
        out_specs=[row_spec, row_spec, row_spec, row_spec,
                   pl.BlockSpec((1, 2 * ML_HEADS, rb), lambda i: (i, 0, 0))],
        out_shape=[jax.ShapeDtypeStruct((t_all, ML_INNER), F32)] * 4
        + [jax.ShapeDtypeStruct((nblk, 2 * ML_HEADS, rb), F32)],
        scratch_shapes=[pltpu.VMEM((rb + V7X_SUBLANES, ML_INNER), F32), pltpu.VMEM((rb + V7X_SUBLANES, ML_INNER), F32)],
        compiler_params=_params(("arbitrary",), 48),
        name="ml_pre",
    )(xmz, xmz, conv_rows, w["conv_w"], w["conv_b"], w["wq"], w["wk"], w["wv"], w["wgq"], w["wgk"], w["wgv"], w["gb"])


def _ml_cell_body(*refs, chunk, has_state):
    if has_state:
        (q_ref, k_ref, v_ref, xc_ref, z_ref, g_ref, gn_ref, sk_ref, c0_ref, n0_ref, m0_ref, _,
         o_ref, co_ref, no_ref, mo_ref, cs, ns, ms) = refs
    else:
        (q_ref, k_ref, v_ref, xc_ref, z_ref, g_ref, gn_ref, sk_ref,
         o_ref, co_ref, no_ref, mo_ref, cs, ns, ms) = refs
    h = pl.program_id(1)
    c = pl.program_id(2)
    nc = pl.num_programs(2)
    L = chunk

    @pl.when(c == 0)
    def _():
        if has_state:
            cs[...] = c0_ref[0, 0]
            ns[...] = n0_ref[0]
            ms[...] = m0_ref[0]
        else:
            cs[...] = jnp.zeros(cs.shape, F32)
            ns[...] = jnp.zeros(ns.shape, F32)
            ms[...] = jnp.zeros(ms.shape, F32)

    q = q_ref[...]
    k = k_ref[...]
    v = v_ref[...]
    ip = g_ref[0, pl.ds(h, 1), :]
    fl = g_ref[0, pl.ds(ML_HEADS + h, 1), :]
    ri = lax.broadcasted_iota(jnp.int32, (L, L), 0)
    ci = lax.broadcasted_iota(jnp.int32, (L, L), 1)
    eye = ri == ci
    tril = ci <= ri
    f_col = jnp.sum(jnp.where(eye, fl, 0.0), axis=1, keepdims=True)
    b_col = jnp.sum(jnp.where(tril, fl, 0.0), axis=1, keepdims=True)
    b_row = jnp.sum(jnp.where(ri <= ci, f_col, 0.0), axis=0, keepdims=True)
    ib = ip - b_row
    m_prev = ms[...]
    d = jnp.where(tril, b_col + ib, -jnp.inf)
    m_inter = b_col + m_prev
    m_t = jnp.maximum(m_inter, jnp.max(d, axis=1, keepdims=True))
    a_inter = jnp.exp(m_inter - m_t)
    s = _dot_nt(q, k) * jnp.exp(d - m_t)
    num = _dot(s, v) + a_inter * _dot(q, cs[...])
    den = jnp.sum(s, axis=1, keepdims=True) + a_inter * jnp.sum(q * ns[...], axis=1, keepdims=True)
    hc = num / jnp.maximum(jnp.abs(den), jnp.exp(-m_t))
    m_new = m_t[L - 1:L, :]
    b_last = b_row[:, L - 1:L]
    w_row = jnp.exp(b_last + ib - m_new)
    w_col = jnp.sum(jnp.where(eye, w_row, 0.0), axis=1, keepdims=True)
    a_end = jnp.exp(b_last + m_prev - m_new)
    kw = k * w_col
    cs[...] = a_end * cs[...] + _dot_tn(kw, v)
    ns[...] = a_end * ns[...] + jnp.sum(kw, axis=0, keepdims=True)
    ms[...] = m_new
    hn = _head_norm(hc, gn_ref[...])
    o_ref[...] = (hn + sk_ref[...] * xc_ref[...]) * _silu(z_ref[...])

    @pl.when(c == nc - 1)
    def _():
        co_ref[0, 0] = cs[...]
        no_ref[0] = ns[...]
        mo_ref[0] = ms[...]


def _ml_cell(q, k, v, xc, xmz, gates, gn, skip, n_seq, seq_len, chunk, row0, state=None, gated_in=None):
    t_all = q.shape[0]
    nc = seq_len // chunk
    blk0 = row0 // chunk
    has_state = state is not None
    row_map = lambda b, h, c: (blk0 + b * nc + c, h)
    z_map = lambda b, h, c: (blk0 + b * nc + c, ML_HEADS + h)
    head_spec = pl.BlockSpec((chunk, ML_DH), row_map)
    in_specs = [head_spec, head_spec, head_spec, head_spec,
                pl.BlockSpec((chunk, ML_DH), z_map),
                pl.BlockSpec((1, 2 * ML_HEADS, chunk), lambda b, h, c: (b * nc + c, 0, 0)),
                pl.BlockSpec((1, ML_DH), lambda b, h, c: (0, h)),
                pl.BlockSpec((1, ML_DH), lambda b, h, c: (0, h))]
    args = [q, k, v, xc, xmz, gates, gn, skip]
    aliases = {}
    if has_state:
        c0, n0, m0 = state
        in_specs += [pl.BlockSpec((1, 1, ML_DH, ML_DH), lambda b, h, c: (b, h, 0, 0)),
                     pl.BlockSpec((1, 1, ML_DH), lambda b, h, c: (b * ML_HEADS + h, 0, 0)),
                     pl.BlockSpec((1, 1, 1), lambda b, h, c: (b * ML_HEADS + h, 0, 0)),
                     pl.BlockSpec(memory_space=pl.ANY)]
        args += [c0, n0.reshape(n_seq * ML_HEADS, 1, ML_DH), m0.reshape(n_seq * ML_HEADS, 1, 1), gated_in]
        aliases = {len(args) - 1: 0}
    out = pl.pallas_call(
        functools.partial(_ml_cell_body, chunk=chunk, has_state=has_state),
        grid=(n_seq, ML_HEADS, nc),
        in_specs=in_specs,
        out_specs=[head_spec,
                   pl.BlockSpec((1, 1, ML_DH, ML_DH), lambda b, h, c: (b, h, 0, 0)),
                   pl.BlockSpec((1, 1, ML_DH), lambda b, h, c: (b * ML_HEADS + h, 0, 0)),
                   pl.BlockSpec((1, 1, 1), lambda b, h, c: (b * ML_HEADS + h, 0, 0))],
        out_shape=[jax.ShapeDtypeStruct((t_all, ML_INNER), F32),
                   jax.ShapeDtypeStruct((n_seq, ML_HEADS, ML_DH, ML_DH), F32),
                   jax.ShapeDtypeStruct((n_seq * ML_HEADS, 1, ML_DH), F32),
                   jax.ShapeDtypeStruct((n_seq * ML_HEADS, 1, 1), F32)],
        scratch_shapes=[pltpu.VMEM((ML_DH, ML_DH), F32), pltpu.VMEM((1, ML_DH), F32), pltpu.VMEM((1, 1), F32)],
        input_output_aliases=aliases,
        compiler_params=_params(("arbitrary", "arbitrary", "arbitrary"), 40),
        name="ml_cell_state" if has_state else "ml_cell",
    )(*args)
    gated, c_new, n_new, m_new = out
    return gated, c_new, n_new.reshape(n_seq, ML_HEADS, ML_DH), m_new.reshape(n_seq, ML_HEADS)


def _ml_weights(w_in, conv_w, conv_b, wq, wk, wv, w_i, b_i, w_f, b_f, gn_g, skip, w_out):
    nt = ML_INNER // V7X_MXU_DIM
    per = V7X_MXU_DIM // ML_QKV_BLOCK
    eye = jnp.eye(per, dtype=F32)

    def block_diag(w):
        wt = w.reshape(nt, per, ML_QKV_BLOCK, ML_QKV_BLOCK)
        return jnp.einsum("tncd,nm->tncmd", wt, eye).reshape(nt, V7X_MXU_DIM, V7X_MXU_DIM).astype(BF16)

    wg = jnp.concatenate([w_i, w_f], axis=1).T.astype(BF16)
    return dict(w_in=w_in.astype(BF16), conv_w=conv_w, conv_b=conv_b.reshape(1, ML_INNER),
                wq=block_diag(wq), wk=block_diag(wk), wv=block_diag(wv),
                wgq=wg[:, :ML_INNER], wgk=wg[:, ML_INNER:2 * ML_INNER], wgv=wg[:, 2 * ML_INNER:],
                gb=jnp.concatenate([b_i, b_f]).reshape(2 * ML_HEADS, 1),
                gn=gn_g.reshape(1, ML_INNER), skip=skip.reshape(1, ML_INNER), w_out=w_out.astype(BF16))


def _ml_layer(x_all, w, dims, state_c, state_n, state_m, state_conv, ln_g, ln_b):
    n_p, seq_len, n_s, sample_len = dims
    t_p = n_p * seq_len
    rb = ROW_BLOCK
    xmz = _mm(x_all, w["w_in"], 2048)
    pad = jnp.pad(state_conv, ((0, 0), (0, sample_len - (ML_CONV - 1)), (0, 0))).reshape(n_s * sample_len, ML_INNER)
    conv_rows = jnp.concatenate([jnp.zeros((rb, ML_INNER), F32), pad], axis=0)
    q, k, v, xc, g3 = _ml_pre(xmz, conv_rows, w, t_p, seq_len, sample_len)
    npb = t_p // rb
    g_p = g3[:npb]
    if ML_CHUNK != rb:
        g_p = g_p.reshape(npb, 2 * ML_HEADS, rb // ML_CHUNK, ML_CHUNK).transpose(0, 2, 1, 3).reshape(-1, 2 * ML_HEADS, ML_CHUNK)
    g_s = g3[npb:].reshape(-1, 2 * ML_HEADS, rb // sample_len, sample_len).transpose(0, 2, 1, 3)
    g_s = g_s.reshape(n_s, 2 * ML_HEADS, sample_len)
    gated, pc, pn, pm = _ml_cell(q, k, v, xc, xmz, g_p, w["gn"], w["skip"], n_p, seq_len, ML_CHUNK, 0)
    gated, sc, sn, sm = _ml_cell(q, k, v, xc, xmz, g_s, w["gn"], w["skip"], n_s, sample_len, sample_len, t_p,
                                 state=(state_c, state_n, state_m), gated_in=gated)
    h1 = _mm_ln(gated, w["w_out"], x_all, ln_g, ln_b)
    xm = xmz[:, :ML_INNER]
    p_conv = xm[:t_p].reshape(n_p, seq_len, ML_INNER)[:, seq_len - (ML_CONV - 1):]
    s_conv = xm[t_p:].reshape(n_s, sample_len, ML_INNER)[:, sample_len - (ML_CONV - 1):]
    return h1, (pc, pn, pm, p_conv), (sc, sn, sm, s_conv)


def _gla_in_body(x_ref, w_ref, wa_ref, wa2_ref, ba_ref, o_ref, la_ref):
    x = x_ref[...].astype(BF16)
    o_ref[...] = jnp.dot(x, w_ref[...], preferred_element_type=F32)
    a = jnp.dot(x, wa_ref[...], preferred_element_type=F32)
    la_ref[...] = _log_sigmoid(_dot(a, wa2_ref[...]) + ba_ref[...]) / GLA_TAU


def _gla_in(x, w):
    m, k = x.shape
    tm = ROW_BLOCK
    n1 = 2 * GLA_KT + 2 * GLA_VT
    full = lambda shape: pl.BlockSpec(shape, lambda i: (0, 0))
    return pl.pallas_call(
        _gla_in_body,
        grid=(m // tm,),
        in_specs=[pl.BlockSpec((tm, k), lambda i: (i, 0)), full((k, n1)), full((k, V7X_LANES)),
                  full((V7X_LANES, GLA_KT)), full((1, GLA_KT))],
        out_specs=[pl.BlockSpec((tm, n1), lambda i: (i, 0)), pl.BlockSpec((tm, GLA_KT), lambda i: (i, 0))],
        out_shape=[jax.ShapeDtypeStruct((m, n1), F32), jax.ShapeDtypeStruct((m, GLA_KT), F32)],
        compiler_params=_params(("arbitrary",), 48),
        name="gla_in",
    )(x, w["w_qkvr"], w["w_a"], w["w_a2"], w["b_a"])


def _gla_levels(chunk):
    t = jnp.arange(chunk)[:, None]
    s = jnp.arange(chunk)[None, :]
    mats = [(s <= t)]
    size = chunk
    while size >= 2:
        ref = t - t % size + size // 2 - 1
        mats.append(s <= ref)
        size //= 2
    return jnp.concatenate(mats, axis=0).astype(BF16)


def _gla_cell_body(*refs, chunk, has_state):
    if has_state:
        q_ref, k_ref, v_ref, r_ref, la_ref, lv_ref, gn_ref, s0_ref, _, o_ref, so_ref, ss = refs
    else:
        q_ref, k_ref, v_ref, r_ref, la_ref, lv_ref, gn_ref, o_ref, so_ref, ss = refs
    c = pl.program_id(2)
    nc = pl.num_programs(2)
    L = chunk
    kpad = lv_ref.shape[1]

    @pl.when(c == 0)
    def _():
        if has_state:
            ss[...] = s0_ref[0, 0]
        else:
            ss[...] = jnp.zeros(ss.shape, F32)

    q = q_ref[...] * (GLA_DK ** -0.5)
    k = k_ref[...]
    v = v_ref[...]
    la = la_ref[...]
    parts = jnp.concatenate(_split3(la), axis=1)
    if kpad > L:
        parts = jnp.concatenate([parts, jnp.zeros((kpad - L, parts.shape[1]), BF16)], axis=0)
    cum = jnp.dot(lv_ref[...], parts, preferred_element_type=F32)
    cum = cum[:, :GLA_DK] + cum[:, GLA_DK:2 * GLA_DK] + cum[:, 2 * GLA_DK:]
    b = cum[0:L]
    ri = lax.broadcasted_iota(jnp.int32, (L, L), 0)
    ci = lax.broadcasted_iota(jnp.int32, (L, L), 1)
    tpos = lax.broadcasted_iota(jnp.int32, (L, 1), 0)
    att = jnp.where(ri == ci, jnp.sum(q * k, axis=1, keepdims=True), 0.0)
    size = L
    lvl = 1
    while size >= 2:
        bref = cum[lvl * L:(lvl + 1) * L]
        upper = (tpos % size) >= (size // 2)
        qs = q * jnp.exp(jnp.where(upper, b - bref, -jnp.inf))
        ks = k * jnp.exp(jnp.where(upper, -jnp.inf, bref - b))
        att = att + jnp.where((ri // size) == (ci // size), _dot_nt(qs, ks), 0.0)
        size //= 2
        lvl += 1
    st = ss[...]
    o = _dot(att, v) + _dot(q * jnp.exp(b), st)
    b_end = b[L - 1:L, :]
    e_end = jnp.exp(b_end)
    rk = lax.broadcasted_iota(jnp.int32, (GLA_DK, GLA_DK), 0)
    ck = lax.broadcasted_iota(jnp.int32, (GLA_DK, GLA_DK), 1)
    e_col = jnp.sum(jnp.where(rk == ck, e_end, 0.0), axis=1, keepdims=True)
    ss[...] = e_col * st + _dot_tn(k * jnp.exp(b_end - b), v)
    o_ref[...] = _silu(r_ref[...]) * _head_norm(o, gn_ref[...])

    @pl.when(c == nc - 1)
    def _():
        so_ref[0, 0] = ss[...]


def _gla_cell(qkvr, la, levels, gn, n_seq, seq_len, chunk, row0, state=None, gated_in=None):
    t_all = qkvr.shape[0]
    nc = seq_len // chunk
    blk0 = row0 // chunk
    has_state = state is not None
    kq = GLA_KT // GLA_DK
    kv = 2 * GLA_KT // GLA_DV
    row = lambda b, h, c: blk0 + b * nc + c
    in_specs = [pl.BlockSpec((chunk, GLA_DK), lambda b, h, c: (row(b, h, c), h)),
                pl.BlockSpec((chunk, GLA_DK), lambda b, h, c: (row(b, h, c), kq + h)),
                pl.BlockSpec((chunk, GLA_DV), lambda b, h, c: (row(b, h, c), kv + h)),
                pl.BlockSpec((chunk, GLA_DV), lambda b, h, c: (row(b, h, c), kv + GLA_HEADS + h)),
                pl.BlockSpec((chunk, GLA_DK), lambda b, h, c: (row(b, h, c), h)),
                pl.BlockSpec(levels.shape, lambda b, h, c: (0, 0)),
                pl.BlockSpec((1, GLA_DV), lambda b, h, c: (0, h))]
    args = [qkvr, qkvr, qkvr, qkvr, la, levels, gn]
    aliases = {}
    if has_state:
        in_specs += [pl.BlockSpec((1, 1, GLA_DK, GLA_DV), lambda b, h, c: (b, h, 0, 0)),
                     pl.BlockSpec(memory_space=pl.ANY)]
        args += [state, gated_in]
        aliases = {len(args) - 1: 0}
    return pl.pallas_call(
        functools.partial(_gla_cell_body, chunk=chunk, has_state=has_state),
        grid=(n_seq, GLA_HEADS, nc),
        in_specs=in_specs,
        out_specs=[pl.BlockSpec((chunk, GLA_DV), lambda b, h, c: (row(b, h, c), h)),
                   pl.BlockSpec((1, 1, GLA_DK, GLA_DV), lambda b, h, c: (b, h, 0, 0))],
        out_shape=[jax.ShapeDtypeStruct((t_all, GLA_VT), F32),
                   jax.ShapeDtypeStruct((n_seq, GLA_HEADS, GLA_DK, GLA_DV), F32)],
        scratch_shapes=[pltpu.VMEM((GLA_DK, GLA_DV), F32)],
        input_output_aliases=aliases,
        compiler_params=_params(("arbitrary", "arbitrary", "arbitrary"), 32),
        name="gla_cell_state" if has_state else "gla_cell",
    )(*args)


def _gla_weights(w_in, w_a2, b_a, gn_g, w_out):
    n1 = 2 * GLA_KT + 2 * GLA_VT
    w_a = jnp.pad(w_in[:, n1:], ((0, 0), (0, V7X_LANES - GLA_RANK)))
    w_a2p = jnp.pad(w_a2, ((0, V7X_LANES - GLA_RANK), (0, 0)))
    return dict(w_qkvr=w_in[:, :n1].astype(BF16), w_a=w_a.astype(BF16), w_a2=w_a2p.astype(BF16),
                b_a=b_a.reshape(1, GLA_KT), gn=gn_g.reshape(1, GLA_VT), w_out=w_out.astype(BF16))


def _pad_levels(levels, chunk):
    if chunk >= V7X_LANES:
        return levels
    return jnp.pad(levels, ((0, 0), (0, V7X_LANES - chunk)))


def _gla_layer(x_all, w, dims, state_s, ln_g, ln_b):
    n_p, seq_len, n_s, sample_len = dims
    t_p = n_p * seq_len
    qkvr, la = _gla_in(x_all, w)
    chunk = min(GLA_CHUNK, seq_len)
    gated, p_s = _gla_cell(qkvr, la, _pad_levels(_gla_levels(chunk), chunk), w["gn"], n_p, seq_len, chunk, 0)
    gated, s_s = _gla_cell(qkvr, la, _pad_levels(_gla_levels(sample_len), sample_len), w["gn"], n_s, sample_len,
                           sample_len, t_p, state=state_s, gated_in=gated)
    h1 = _mm_ln(gated, w["w_out"], x_all, ln_g, ln_b)
    return h1, p_s, s_s


def _s5_body(x_ref, sre_ref, sim_ref, win_ref, bt_ref, cre_ref, cim_ref, lre_ref, lim_ref, d_ref, wglu_ref,
             g_ref, b_ref, o_ref, ore_ref, oim_ref, st_re, st_im, car_re, car_im, *, nb, lc):
    c = pl.program_id(1)
    nc = pl.num_programs(1)
    rows = nb * lc
    ntile = D_MODEL // S5_TILE_CH

    @pl.when(c == 0)
    def _():
        car_re[...] = sre_ref[...]
        car_im[...] = sim_ref[...]

    x = x_ref[...].reshape(rows, D_MODEL)
    u = _dot(x, win_ref[...])
    lpt = S5_TILE_ST // V7X_LANES
    for t in range(ntile):
        bu = _dot(u[:, t * S5_TILE_CH:(t + 1) * S5_TILE_CH], bt_ref[t])
        for j in range(lpt):
            st_re[t * lpt + j] = bu[:, j * V7X_LANES:(j + 1) * V7X_LANES]
            st_im[t * lpt + j] = bu[:, S5_TILE_ST + j * V7X_LANES:S5_TILE_ST + (j + 1) * V7X_LANES]

    for gidx in range(nb // V7X_SUBLANES):
        base = gidx * V7X_SUBLANES * lc
        grp = slice(gidx * V7X_SUBLANES, (gidx + 1) * V7X_SUBLANES)
        for t in range(ntile):
            slabs = list(range(t * lpt, (t + 1) * lpt))
            lanes = [slice(j * V7X_LANES, (j + 1) * V7X_LANES) for j in slabs]
            lre = [lre_ref[:, ln] for ln in lanes]
            lim = [lim_ref[:, ln] for ln in lanes]

            def step(tok, carry, slabs=slabs, lre=lre, lim=lim, base=base):
                sel = pl.ds(base + tok, V7X_SUBLANES, stride=lc)
                out = []
                for n, j in enumerate(slabs):
                    pr, pi = carry[2 * n], carry[2 * n + 1]
                    nr = lre[n] * pr - lim[n] * pi + st_re[j, sel, :]
                    ni = lre[n] * pi + lim[n] * pr + st_im[j, sel, :]
                    st_re[j, sel, :] = nr
                    st_im[j, sel, :] = ni
                    out += [nr, ni]
                return tuple(out)

            init = []
            for ln in lanes:
                init += [car_re[grp, ln], car_im[grp, ln]]
            fin = lax.fori_loop(0, lc, step, tuple(init))
            for n, ln in enumerate(lanes):
                car_re[grp, ln] = fin[2 * n]
                car_im[grp, ln] = fin[2 * n + 1]

    ys = []
    for t in range(ntile):
        sre = jnp.concatenate([st_re[t * lpt + j] for j in range(lpt)], axis=1)
        sim = jnp.concatenate([st_im[t * lpt + j] for j in range(lpt)], axis=1)
        ys.append(_dot(sre, cre_ref[t]) - _dot(sim, cim_ref[t]))
    y = jnp.concatenate(ys, axis=1) + d_ref[...] * u
    y = jax.nn.gelu(y)
    vg = _dot(y, wglu_ref[...])
    mix = vg[:, :D_MODEL] * _sigmoid(vg[:, D_MODEL:])
    o_ref[...] = _layer_norm(DEEPNORM_ALPHA * x + mix, g_ref[...], b_ref[...]).reshape(nb, lc, D_MODEL)

    @pl.when(c == nc - 1)
    def _():
        ore_ref[...] = car_re[...]
        oim_ref[...] = car_im[...]


def _s5_call(x3, s_re, s_im, w, ln_g, ln_b, nb, lc):
    n_seq, seq_len, _ = x3.shape
    ntile = D_MODEL // S5_TILE_CH
    rows = nb * lc
    full2 = lambda shape: pl.BlockSpec(shape, lambda i, c: (0, 0))
    full3 = lambda shape: pl.BlockSpec(shape, lambda i, c: (0, 0, 0))
    st_spec = pl.BlockSpec((nb, S5_STATE), lambda i, c: (i, 0))
    return pl.pallas_call(
        functools.partial(_s5_body, nb=nb, lc=lc),
        grid=(n_seq // nb, seq_len // lc),
        in_specs=[pl.BlockSpec((nb, lc, D_MODEL), lambda i, c: (i, c, 0)), st_spec, st_spec,
                  full2((D_MODEL, D_MODEL)), full3((ntile, S5_TILE_CH, 2 * S5_TILE_ST)),
                  full3((ntile, S5_TILE_ST, S5_TILE_CH)), full3((ntile, S5_TILE_ST, S5_TILE_CH)),
                  full2((V7X_SUBLANES, S5_STATE)), full2((V7X_SUBLANES, S5_STATE)), full2((1, D_MODEL)),
                  full2((D_MODEL, 2 * D_MODEL)), full2((1, D_MODEL)), full2((1, D_MODEL))],
        out_specs=[pl.BlockSpec((nb, lc, D_MODEL), lambda i, c: (i, c, 0)), st_spec, st_spec],
        out_shape=[jax.ShapeDtypeStruct(x3.shape, F32), jax.ShapeDtypeStruct((n_seq, S5_STATE), F32),
                   jax.ShapeDtypeStruct((n_seq, S5_STATE), F32)],
        scratch_shapes=[pltpu.VMEM((S5_STATE // V7X_LANES, rows, V7X_LANES), F32),
                        pltpu.VMEM((S5_STATE // V7X_LANES, rows, V7X_LANES), F32),
                        pltpu.VMEM((nb, S5_STATE), F32), pltpu.VMEM((nb, S5_STATE), F32)],
        compiler_params=_params(("arbitrary", "arbitrary"), 56),
        name="s5",
    )(x3, s_re, s_im, w["w_in"], w["b_tiles"], w["c_re"], w["c_im"], w["lam_re"], w["lam_im"], w["d"], w["w_glu"],
      ln_g.reshape(1, D_MODEL), ln_b.reshape(1, D_MODEL))


def _s5_weights(w_in, a_re, a_im, log_dt, b_re, b_im, c_re, c_im, d_skip, w_glu):
    lam = lax.complex(a_re.astype(F32), a_im.astype(F32))
    dt = jnp.exp(log_dt.astype(F32))[:, None]
    lam_bar = jnp.exp(lam * dt)
    b_bar = ((lam_bar - 1.0) / lam)[..., None] * lax.complex(b_re.astype(F32), b_im.astype(F32))
    ntile = D_MODEL // S5_TILE_CH
    gpt = S5_TILE_CH // S5_GC
    eye = jnp.eye(gpt, dtype=F32)

    def b_tiles(bb):
        return jnp.einsum("igpc,gh->igchp", bb.reshape(ntile, gpt, S5_P, S5_GC), eye).reshape(ntile, S5_TILE_CH, S5_TILE_ST)

    def c_tiles(cc):
        return jnp.einsum("igcp,gh->igphc", cc.reshape(ntile, gpt, S5_GC, S5_P), eye).reshape(ntile, S5_TILE_ST, S5_TILE_CH)

    bt = jnp.concatenate([b_tiles(b_bar.real), b_tiles(b_bar.imag)], axis=2).astype(BF16)
    bcast = lambda a: jnp.broadcast_to(a.reshape(1, S5_STATE), (V7X_SUBLANES, S5_STATE))
    return dict(w_in=w_in.astype(BF16), b_tiles=bt, c_re=c_tiles(c_re.astype(F32)).astype(BF16),
                c_im=c_tiles(c_im.astype(F32)).astype(BF16), lam_re=bcast(lam_bar.real), lam_im=bcast(lam_bar.imag),
                d=d_skip.reshape(1, D_MODEL).astype(F32), w_glu=w_glu.astype(BF16))


def _s5_layer(x_all, w, dims, state_re, state_im, ln_g, ln_b):
    n_p, seq_len, n_s, sample_len = dims
    t_p = n_p * seq_len
    xp = x_all[:t_p].reshape(n_p, seq_len, D_MODEL)
    xs = x_all[t_p:].reshape(n_s, sample_len, D_MODEL)
    zero = jnp.zeros((n_p, S5_STATE), F32)
    hp, p_re, p_im = _s5_call(xp, zero, zero, w, ln_g, ln_b, n_p, min(S5_CHUNK, seq_len))
    nb_s = min(n_s, ROW_BLOCK // sample_len)
    hs, s_re, s_im = _s5_call(xs, state_re.reshape(n_s, S5_STATE), state_im.reshape(n_s, S5_STATE), w, ln_g, ln_b,
                              nb_s, sample_len)
    h1 = jnp.concatenate([hp.reshape(t_p, D_MODEL), hs.reshape(n_s * sample_len, D_MODEL)], axis=0)
    shp = lambda a, n: a.reshape(n, S5_GROUPS, S5_P)
    return h1, (shp(p_re, n_p), shp(p_im, n_p)), (shp(s_re, n_s), shp(s_im, n_s))


def _route_body(x_ref, wh_ref, wl_ref, br_ref, g_ref):
    x = x_ref[...]
    xh = x.astype(BF16)
    xl = (x - xh.astype(F32)).astype(BF16)
    wh = wh_ref[...]
    logits = (jnp.dot(xh, wh, preferred_element_type=F32) + jnp.dot(xl, wh, preferred_element_type=F32)
              + jnp.dot(xh, wl_ref[...], preferred_element_type=F32))
    s = _sigmoid(logits)
    work = s + br_ref[...]
    lane = lax.broadcasted_iota(jnp.int32, s.shape, 1).astype(F32)
    chosen = jnp.zeros(s.shape, jnp.bool_)
    for _ in range(TOP_K):
        mx = jnp.max(work, axis=1, keepdims=True)
        idx = jnp.min(jnp.where(work == mx, lane, float(N_EXPERTS)), axis=1, keepdims=True)
        hit = lane == idx
        chosen = jnp.logical_or(chosen, hit)
        work = jnp.where(hit, -jnp.inf, work)
    sel = jnp.where(chosen, s, 0.0)
    g_ref[...] = sel / jnp.sum(sel, axis=1, keepdims=True) * ROUTE_SCALE


def _route(x, w):
    m, k = x.shape
    tm = ROW_BLOCK
    full = lambda shape: pl.BlockSpec(shape, lambda i: (0, 0))
    return pl.pallas_call(
        _route_body,
        grid=(m // tm,),
        in_specs=[pl.BlockSpec((tm, k), lambda i: (i, 0)), full((k, N_EXPERTS)), full((k, N_EXPERTS)),
                  full((1, N_EXPERTS))],
        out_specs=pl.BlockSpec((tm, N_EXPERTS), lambda i: (i, 0)),
        out_shape=jax.ShapeDtypeStruct((m, N_EXPERTS), F32),
        compiler_params=_params(("arbitrary",), 32),
        name="route",
    )(x, w["wr_hi"], w["wr_lo"], w["b_router"])


def _moe_body(x_ref, gd_ref, wg_ref, wu_ref, wd_ref, sg_ref, su_ref, sd_ref, g_ref, b_ref, o_ref, acc):
    e = pl.program_id(1)
    ne = pl.num_programs(1)
    xb = x_ref[...].astype(BF16)

    @pl.when(e == 0)
    def _():
        hs = _silu(jnp.dot(xb, sg_ref[...], preferred_element_type=F32)) * jnp.dot(xb, su_ref[...], preferred_element_type=F32)
        acc[...] = _dot(hs, sd_ref[...])

    h = _silu(jnp.dot(xb, wg_ref[0], preferred_element_type=F32)) * jnp.dot(xb, wu_ref[0], preferred_element_type=F32)
    gd = gd_ref[...]
    lane = lax.broadcasted_iota(jnp.int32, gd.shape, 1)
    gate = jnp.sum(jnp.where(lane == e, gd, 0.0), axis=1, keepdims=True)
    acc[...] += _dot(h, wd_ref[0]) * gate

    @pl.when(e == ne - 1)
    def _():
        o_ref[...] = _layer_norm(DEEPNORM_ALPHA * x_ref[...] + acc[...], g_ref[...], b_ref[...])


def _moe_ffn(x, gates, w, ln_g, ln_b):
    m, k = x.shape
    tm = MOE_ROWS
    full = lambda shape: pl.BlockSpec(shape, lambda i, e: (0, 0))
    return pl.pallas_call(
        _moe_body,
        grid=(m // tm, N_EXPERTS),
        in_specs=[pl.BlockSpec((tm, k), lambda i, e: (i, 0)), pl.BlockSpec((tm, N_EXPERTS), lambda i, e: (i, 0)),
                  pl.BlockSpec((1, k, EXPERT_FF), lambda i, e: (e, 0, 0)),
                  pl.BlockSpec((1, k, EXPERT_FF), lambda i, e: (e, 0, 0)),
                  pl.BlockSpec((1, EXPERT_FF, k), lambda i, e: (e, 0, 0)),
                  full((k, w["ws_gate"].shape[1])), full((k, w["ws_up"].shape[1])), full((w["ws_down"].shape[0], k)),
                  full((1, k)), full((1, k))],
        out_specs=pl.BlockSpec((tm, k), lambda i, e: (i, 0)),
        out_shape=jax.ShapeDtypeStruct((m, k), F32),
        scratch_shapes=[pltpu.VMEM((tm, k), F32)],
        compiler_params=_params(("arbitrary", "arbitrary"), 48),
        name="moe_ffn",
    )(x, gates, w["w_gate"], w["w_up"], w["w_down"], w["ws_gate"], w["ws_up"], w["ws_down"],
      ln_g.reshape(1, k), ln_b.reshape(1, k))


def _moe_weights(w_router, b_router, w_gate, w_up, w_down, ws_gate, ws_up, ws_down):
    wr_hi = w_router.astype(BF16)
    wr_lo = (w_router - wr_hi.astype(F32)).astype(BF16)
    return dict(wr_hi=wr_hi, wr_lo=wr_lo, b_router=b_router.reshape(1, N_EXPERTS).astype(F32),
                w_gate=w_gate.astype(BF16), w_up=w_up.astype(BF16), w_down=w_down.astype(BF16),
                ws_gate=ws_gate.astype(BF16), ws_up=ws_up.astype(BF16), ws_down=ws_down.astype(BF16))


def _moe_layer(x, w, ln_g, ln_b):
    return _moe_ffn(x, _route(x, w), w, ln_g, ln_b)


def kernel(x_prompt, x_sample, state_mlstm_C, state_mlstm_n, state_mlstm_m, state_mlstm_conv, state_gla_S, state_s5_re, state_s5_im, ln1_g, ln1_b, ln2_g, ln2_b, ml_w_in, ml_conv_w, ml_conv_b, ml_wq, ml_wk, ml_wv, ml_w_i, ml_b_i, ml_w_f, ml_b_f, ml_gn_g, ml_skip, ml_w_out, gla_w_in, gla_w_a2, gla_b_a, gla_gn_g, gla_w_out, s5_w_in, s5_a_re, s5_a_im, s5_log_dt, s5_b_re, s5_b_im, s5_c_re, s5_c_im, s5_d, s5_w_glu, moe_w_router, moe_b_router, moe_w_gate, moe_w_up, moe_w_down, moe_ws_gate, moe_ws_up, moe_ws_down):
    n_p, seq_len, _ = x_prompt.shape
    n_s, sample_len, _ = x_sample.shape
    dims = (n_p, seq_len, n_s, sample_len)
    t_p = n_p * seq_len
    t_s = n_s * sample_len
    assert seq_len % ROW_BLOCK == 0 and t_s % ROW_BLOCK == 0 and ROW_BLOCK % sample_len == 0
    assert sample_len == V7X_SUBLANES and (t_p + t_s) % MOE_ROWS == 0
    x = jnp.concatenate([x_prompt.reshape(t_p, D_MODEL), x_sample.reshape(t_s, D_MODEL)], axis=0)
    p_ml, s_ml, p_gla, s_gla, p_s5, s_s5 = [], [], [], [], [], []
    for i in range(DEPTH):
        j = i // N_MIXERS
        if i % N_MIXERS == 0:
            w = _ml_weights(ml_w_in[j], ml_conv_w[j], ml_conv_b[j], ml_wq[j], ml_wk[j], ml_wv[j], ml_w_i[j], ml_b_i[j],
                            ml_w_f[j], ml_b_f[j], ml_gn_g[j], ml_skip[j], ml_w_out[j])
            x, ps, ss = _ml_layer(x, w, dims, state_mlstm_C[j], state_mlstm_n[j], state_mlstm_m[j],
                                  state_mlstm_conv[j], ln1_g[i], ln1_b[i])
            p_ml.append(ps)
            s_ml.append(ss)
        elif i % N_MIXERS == 1:
            w = _gla_weights(gla_w_in[j], gla_w_a2[j], gla_b_a[j], gla_gn_g[j], gla_w_out[j])
            x, ps, ss = _gla_layer(x, w, dims, state_gla_S[j], ln1_g[i], ln1_b[i])
            p_gla.append(ps)
            s_gla.append(ss)
        else:
            w = _s5_weights(s5_w_in[j], s5_a_re[j], s5_a_im[j], s5_log_dt[j], s5_b_re[j], s5_b_im[j], s5_c_re[j],
                            s5_c_im[j], s5_d[j], s5_w_glu[j])
            x, ps, ss = _s5_layer(x, w, dims, state_s5_re[j], state_s5_im[j], ln1_g[i], ln1_b[i])
            p_s5.append(ps)
            s_s5.append(ss)
        wm = _moe_weights(moe_w_router[i], moe_b_router[i], moe_w_gate[i], moe_w_up[i], moe_w_down[i],
                          moe_ws_gate[i], moe_ws_up[i], moe_ws_down[i])
        x = _moe_layer(x, wm, ln2_g[i], ln2_b[i])
    stack = lambda items, idx: jnp.stack([it[idx] for it in items])
    return (x[:t_p].reshape(n_p, seq_len, D_MODEL), x[t_p:].reshape(n_s, sample_len, D_MODEL),
            stack(p_ml, 0), stack(p_ml, 1), stack(p_ml, 2), stack(p_ml, 3), jnp.stack(p_gla),
            stack(p_s5, 0), stack(p_s5, 1),
            stack(s_ml, 0), stack(s_ml, 1), stack(s_ml, 2), stack(s_ml, 3), jnp.stack(s_gla),
            stack(s_s5, 0), stack(s_s5, 1))
```

```python
import functools

import jax
import jax.numpy as jnp
from jax import lax
from jax.experimental import pallas as pl
from jax.experimental.pallas import tpu as pltpu

F32 = jnp.float32
BF16 = jnp.bfloat16

D_MODEL = 1024
DEPTH = 4
N_MIXERS = 3
ML_INNER = 2 * D_MODEL
ML_HEADS = 4
ML_DH = ML_INNER // ML_HEADS
ML_QKV_BLOCK = 4
ML_CONV = 4
GLA_HEADS = 4
GLA_KT = D_MODEL // 2
GLA_VT = D_MODEL
GLA_DK = GLA_KT // GLA_HEADS
GLA_DV = GLA_VT // GLA_HEADS
GLA_RANK = 16
GLA_TAU = 16.0
S5_GC = 16
S5_GROUPS = D_MODEL // S5_GC
S5_P = 64
S5_STATE = S5_GROUPS * S5_P
N_EXPERTS = 64
TOP_K = 8
EXPERT_FF = 256
ROUTE_SCALE = 2.5
DEEPNORM_ALPHA = (2.0 * DEPTH) ** 0.25
NORM_EPS = 1e-5

V7X_VMEM_BYTES = 64 * 1024 * 1024
V7X_LANES = 128
V7X_SUBLANES = 8
V7X_MXU_DIM = 256

ROW_BLOCK = 256
ML_CHUNK = 256
GLA_CHUNK = 128
S5_CHUNK = 32
S5_TILE_CH = 128
S5_TILE_ST = S5_TILE_CH // S5_GC * S5_P
MOE_ROWS = 1024


def _params(semantics, vmem_mb):
    assert vmem_mb * 1024 * 1024 < V7X_VMEM_BYTES
    return pltpu.CompilerParams(dimension_semantics=semantics, vmem_limit_bytes=vmem_mb * 1024 * 1024)


def _dot(a, b):
    return jnp.dot(a.astype(BF16), b.astype(BF16), preferred_element_type=F32)


def _dot_nt(a, b):
    return lax.dot_general(a.astype(BF16), b.astype(BF16), (((1,), (1,)), ((), ())), preferred_element_type=F32)


def _dot_tn(a, b):
    return lax.dot_general(a.astype(BF16), b.astype(BF16), (((0,), (0,)), ((), ())), preferred_element_type=F32)


def _sigmoid(x):
    return 1.0 / (1.0 + jnp.exp(-x))


def _silu(x):
    return x * _sigmoid(x)


def _log_sigmoid(x):
    return jnp.minimum(x, 0.0) - jnp.log1p(jnp.exp(-jnp.abs(x)))


def _layer_norm(y, g, b):
    mu = jnp.mean(y, axis=-1, keepdims=True)
    d = y - mu
    var = jnp.mean(d * d, axis=-1, keepdims=True)
    return d * lax.rsqrt(var + NORM_EPS) * g + b


def _head_norm(h, g):
    mu = jnp.mean(h, axis=-1, keepdims=True)
    d = h - mu
    var = jnp.mean(d * d, axis=-1, keepdims=True)
    return d * lax.rsqrt(var + NORM_EPS) * g


def _split3(x):
    hi = x.astype(BF16)
    r1 = x - hi.astype(F32)
    mid = r1.astype(BF16)
    lo = (r1 - mid.astype(F32)).astype(BF16)
    return hi, mid, lo


def _mm_body(x_ref, w_ref, o_ref):
    o_ref[...] = _dot(x_ref[...], w_ref[...])


def _mm(x, w, tn):
    m, k = x.shape
    n = w.shape[1]
    tm = ROW_BLOCK
    return pl.pallas_call(
        _mm_body,
        grid=(n // tn, m // tm),
        in_specs=[pl.BlockSpec((tm, k), lambda j, i: (i, 0)), pl.BlockSpec((k, tn), lambda j, i: (0, j))],
        out_specs=pl.BlockSpec((tm, tn), lambda j, i: (i, j)),
        out_shape=jax.ShapeDtypeStruct((m, n), F32),
        compiler_params=_params(("arbitrary", "arbitrary"), 40),
        name="mm",
    )(x, w)


def _mm_ln_body(a_ref, w_ref, r_ref, g_ref, b_ref, o_ref):
    y = _dot(a_ref[...], w_ref[...])
    o_ref[...] = _layer_norm(DEEPNORM_ALPHA * r_ref[...] + y, g_ref[...], b_ref[...])


def _mm_ln(a, w, resid, g, b):
    m, k = a.shape
    n = w.shape[1]
    tm = ROW_BLOCK
    return pl.pallas_call(
        _mm_ln_body,
        grid=(m // tm,),
        in_specs=[pl.BlockSpec((tm, k), lambda i: (i, 0)), pl.BlockSpec((k, n), lambda i: (0, 0)),
                  pl.BlockSpec((tm, n), lambda i: (i, 0)), pl.BlockSpec((1, n), lambda i: (0, 0)),
                  pl.BlockSpec((1, n), lambda i: (0, 0))],
        out_specs=pl.BlockSpec((tm, n), lambda i: (i, 0)),
        out_shape=jax.ShapeDtypeStruct((m, n), F32),
        compiler_params=_params(("arbitrary",), 40),
        name="mm_ln",
    )(a, w, resid, g.reshape(1, n), b.reshape(1, n))


def _ml_pre_body(x_ref, prev_ref, p_ref, cw_ref, cb_ref, wq_ref, wk_ref, wv_ref, wgq_ref, wgk_ref, wgv_ref, gb_ref,
                 q_ref, k_ref, v_ref, xc_ref, g_ref, scx, scp, *, n_prompt_blocks, blocks_per_seq, sample_len):
    i = pl.program_id(0)
    rows = x_ref.shape[0]
    is_prompt = i < n_prompt_blocks
    no_prev = jnp.logical_or(i % blocks_per_seq == 0, jnp.logical_not(is_prompt))
    x = x_ref[...]
    scx[0:V7X_SUBLANES, :] = jnp.where(no_prev, 0.0, prev_ref[...])
    scx[V7X_SUBLANES:V7X_SUBLANES + rows, :] = x
    scp[0:rows, :] = p_ref[...]
    scp[rows:rows + V7X_SUBLANES, :] = jnp.zeros((V7X_SUBLANES, x.shape[1]), F32)
    r = lax.broadcasted_iota(jnp.int32, (rows, 1), 0)
    tpos = jnp.where(is_prompt, (i % blocks_per_seq) * rows + r, r % sample_len)
    acc = cb_ref[...] + x * cw_ref[ML_CONV - 1:ML_CONV, :]
    for s in range(1, ML_CONV):
        xs = scx[V7X_SUBLANES - s:V7X_SUBLANES - s + rows, :]
        ps = scp[ML_CONV - 1 - s:ML_CONV - 1 - s + rows, :]
        acc = acc + jnp.where(tpos >= s, xs, ps) * cw_ref[ML_CONV - 1 - s:ML_CONV - s, :]
    xc = _silu(acc)
    xc_ref[...] = xc
    g = jnp.zeros((2 * ML_HEADS, rows), F32)
    for t in range(ML_INNER // V7X_MXU_DIM):
        sl = slice(t * V7X_MXU_DIM, (t + 1) * V7X_MXU_DIM)
        q_t = _dot(xc[:, sl], wq_ref[t])
        k_t = _dot(xc[:, sl], wk_ref[t])
        v_t = _dot(x[:, sl], wv_ref[t])
        q_ref[:, sl] = q_t
        k_ref[:, sl] = k_t * (ML_DH ** -0.5)
        v_ref[:, sl] = v_t
        g = g + _dot_nt(wgq_ref[:, sl], q_t) + _dot_nt(wgk_ref[:, sl], k_t) + _dot_nt(wgv_ref[:, sl], v_t)
    g = g + gb_ref[...]
    gate_row = lax.broadcasted_iota(jnp.int32, g.shape, 0)
    g_ref[0] = jnp.where(gate_row < ML_HEADS, g, _log_sigmoid(g))


def _ml_pre(xmz, conv_rows, w, n_prompt_rows, seq_len, sample_len):
    t_all = xmz.shape[0]
    rb = ROW_BLOCK
    nblk = t_all // rb
    npb = n_prompt_rows // rb
    bps = seq_len // rb
    sub = rb // V7X_SUBLANES
    body = functools.partial(_ml_pre_body, n_prompt_blocks=npb, blocks_per_seq=bps, sample_len=sample_len)
    full2 = lambda shape: pl.BlockSpec(shape, lambda i: (0, 0))
    full3 = lambda shape: pl.BlockSpec(shape, lambda i: (0, 0, 0))
    nt = ML_INNER // V7X_MXU_DIM
    row_spec = pl.BlockSpec((rb, ML_INNER), lambda i: (i, 0))
    return pl.pallas_call(
        body,
        grid=(nblk,),
        in_specs=[row_spec,
                  pl.BlockSpec((V7X_SUBLANES, ML_INNER), lambda i: (jnp.maximum(i * sub - 1, 0), 0)),
                  pl.BlockSpec((rb, ML_INNER), lambda i: (jnp.maximum(i - npb + 1, 0), 0)),
                  full2((ML_CONV, ML_INNER)), full2((1, ML_INNER)),
                  full3((nt, V7X_MXU_DIM, V7X_MXU_DIM)), full3((nt, V7X_MXU_DIM, V7X_MXU_DIM)),
                  full3((nt, V7X_MXU_DIM, V7X_MXU_DIM)),
                  full2((2 * ML_HEADS, ML_INNER)), full2((2 * ML_HEADS, ML_INNER)), full2((2 * ML_HEADS, ML_INNER)),
                  full2((2 * ML_HEADS, 1))],
        out_specs=[row_spec, row_spec, row_spec, row_spec,
                   pl.BlockSpec((1, 2 * ML_HEADS, rb), lambda i: (i, 0, 0))],
        out_shape=[jax.ShapeDtypeStruct((t_all, ML_INNER), F32)] * 4
        + [jax.ShapeDtypeStruct((nblk, 2 * ML_HEADS, rb), F32)],
        scratch_shapes=[pltpu.VMEM((rb + V7X_SUBLANES, ML_INNER), F32), pltpu.VMEM((rb + V7X_SUBLANES, ML_INNER), F32)],
        compiler_params=_params(("arbitrary",), 48),
        name="ml_pre",
    )(xmz, xmz, conv_rows, w["conv_w"], w["conv_b"], w["wq"], w["wk"], w["wv"], w["wgq"], w["wgk"], w["wgv"], w["gb"])


def _ml_chunk(q, k, v, ip, fl, c_prev, n_prev, m_prev):
    L = q.shape[0]
    ri = lax.broadcasted_iota(jnp.int32, (L, L), 0)
    ci = lax.broadcasted_iota(jnp.int32, (L, L), 1)
    eye = ri == ci
    tril = ci <= ri
    f_col = jnp.sum(jnp.where(eye, fl, 0.0), axis=1, keepdims=True)
    b_col = jnp.sum(jnp.where(tril, fl, 0.0), axis=1, keepdims=True)
    b_row = jnp.sum(jnp.where(ri <= ci, f_col, 0.0), axis=0, keepdims=True)
    ib = ip - b_row
    d = jnp.where(tril, b_col + ib, -jnp.inf)
    m_inter = b_col + m_prev
    m_t = jnp.maximum(m_inter, jnp.max(d, axis=1, keepdims=True))
    a_inter = jnp.exp(m_inter - m_t)
    s = _dot_nt(q, k) * jnp.exp(d - m_t)
    num = _dot(s, v) + a_inter * _dot(q, c_prev)
    den = jnp.sum(s, axis=1, keepdims=True) + a_inter * jnp.sum(q * n_prev, axis=1, keepdims=True)
    hc = num / jnp.maximum(jnp.abs(den), jnp.exp(-m_t))
    m_new = m_t[L - 1:L, :]
    b_last = b_row[:, L - 1:L]
    w_row = jnp.exp(b_last + ib - m_new)
    w_col = jnp.sum(jnp.where(eye, w_row, 0.0), axis=1, keepdims=True)
    a_end = jnp.exp(b_last + m_prev - m_new)
    kw = k * w_col
    c_new = a_end * c_prev + _dot_tn(kw, v)
    n_new = a_end * n_prev + jnp.sum(kw, axis=0, keepdims=True)
    return hc, c_new, n_new, m_new


def _ml_gate_out(hc, gn, sk, xc, z):
    return (_head_norm(hc, gn) + sk * xc) * _silu(z)


def _ml_cell_body(q_ref, k_ref, v_ref, xc_ref, z_ref, g_ref, gn_ref, sk_ref, o_ref, co_ref, no_ref, mo_ref, cs, ns, ms):
    h = pl.program_id(1)
    c = pl.program_id(2)
    nc = pl.num_programs(2)

    @pl.when(c == 0)
    def _():
        cs[...] = jnp.zeros(cs.shape, F32)
        ns[...] = jnp.zeros(ns.shape, F32)
        ms[...] = jnp.zeros(ms.shape, F32)

    ip = g_ref[0, pl.ds(h, 1), :]
    fl = g_ref[0, pl.ds(ML_HEADS + h, 1), :]
    hc, c_new, n_new, m_new = _ml_chunk(q_ref[...], k_ref[...], v_ref[...], ip, fl, cs[...], ns[...], ms[...])
    cs[...] = c_new
    ns[...] = n_new
    ms[...] = m_new
    o_ref[...] = _ml_gate_out(hc, gn_ref[...], sk_ref[...], xc_ref[...], z_ref[...])

    @pl.when(c == nc - 1)
    def _():
        co_ref[0, 0] = cs[...]
        no_ref[0] = ns[...]
        mo_ref[0] = ms[...]


def _ml_cell(q, k, v, xc, xmz, gates, gn, skip, n_seq, seq_len, chunk):
    t_all = q.shape[0]
    nc = seq_len // chunk
    head_spec = pl.BlockSpec((chunk, ML_DH), lambda b, h, c: (b * nc + c, h))
    state_map = lambda b, h, c: (b * ML_HEADS + h, 0, 0)
    out = pl.pallas_call(
        _ml_cell_body,
        grid=(n_seq, ML_HEADS, nc),
        in_specs=[head_spec, head_spec, head_spec, head_spec,
                  pl.BlockSpec((chunk, ML_DH), lambda b, h, c: (b * nc + c, ML_HEADS + h)),
                  pl.BlockSpec((1, 2 * ML_HEADS, chunk), lambda b, h, c: (b * nc + c, 0, 0)),
                  pl.BlockSpec((1, ML_DH), lambda b, h, c: (0, h)),
                  pl.BlockSpec((1, ML_DH), lambda b, h, c: (0, h))],
        out_specs=[head_spec,
                   pl.BlockSpec((1, 1, ML_DH, ML_DH), lambda b, h, c: (b, h, 0, 0)),
                   pl.BlockSpec((1, 1, ML_DH), state_map),
                   pl.BlockSpec((1, 1, 1), state_map)],
        out_shape=[jax.ShapeDtypeStruct((t_all, ML_INNER), F32),
                   jax.ShapeDtypeStruct((n_seq, ML_HEADS, ML_DH, ML_DH), F32),
                   jax.ShapeDtypeStruct((n_seq * ML_HEADS, 1, ML_DH), F32),
                   jax.ShapeDtypeStruct((n_seq * ML_HEADS, 1, 1), F32)],
        scratch_shapes=[pltpu.VMEM((ML_DH, ML_DH), F32), pltpu.VMEM((1, ML_DH), F32), pltpu.VMEM((1, 1), F32)],
        compiler_params=_params(("arbitrary", "arbitrary", "arbitrary"), 40),
        name="ml_cell",
    )(q, k, v, xc, xmz, gates, gn, skip)
    gated, c_new, n_new, m_new = out
    return gated, c_new, n_new.reshape(n_seq, ML_HEADS, ML_DH), m_new.reshape(n_seq, ML_HEADS)


def _ml_cell_sample_body(*refs):
    q_ref, k_ref, v_ref, xc_ref, z_ref, g_ref, gn_ref, sk_ref, c0_ref, n0_ref, m0_ref = refs[:11]
    o_ref, co_ref, no_ref, mo_ref = refs[-4:]
    for h in range(ML_HEADS):
        sl = slice(h * ML_DH, (h + 1) * ML_DH)
        hc, c_new, n_new, m_new = _ml_chunk(
            q_ref[:, sl], k_ref[:, sl], v_ref[:, sl], g_ref[0, h:h + 1, :], g_ref[0, ML_HEADS + h:ML_HEADS + h + 1, :],
            c0_ref[0, h], n0_ref[0, h:h + 1, :], m0_ref[0, h:h + 1, :])
        o_ref[:, sl] = _ml_gate_out(hc, gn_ref[:, sl], sk_ref[:, sl], xc_ref[:, sl], z_ref[:, sl])
        co_ref[0, h] = c_new
        no_ref[0, h:h + 1, :] = n_new
        mo_ref[0, h:h + 1, :] = m_new


def _ml_cell_sample(q, k, v, xc, xmz, gates, gn, skip, n_seq, seq_len, row0, c_all, n_all, m_all, layer, n_layers,
                    gated_in, c_out_prev):
    t_all = q.shape[0]
    blk0 = row0 // seq_len
    s0 = layer * n_seq
    row_spec = pl.BlockSpec((seq_len, ML_INNER), lambda b: (blk0 + b, 0))
    full = pl.BlockSpec((1, ML_INNER), lambda b: (0, 0))
    c_spec = pl.BlockSpec((1, ML_HEADS, ML_DH, ML_DH), lambda b: (s0 + b, 0, 0, 0))
    in_specs = [row_spec, row_spec, row_spec, row_spec,
                pl.BlockSpec((seq_len, ML_INNER), lambda b: (blk0 + b, 1)),
                pl.BlockSpec((1, 2 * ML_HEADS, seq_len), lambda b: (b, 0, 0)), full, full,
                c_spec,
                pl.BlockSpec((1, ML_HEADS, ML_DH), lambda b: (s0 + b, 0, 0)),
                pl.BlockSpec((1, ML_HEADS, 1), lambda b: (s0 + b, 0, 0)),
                pl.BlockSpec(memory_space=pl.ANY)]
    args = [q, k, v, xc, xmz, gates, gn, skip, c_all, n_all, m_all, gated_in]
    aliases = {len(args) - 1: 0}
    if c_out_prev is not None:
        in_specs.append(pl.BlockSpec(memory_space=pl.ANY))
        args.append(c_out_prev)
        aliases[len(args) - 1] = 1
    return pl.pallas_call(
        _ml_cell_sample_body,
        grid=(n_seq,),
        in_specs=in_specs,
        out_specs=[row_spec, c_spec,
                   pl.BlockSpec((1, ML_HEADS, ML_DH), lambda b: (b, 0, 0)),
                   pl.BlockSpec((1, ML_HEADS, 1), lambda b: (b, 0, 0))],
        out_shape=[jax.ShapeDtypeStruct((t_all, ML_INNER), F32),
                   jax.ShapeDtypeStruct((n_layers * n_seq, ML_HEADS, ML_DH, ML_DH), F32),
                   jax.ShapeDtypeStruct((n_seq, ML_HEADS, ML_DH), F32),
                   jax.ShapeDtypeStruct((n_seq, ML_HEADS, 1), F32)],
        input_output_aliases=aliases,
        compiler_params=_params(("arbitrary",), 48),
        name="ml_cell_sample",
    )(*args)


def _ml_weights(w_in, conv_w, conv_b, wq, wk, wv, w_i, b_i, w_f, b_f, gn_g, skip, w_out):
    nt = ML_INNER // V7X_MXU_DIM
    per = V7X_MXU_DIM // ML_QKV_BLOCK
    eye = jnp.eye(per, dtype=F32)

    def block_diag(w):
        wt = w.reshape(nt, per, ML_QKV_BLOCK, ML_QKV_BLOCK)
        return jnp.einsum("tncd,nm->tncmd", wt, eye).reshape(nt, V7X_MXU_DIM, V7X_MXU_DIM).astype(BF16)

    wg = jnp.concatenate([w_i, w_f], axis=1).T.astype(BF16)
    return dict(w_in=w_in.astype(BF16), conv_w=conv_w, conv_b=conv_b.reshape(1, ML_INNER),
                wq=block_diag(wq), wk=block_diag(wk), wv=block_diag(wv),
                wgq=wg[:, :ML_INNER], wgk=wg[:, ML_INNER:2 * ML_INNER], wgv=wg[:, 2 * ML_INNER:],
                gb=jnp.concatenate([b_i, b_f]).reshape(2 * ML_HEADS, 1),
                gn=gn_g.reshape(1, ML_INNER), skip=skip.reshape(1, ML_INNER), w_out=w_out.astype(BF16))


def _ml_layer(x_all, w, dims, c_all, n_all, m_all, state_conv, layer, n_layers, c_out_prev, ln_g, ln_b):
    n_p, seq_len, n_s, sample_len = dims
    t_p = n_p * seq_len
    rb = ROW_BLOCK
    xmz = _mm(x_all, w["w_in"], 2048)
    pad = jnp.pad(state_conv, ((0, 0), (0, sample_len - (ML_CONV - 1)), (0, 0))).reshape(n_s * sample_len, ML_INNER)
    conv_rows = jnp.concatenate([jnp.zeros((rb, ML_INNER), F32), pad], axis=0)
    q, k, v, xc, g3 = _ml_pre(xmz, conv_rows, w, t_p, seq_len, sample_len)
    npb = t_p // rb
    g_p = g3[:npb]
    if ML_CHUNK != rb:
        g_p = g_p.reshape(npb, 2 * ML_HEADS, rb // ML_CHUNK, ML_CHUNK).transpose(0, 2, 1, 3).reshape(-1, 2 * ML_HEADS, ML_CHUNK)
    g_s = g3[npb:].reshape(-1, 2 * ML_HEADS, rb // sample_len, sample_len).transpose(0, 2, 1, 3)
    g_s = g_s.reshape(n_s, 2 * ML_HEADS, sample_len)
    gated, pc, pn, pm = _ml_cell(q, k, v, xc, xmz, g_p, w["gn"], w["skip"], n_p, seq_len, ML_CHUNK)
    gated, c_out, sn, sm = _ml_cell_sample(q, k, v, xc, xmz, g_s, w["gn"], w["skip"], n_s, sample_len, t_p,
                                           c_all, n_all, m_all, layer, n_layers, gated, c_out_prev)
    h1 = _mm_ln(gated, w["w_out"], x_all, ln_g, ln_b)
    xm = xmz[:, :ML_INNER]
    p_conv = xm[:t_p].reshape(n_p, seq_len, ML_INNER)[:, seq_len - (ML_CONV - 1):]
    s_conv = xm[t_p:].reshape(n_s, sample_len, ML_INNER)[:, sample_len - (ML_CONV - 1):]
    return h1, (pc, pn, pm, p_conv), (sn, sm.reshape(n_s, ML_HEADS), s_conv), c_out


def _gla_in_body(x_ref, w_ref, wa_ref, wa2_ref, ba_ref, o_ref, la_ref):
    x = x_ref[...].astype(BF16)
    o_ref[...] = jnp.dot(x, w_ref[...], preferred_element_type=F32)
    a = jnp.dot(x, wa_ref[...], preferred_element_type=F32)
    la_ref[...] = _log_sigmoid(_dot(a, wa2_ref[...]) + ba_ref[...]) / GLA_TAU


def _gla_in(x, w):
    m, k = x.shape
    tm = ROW_BLOCK
    n1 = 2 * GLA_KT + 2 * GLA_VT
    full = lambda shape: pl.BlockSpec(shape, lambda i: (0, 0))
    return pl.pallas_call(
        _gla_in_body,
        grid=(m // tm,),
        in_specs=[pl.BlockSpec((tm, k), lambda i: (i, 0)), full((k, n1)), full((k, V7X_LANES)),
                  full((V7X_LANES, GLA_KT)), full((1, GLA_KT))],
        out_specs=[pl.BlockSpec((tm, n1), lambda i: (i, 0)), pl.BlockSpec((tm, GLA_KT), lambda i: (i, 0))],
        out_shape=[jax.ShapeDtypeStruct((m, n1), F32), jax.ShapeDtypeStruct((m, GLA_KT), F32)],
        compiler_params=_params(("arbitrary",), 48),
        name="gla_in",
    )(x, w["w_qkvr"], w["w_a"], w["w_a2"], w["b_a"])


def _gla_levels(chunk):
    t = jnp.arange(chunk)[:, None]
    s = jnp.arange(chunk)[None, :]
    mats = [(s <= t)]
    size = chunk
    while size >= 2:
        ref = t - t % size + size // 2 - 1
        mats.append(s <= ref)
        size //= 2
    levels = jnp.concatenate(mats, axis=0).astype(BF16)
    return jnp.pad(levels, ((0, 0), (0, max(V7X_LANES - chunk, 0))))


def _gla_chunk(q, k, v, la, lv, st):
    L = q.shape[0]
    kpad = lv.shape[1]
    parts = jnp.concatenate(_split3(la), axis=1)
    if kpad > L:
        parts = jnp.concatenate([parts, jnp.zeros((kpad - L, parts.shape[1]), BF16)], axis=0)
    cum = jnp.dot(lv, parts, preferred_element_type=F32)
    cum = cum[:, :GLA_DK] + cum[:, GLA_DK:2 * GLA_DK] + cum[:, 2 * GLA_DK:]
    b = cum[0:L]
    ri = lax.broadcasted_iota(jnp.int32, (L, L), 0)
    ci = lax.broadcasted_iota(jnp.int32, (L, L), 1)
    tpos = lax.broadcasted_iota(jnp.int32, (L, 1), 0)
    att = jnp.where(ri == ci, jnp.sum(q * k, axis=1, keepdims=True), 0.0)
    size = L
    lvl = 1
    while size >= 2:
        bref = cum[lvl * L:(lvl + 1) * L]
        upper = (tpos % size) >= (size // 2)
        qs = q * jnp.exp(jnp.where(upper, b - bref, -jnp.inf))
        ks = k * jnp.exp(jnp.where(upper, -jnp.inf, bref - b))
        att = att + jnp.where((ri // size) == (ci // size), _dot_nt(qs, ks), 0.0)
        size //= 2
        lvl += 1
    o = _dot(att, v) + _dot(q * jnp.exp(b), st)
    b_end = b[L - 1:L, :]
    e_end = jnp.exp(b_end)
    rk = lax.broadcasted_iota(jnp.int32, (GLA_DK, GLA_DK), 0)
    ck = lax.broadcasted_iota(jnp.int32, (GLA_DK, GLA_DK), 1)
    e_col = jnp.sum(jnp.where(rk == ck, e_end, 0.0), axis=1, keepdims=True)
    st_new = e_col * st + _dot_tn(k * jnp.exp(b_end - b), v)
    return o, st_new


def _gla_cell_body(q_ref, k_ref, v_ref, r_ref, la_ref, lv_ref, gn_ref, o_ref, so_ref, ss):
    c = pl.program_id(2)
    nc = pl.num_programs(2)

    @pl.when(c == 0)
    def _():
        ss[...] = jnp.zeros(ss.shape, F32)

    o, st_new = _gla_chunk(q_ref[...] * (GLA_DK ** -0.5), k_ref[...], v_ref[...], la_ref[...], lv_ref[...], ss[...])
    ss[...] = st_new
    o_ref[...] = _silu(r_ref[...]) * _head_norm(o, gn_ref[...])

    @pl.when(c == nc - 1)
    def _():
        so_ref[0, 0] = ss[...]


def _gla_cell(qkvr, la, gn, n_seq, seq_len, chunk):
    t_all = qkvr.shape[0]
    nc = seq_len // chunk
    levels = _gla_levels(chunk)
    kq = GLA_KT // GLA_DK
    kv = 2 * GLA_KT // GLA_DV
    return pl.pallas_call(
        _gla_cell_body,
        grid=(n_seq, GLA_HEADS, nc),
        in_specs=[pl.BlockSpec((chunk, GLA_DK), lambda b, h, c: (b * nc + c, h)),
                  pl.BlockSpec((chunk, GLA_DK), lambda b, h, c: (b * nc + c, kq + h)),
                  pl.BlockSpec((chunk, GLA_DV), lambda b, h, c: (b * nc + c, kv + h)),
                  pl.BlockSpec((chunk, GLA_DV), lambda b, h, c: (b * nc + c, kv + GLA_HEADS + h)),
                  pl.BlockSpec((chunk, GLA_DK), lambda b, h, c: (b * nc + c, h)),
                  pl.BlockSpec(levels.shape, lambda b, h, c: (0, 0)),
                  pl.BlockSpec((1, GLA_DV), lambda b, h, c: (0, h))],
        out_specs=[pl.BlockSpec((chunk, GLA_DV), lambda b, h, c: (b * nc + c, h)),
                   pl.BlockSpec((1, 1, GLA_DK, GLA_DV), lambda b, h, c: (b, h, 0, 0))],
        out_shape=[jax.ShapeDtypeStruct((t_all, GLA_VT), F32),
                   jax.ShapeDtypeStruct((n_seq, GLA_HEADS, GLA_DK, GLA_DV), F32)],
        scratch_shapes=[pltpu.VMEM((GLA_DK, GLA_DV), F32)],
        compiler_params=_params(("arbitrary", "arbitrary", "arbitrary"), 32),
        name="gla_cell",
    )(qkvr, qkvr, qkvr, qkvr, la, levels, gn)


def _gla_cell_sample_body(x_ref, la_ref, lv_ref, gn_ref, s0_ref, _, o_ref, so_ref):
    lv = lv_ref[...]
    for h in range(GLA_HEADS):
        qs = slice(h * GLA_DK, (h + 1) * GLA_DK)
        ks = slice(GLA_KT + h * GLA_DK, GLA_KT + (h + 1) * GLA_DK)
        vs = slice(2 * GLA_KT + h * GLA_DV, 2 * GLA_KT + (h + 1) * GLA_DV)
        rs = slice(2 * GLA_KT + GLA_VT + h * GLA_DV, 2 * GLA_KT + GLA_VT + (h + 1) * GLA_DV)
        os_ = slice(h * GLA_DV, (h + 1) * GLA_DV)
        o, st_new = _gla_chunk(x_ref[:, qs] * (GLA_DK ** -0.5), x_ref[:, ks], x_ref[:, vs], la_ref[:, qs], lv,
                               s0_ref[0, h])
        o_ref[:, os_] = _silu(x_ref[:, rs]) * _head_norm(o, gn_ref[:, os_])
        so_ref[0, h] = st_new


def _gla_cell_sample(qkvr, la, gn, n_seq, seq_len, row0, state, gated_in):
    t_all, n1 = qkvr.shape
    blk0 = row0 // seq_len
    levels = _gla_levels(seq_len)
    st_spec = pl.BlockSpec((1, GLA_HEADS, GLA_DK, GLA_DV), lambda b: (b, 0, 0, 0))
    return pl.pallas_call(
        _gla_cell_sample_body,
        grid=(n_seq,),
        in_specs=[pl.BlockSpec((seq_len, n1), lambda b: (blk0 + b, 0)),
                  pl.BlockSpec((seq_len, GLA_KT), lambda b: (blk0 + b, 0)),
                  pl.BlockSpec(levels.shape, lambda b: (0, 0)),
                  pl.BlockSpec((1, GLA_VT), lambda b: (0, 0)),
                  st_spec, pl.BlockSpec(memory_space=pl.ANY)],
        out_specs=[pl.BlockSpec((seq_len, GLA_VT), lambda b: (blk0 + b, 0)), st_spec],
        out_shape=[jax.ShapeDtypeStruct((t_all, GLA_VT), F32),
                   jax.ShapeDtypeStruct((n_seq, GLA_HEADS, GLA_DK, GLA_DV), F32)],
        input_output_aliases={5: 0},
        compiler_params=_params(("arbitrary",), 32),
        name="gla_cell_sample",
    )(qkvr, la, levels, gn, state, gated_in)


def _gla_weights(w_in, w_a2, b_a, gn_g, w_out):
    n1 = 2 * GLA_KT + 2 * GLA_VT
    w_a = jnp.pad(w_in[:, n1:], ((0, 0), (0, V7X_LANES - GLA_RANK)))
    w_a2p = jnp.pad(w_a2, ((0, V7X_LANES - GLA_RANK), (0, 0)))
    return dict(w_qkvr=w_in[:, :n1].astype(BF16), w_a=w_a.astype(BF16), w_a2=w_a2p.astype(BF16),
                b_a=b_a.reshape(1, GLA_KT), gn=gn_g.reshape(1, GLA_VT), w_out=w_out.astype(BF16))


def _gla_layer(x_all, w, dims, state_s, ln_g, ln_b):
    n_p, seq_len, n_s, sample_len = dims
    t_p = n_p * seq_len
    qkvr, la = _gla_in(x_all, w)
    gated, p_s = _gla_cell(qkvr, la, w["gn"], n_p, seq_len, min(GLA_CHUNK, seq_len))
    gated, s_s = _gla_cell_sample(qkvr, la, w["gn"], n_s, sample_len, t_p, state_s, gated)
    h1 = _mm_ln(gated, w["w_out"], x_all, ln_g, ln_b)
    return h1, p_s, s_s


def _s5_body(x_ref, sre_ref, sim_ref, win_ref, bt_ref, cre_ref, cim_ref, lre_ref, lim_ref, d_ref, wglu_ref,
             g_ref, b_ref, o_ref, ore_ref, oim_ref, st_re, st_im, car_re, car_im, *, nb, lc):
    c = pl.program_id(1)
    nc = pl.num_programs(1)
    rows = nb * lc
    ntile = D_MODEL // S5_TILE_CH

    @pl.when(c == 0)
    def _():
        car_re[...] = sre_ref[...]
        car_im[...] = sim_ref[...]

    x = x_ref[...].reshape(rows, D_MODEL)
    u = _dot(x, win_ref[...])
    lpt = S5_TILE_ST // V7X_LANES
    for t in range(ntile):
        bu = _dot(u[:, t * S5_TILE_CH:(t + 1) * S5_TILE_CH], bt_ref[t])
        for j in range(lpt):
            st_re[t * lpt + j] = bu[:, j * V7X_LANES:(j + 1) * V7X_LANES]
            st_im[t * lpt + j] = bu[:, S5_TILE_ST + j * V7X_LANES:S5_TILE_ST + (j + 1) * V7X_LANES]

    for gidx in range(nb // V7X_SUBLANES):
        grp = slice(gidx * V7X_SUBLANES, (gidx + 1) * V7X_SUBLANES)
        for t in range(ntile):
            slabs = list(range(t * lpt, (t + 1) * lpt))
            lanes = [slice(j * V7X_LANES, (j + 1) * V7X_LANES) for j in slabs]
            lre = [lre_ref[:, ln] for ln in lanes]
            lim = [lim_ref[:, ln] for ln in lanes]

            def step(tok, carry, slabs=slabs, lre=lre, lim=lim, gidx=gidx):
                sel = pl.ds(pl.multiple_of(tok * nb + gidx * V7X_SUBLANES, V7X_SUBLANES), V7X_SUBLANES)
                out = []
                for n, j in enumerate(slabs):
                    pr, pi = carry[2 * n], carry[2 * n + 1]
                    nr = lre[n] * pr - lim[n] * pi + st_re[j, sel, :]
                    ni = lre[n] * pi + lim[n] * pr + st_im[j, sel, :]
                    st_re[j, sel, :] = nr
                    st_im[j, sel, :] = ni
                    out += [nr, ni]
                return tuple(out)

            init = []
            for ln in lanes:
                init += [car_re[grp, ln], car_im[grp, ln]]
            fin = lax.fori_loop(0, lc, step, tuple(init))
            for n, ln in enumerate(lanes):
                car_re[grp, ln] = fin[2 * n]
                car_im[grp, ln] = fin[2 * n + 1]

    ys = []
    for t in range(ntile):
        sre = jnp.concatenate([st_re[t * lpt + j] for j in range(lpt)], axis=1)
        sim = jnp.concatenate([st_im[t * lpt + j] for j in range(lpt)], axis=1)
        ys.append(_dot(sre, cre_ref[t]) - _dot(sim, cim_ref[t]))
    y = jnp.concatenate(ys, axis=1) + d_ref[...] * u
    y = jax.nn.gelu(y)
    vg = _dot(y, wglu_ref[...])
    mix = vg[:, :D_MODEL] * _sigmoid(vg[:, D_MODEL:])
    o_ref[...] = _layer_norm(DEEPNORM_ALPHA * x + mix, g_ref[...], b_ref[...]).reshape(lc, nb, D_MODEL)

    @pl.when(c == nc - 1)
    def _():
        ore_ref[...] = car_re[...]
        oim_ref[...] = car_im[...]


def _s5_call(x3, s_re, s_im, w, ln_g, ln_b, nb, lc):
    seq_len, n_seq, _ = x3.shape
    ntile = D_MODEL // S5_TILE_CH
    rows = nb * lc
    full2 = lambda shape: pl.BlockSpec(shape, lambda i, c: (0, 0))
    full3 = lambda shape: pl.BlockSpec(shape, lambda i, c: (0, 0, 0))
    st_spec = pl.BlockSpec((nb, S5_STATE), lambda i, c: (i, 0))
    x_spec = pl.BlockSpec((lc, nb, D_MODEL), lambda i, c: (c, i, 0))
    return pl.pallas_call(
        functools.partial(_s5_body, nb=nb, lc=lc),
        grid=(n_seq // nb, seq_len // lc),
        in_specs=[x_spec, st_spec, st_spec,
                  full2((D_MODEL, D_MODEL)), full3((ntile, S5_TILE_CH, 2 * S5_TILE_ST)),
                  full3((ntile, S5_TILE_ST, S5_TILE_CH)), full3((ntile, S5_TILE_ST, S5_TILE_CH)),
                  full2((V7X_SUBLANES, S5_STATE)), full2((V7X_SUBLANES, S5_STATE)), full2((1, D_MODEL)),
                  full2((D_MODEL, 2 * D_MODEL)), full2((1, D_MODEL)), full2((1, D_MODEL))],
        out_specs=[x_spec, st_spec, st_spec],
        out_shape=[jax.ShapeDtypeStruct(x3.shape, F32), jax.ShapeDtypeStruct((n_seq, S5_STATE), F32),
                   jax.ShapeDtypeStruct((n_seq, S5_STATE), F32)],
        scratch_shapes=[pltpu.VMEM((S5_STATE // V7X_LANES, rows, V7X_LANES), F32),
                        pltpu.VMEM((S5_STATE // V7X_LANES, rows, V7X_LANES), F32),
                        pltpu.VMEM((nb, S5_STATE), F32), pltpu.VMEM((nb, S5_STATE), F32)],
        compiler_params=_params(("arbitrary", "arbitrary"), 56),
        name="s5",
    )(x3, s_re, s_im, w["w_in"], w["b_tiles"], w["c_re"], w["c_im"], w["lam_re"], w["lam_im"], w["d"], w["w_glu"],
      ln_g.reshape(1, D_MODEL), ln_b.reshape(1, D_MODEL))


def _s5_weights(w_in, a_re, a_im, log_dt, b_re, b_im, c_re, c_im, d_skip, w_glu):
    lam = lax.complex(a_re.astype(F32), a_im.astype(F32))
    dt = jnp.exp(log_dt.astype(F32))[:, None]
    lam_bar = jnp.exp(lam * dt)
    b_bar = ((lam_bar - 1.0) / lam)[..., None] * lax.complex(b_re.astype(F32), b_im.astype(F32))
    ntile = D_MODEL // S5_TILE_CH
    gpt = S5_TILE_CH // S5_GC
    eye = jnp.eye(gpt, dtype=F32)

    def b_tiles(bb):
        return jnp.einsum("igpc,gh->igchp", bb.reshape(ntile, gpt, S5_P, S5_GC), eye).reshape(ntile, S5_TILE_CH, S5_TILE_ST)

    def c_tiles(cc):
        return jnp.einsum("igcp,gh->igphc", cc.reshape(ntile, gpt, S5_GC, S5_P), eye).reshape(ntile, S5_TILE_ST, S5_TILE_CH)

    bt = jnp.concatenate([b_tiles(b_bar.real), b_tiles(b_bar.imag)], axis=2).astype(BF16)
    bcast = lambda a: jnp.broadcast_to(a.reshape(1, S5_STATE), (V7X_SUBLANES, S5_STATE))
    return dict(w_in=w_in.astype(BF16), b_tiles=bt, c_re=c_tiles(c_re.astype(F32)).astype(BF16),
                c_im=c_tiles(c_im.astype(F32)).astype(BF16), lam_re=bcast(lam_bar.real), lam_im=bcast(lam_bar.imag),
                d=d_skip.reshape(1, D_MODEL).astype(F32), w_glu=w_glu.astype(BF16))


def _s5_layer(x_all, w, dims, state_re, state_im, ln_g, ln_b):
    n_p, seq_len, n_s, sample_len = dims
    t_p = n_p * seq_len
    xp = x_all[:t_p].reshape(n_p, seq_len, D_MODEL).transpose(1, 0, 2)
    xs = x_all[t_p:].reshape(n_s, sample_len, D_MODEL).transpose(1, 0, 2)
    zero = jnp.zeros((n_p, S5_STATE), F32)
    hp, p_re, p_im = _s5_call(xp, zero, zero, w, ln_g, ln_b, n_p, min(S5_CHUNK, seq_len))
    nb_s = min(n_s, ROW_BLOCK // sample_len)
    hs, s_re, s_im = _s5_call(xs, state_re.reshape(n_s, S5_STATE), state_im.reshape(n_s, S5_STATE), w, ln_g, ln_b,
                              nb_s, sample_len)
    h1 = jnp.concatenate([hp.transpose(1, 0, 2).reshape(t_p, D_MODEL),
                          hs.transpose(1, 0, 2).reshape(n_s * sample_len, D_MODEL)], axis=0)
    shp = lambda a, n: a.reshape(n, S5_GROUPS, S5_P)
    return h1, (shp(p_re, n_p), shp(p_im, n_p)), (shp(s_re, n_s), shp(s_im, n_s))


def _route_body(x_ref, wh_ref, wl_ref, br_ref, g_ref):
    x = x_ref[...]
    xh = x.astype(BF16)
    xl = (x - xh.astype(F32)).astype(BF16)
    wh = wh_ref[...]
    logits = (jnp.dot(xh, wh, preferred_element_type=F32) + jnp.dot(xl, wh, preferred_element_type=F32)
              + jnp.dot(xh, wl_ref[...], preferred_element_type=F32))
    s = _sigmoid(logits)
    work = s + br_ref[...]
    lane = lax.broadcasted_iota(jnp.int32, s.shape, 1).astype(F32)
    chosen = jnp.zeros(s.shape, jnp.bool_)
    for _ in range(TOP_K):
        mx = jnp.max(work, axis=1, keepdims=True)
        idx = jnp.min(jnp.where(work == mx, lane, float(N_EXPERTS)), axis=1, keepdims=True)
        hit = lane == idx
        chosen = jnp.logical_or(chosen, hit)
        work = jnp.where(hit, -jnp.inf, work)
    sel = jnp.where(chosen, s, 0.0)
    g_ref[...] = sel / jnp.sum(sel, axis=1, keepdims=True) * ROUTE_SCALE


def _route(x, w):
    m, k = x.shape
    tm = ROW_BLOCK
    full = lambda shape: pl.BlockSpec(shape, lambda i: (0, 0))
    return pl.pallas_call(
        _route_body,
        grid=(m // tm,),
        in_specs=[pl.BlockSpec((tm, k), lambda i: (i, 0)), full((k, N_EXPERTS)), full((k, N_EXPERTS)),
                  full((1, N_EXPERTS))],
        out_specs=pl.BlockSpec((tm, N_EXPERTS), lambda i: (i, 0)),
        out_shape=jax.ShapeDtypeStruct((m, N_EXPERTS), F32),
        compiler_params=_params(("arbitrary",), 32),
        name="route",
    )(x, w["wr_hi"], w["wr_lo"], w["b_router"])


def _moe_body(x_ref, gd_ref, wg_ref, wu_ref, wd_ref, sg_ref, su_ref, sd_ref, g_ref, b_ref, o_ref, acc):
    e = pl.program_id(1)
    ne = pl.num_programs(1)
    xb = x_ref[...].astype(BF16)

    @pl.when(e == 0)
    def _():
        hs = _silu(jnp.dot(xb, sg_ref[...], preferred_element_type=F32)) * jnp.dot(xb, su_ref[...], preferred_element_type=F32)
        acc[...] = _dot(hs, sd_ref[...])

    h = _silu(jnp.dot(xb, wg_ref[0], preferred_element_type=F32)) * jnp.dot(xb, wu_ref[0], preferred_element_type=F32)
    gd = gd_ref[...]
    lane = lax.broadcasted_iota(jnp.int32, gd.shape, 1)
    gate = jnp.sum(jnp.where(lane == e, gd, 0.0), axis=1, keepdims=True)
    acc[...] += _dot(h, wd_ref[0]) * gate

    @pl.when(e == ne - 1)
    def _():
        o_ref[...] = _layer_norm(DEEPNORM_ALPHA * x_ref[...] + acc[...], g_ref[...], b_ref[...])


def _moe_ffn(x, gates, w, ln_g, ln_b):
    m, k = x.shape
    tm = MOE_ROWS
    full = lambda shape: pl.BlockSpec(shape, lambda i, e: (0, 0))
    return pl.pallas_call(
        _moe_body,
        grid=(m // tm, N_EXPERTS),
        in_specs=[pl.BlockSpec((tm, k), lambda i, e: (i, 0)), pl.BlockSpec((tm, N_EXPERTS), lambda i, e: (i, 0)),
                  pl.BlockSpec((1, k, EXPERT_FF), lambda i, e: (e, 0, 0)),
                  pl.BlockSpec((1, k, EXPERT_FF), lambda i, e: (e, 0, 0)),
                  pl.BlockSpec((1, EXPERT_FF, k), lambda i, e: (e, 0, 0)),
                  full((k, w["ws_gate"].shape[1])), full((k, w["ws_up"].shape[1])), full((w["ws_down"].shape[0], k)),
                  full((1, k)), full((1, k))],
        out_specs=pl.BlockSpec((tm, k), lambda i, e: (i, 0)),
        out_shape=jax.ShapeDtypeStruct((m, k), F32),
        scratch_shapes=[pltpu.VMEM((tm, k), F32)],
        compiler_params=_params(("arbitrary", "arbitrary"), 48),
        name="moe_ffn",
    )(x, gates, w["w_gate"], w["w_up"], w["w_down"], w["ws_gate"], w["ws_up"], w["ws_down"],
      ln_g.reshape(1, k), ln_b.reshape(1, k))


def _moe_weights(w_router, b_router, w_gate, w_up, w_down, ws_gate, ws_up, ws_down):
    wr_hi = w_router.astype(BF16)
    wr_lo = (w_router - wr_hi.astype(F32)).astype(BF16)
    return dict(wr_hi=wr_hi, wr_lo=wr_lo, b_router=b_router.reshape(1, N_EXPERTS).astype(F32),
                w_gate=w_gate.astype(BF16), w_up=w_up.astype(BF16), w_down=w_down.astype(BF16),
                ws_gate=ws_gate.astype(BF16), ws_up=ws_up.astype(BF16), ws_down=ws_down.astype(BF16))


def _moe_layer(x, w, ln_g, ln_b):
    return _moe_ffn(x, _route(x, w), w, ln_g, ln_b)


def kernel(x_prompt, x_sample, state_mlstm_C, state_mlstm_n, state_mlstm_m, state_mlstm_conv, state_gla_S, state_s5_re, state_s5_im, ln1_g, ln1_b, ln2_g, ln2_b, ml_w_in, ml_conv_w, ml_conv_b, ml_wq, ml_wk, ml_wv, ml_w_i, ml_b_i, ml_w_f, ml_b_f, ml_gn_g, ml_skip, ml_w_out, gla_w_in, gla_w_a2, gla_b_a, gla_gn_g, gla_w_out, s5_w_in, s5_a_re, s5_a_im, s5_log_dt, s5_b_re, s5_b_im, s5_c_re, s5_c_im, s5_d, s5_w_glu, moe_w_router, moe_b_router, moe_w_gate, moe_w_up, moe_w_down, moe_ws_gate, moe_ws_up, moe_ws_down):
    n_p, seq_len, _ = x_prompt.shape
    n_s, sample_len, _ = x_sample.shape
    dims = (n_p, seq_len, n_s, sample_len)
    t_p = n_p * seq_len
    t_s = n_s * sample_len
    assert seq_len % ROW_BLOCK == 0 and t_s % ROW_BLOCK == 0 and ROW_BLOCK % sample_len == 0
    assert sample_len == V7X_SUBLANES and n_p % V7X_SUBLANES == 0 and (t_p + t_s) % MOE_ROWS == 0
    x = jnp.concatenate([x_prompt.reshape(t_p, D_MODEL), x_sample.reshape(t_s, D_MODEL)], axis=0)
    n_ml = len(range(0, DEPTH, N_MIXERS))
    c_all = state_mlstm_C.reshape(-1, ML_HEADS, ML_DH, ML_DH)
    n_all = state_mlstm_n.reshape(-1, ML_HEADS, ML_DH)
    m_all = state_mlstm_m.reshape(-1, ML_HEADS, 1)
    c_out = None
    p_ml, s_ml, p_gla, s_gla, p_s5, s_s5 = [], [], [], [], [], []
    for i in range(DEPTH):
        j = i // N_MIXERS
        if i % N_MIXERS == 0:
            w = _ml_weights(ml_w_in[j], ml_conv_w[j], ml_conv_b[j], ml_wq[j], ml_wk[j], ml_wv[j], ml_w_i[j], ml_b_i[j],
                            ml_w_f[j], ml_b_f[j], ml_gn_g[j], ml_skip[j], ml_w_out[j])
            x, ps, ss, c_out = _ml_layer(x, w, dims, c_all, n_all, m_all, state_mlstm_conv[j], j, n_ml, c_out,
                                         ln1_g[i], ln1_b[i])
            p_ml.append(ps)
            s_ml.append(ss)
        elif i % N_MIXERS == 1:
            w = _gla_weights(gla_w_in[j], gla_w_a2[j], gla_b_a[j], gla_gn_g[j], gla_w_out[j])
            x, ps, ss = _gla_layer(x, w, dims, state_gla_S[j], ln1_g[i], ln1_b[i])
            p_gla.append(ps)
            s_gla.append(ss)
        else:
            w = _s5_weights(s5_w_in[j], s5_a_re[j], s5_a_im[j], s5_log_dt[j], s5_b_re[j], s5_b_im[j], s5_c_re[j],
                            s5_c_im[j], s5_d[j], s5_w_glu[j])
            x, ps, ss = _s5_layer(x, w, dims, state_s5_re[j], state_s5_im[j], ln1_g[i], ln1_b[i])
            p_s5.append(ps)
            s_s5.append(ss)
        wm = _moe_weights(moe_w_router[i], moe_b_router[i], moe_w_gate[i], moe_w_up[i], moe_w_down[i],
                          moe_ws_gate[i], moe_ws_up[i], moe_ws_down[i])
        x = _moe_layer(x, wm, ln2_g[i], ln2_b[i])
    stack = lambda items, idx: jnp.stack([it[idx] for it in items])
    return (x[:t_p].reshape(n_p, seq_len, D_MODEL), x[t_p:].reshape(n_s, sample_len, D_MODEL),
            stack(p_ml, 0), stack(p_ml, 1), stack(p_ml, 2), stack(p_ml, 3), jnp.stack(p_gla),
            stack(p_s5, 0), stack(p_s5, 1),
            c_out.reshape(n_ml, n_s, ML_HEADS, ML_DH, ML_DH), stack(s_ml, 0), stack(s_ml, 1), stack(s_ml, 2),
            jnp.stack(s_gla), stack(s_s5, 0), stack(s_s5, 1))
```

```python
import functools

import jax
import jax.numpy as jnp
from jax import lax
from jax.experimental import pallas as pl
from jax.experimental.pallas import tpu as pltpu

F32 = jnp.float32
BF16 = jnp.bfloat16

D_MODEL = 1024
DEPTH = 4
N_MIXERS = 3
ML_INNER = 2 * D_MODEL
ML_HEADS = 4
ML_DH = ML_INNER // ML_HEADS
ML_QKV_BLOCK = 4
ML_CONV = 4
GLA_HEADS = 4
GLA_KT = D_MODEL // 2
GLA_VT = D_MODEL
GLA_DK = GLA_KT // GLA_HEADS
GLA_DV = GLA_VT // GLA_HEADS
GLA_RANK = 16
GLA_TAU = 16.0
S5_GC = 16
S5_GROUPS = D_MODEL // S5_GC
S5_P = 64
S5_STATE = S5_GROUPS * S5_P
N_EXPERTS = 64
TOP_K = 8
EXPERT_FF = 256
ROUTE_SCALE = 2.5
DEEPNORM_ALPHA = (2.0 * DEPTH) ** 0.25
NORM_EPS = 1e-5

V7X_VMEM_BYTES = 64 * 1024 * 1024
V7X_LANES = 128
V7X_SUBLANES = 8
V7X_MXU_DIM = 256

ROW_BLOCK = 256
ML_CHUNK = 256
GLA_CHUNK = 128
S5_CHUNK = 32
S5_TILE_CH = 128
S5_TILE_ST = S5_TILE_CH // S5_GC * S5_P
MOE_ROWS = 1024
MOE_EXPERTS_PER_STEP = 4


def _params(semantics, vmem_mb):
    assert vmem_mb * 1024 * 1024 < V7X_VMEM_BYTES
    return pltpu.CompilerParams(dimension_semantics=semantics, vmem_limit_bytes=vmem_mb * 1024 * 1024)


def _dot(a, b):
    return jnp.dot(a.astype(BF16), b.astype(BF16), preferred_element_type=F32)


def _dot_nt(a, b):
    return lax.dot_general(a.astype(BF16), b.astype(BF16), (((1,), (1,)), ((), ())), preferred_element_type=F32)


def _dot_tn(a, b):
    return lax.dot_general(a.astype(BF16), b.astype(BF16), (((0,), (0,)), ((), ())), preferred_element_type=F32)


def _sigmoid(x):
    return 1.0 / (1.0 + jnp.exp(-x))


def _silu(x):
    return x * _sigmoid(x)


def _log_sigmoid(x):
    return jnp.minimum(x, 0.0) - jnp.log1p(jnp.exp(-jnp.abs(x)))


def _layer_norm(y, g, b):
    mu = jnp.mean(y, axis=-1, keepdims=True)
    d = y - mu
    var = jnp.mean(d * d, axis=-1, keepdims=True)
    return d * lax.rsqrt(var + NORM_EPS) * g + b


def _head_norm(h, g):
    mu = jnp.mean(h, axis=-1, keepdims=True)
    d = h - mu
    var = jnp.mean(d * d, axis=-1, keepdims=True)
    return d * lax.rsqrt(var + NORM_EPS) * g


def _split3(x):
    hi = x.astype(BF16)
    r1 = x - hi.astype(F32)
    mid = r1.astype(BF16)
    lo = (r1 - mid.astype(F32)).astype(BF16)
    return hi, mid, lo


def _mm_body(x_ref, w_ref, o_ref):
    o_ref[...] = _dot(x_ref[...], w_ref[...])


def _mm(x, w, tn):
    m, k = x.shape
    n = w.shape[1]
    tm = ROW_BLOCK
    return pl.pallas_call(
        _mm_body,
        grid=(n // tn, m // tm),
        in_specs=[pl.BlockSpec((tm, k), lambda j, i: (i, 0)), pl.BlockSpec((k, tn), lambda j, i: (0, j))],
        out_specs=pl.BlockSpec((tm, tn), lambda j, i: (i, j)),
        out_shape=jax.ShapeDtypeStruct((m, n), F32),
        compiler_params=_params(("arbitrary", "arbitrary"), 40),
        name="mm",
    )(x, w)


def _mm_ln_body(ap_ref, as_ref, w_ref, r_ref, g_ref, b_ref, o_ref, *, n_prompt_blocks):
    i = pl.program_id(0)

    def run(a_ref):
        y = _dot(a_ref[...], w_ref[...])
        o_ref[...] = _layer_norm(DEEPNORM_ALPHA * r_ref[...] + y, g_ref[...], b_ref[...])

    pl.when(i < n_prompt_blocks)(lambda: run(ap_ref))
    pl.when(i >= n_prompt_blocks)(lambda: run(as_ref))


def _mm_ln(a_p, a_s, w, resid, g, b):
    k = a_p.shape[1]
    m = a_p.shape[0] + a_s.shape[0]
    n = w.shape[1]
    tm = ROW_BLOCK
    npb = a_p.shape[0] // tm
    return pl.pallas_call(
        functools.partial(_mm_ln_body, n_prompt_blocks=npb),
        grid=(m // tm,),
        in_specs=[pl.BlockSpec((tm, k), lambda i: (jnp.minimum(i, npb - 1), 0)),
                  pl.BlockSpec((tm, k), lambda i: (jnp.maximum(i - npb, 0), 0)),
                  pl.BlockSpec((k, n), lambda i: (0, 0)),
                  pl.BlockSpec((tm, n), lambda i: (i, 0)), pl.BlockSpec((1, n), lambda i: (0, 0)),
                  pl.BlockSpec((1, n), lambda i: (0, 0))],
        out_specs=pl.BlockSpec((tm, n), lambda i: (i, 0)),
        out_shape=jax.ShapeDtypeStruct((m, n), F32),
        compiler_params=_params(("arbitrary",), 40),
        name="mm_ln",
    )(a_p, a_s, w, resid, g.reshape(1, n), b.reshape(1, n))


def _ml_pre_body(x_ref, prev_ref, p_ref, cw_ref, cb_ref, wq_ref, wk_ref, wv_ref, wgq_ref, wgk_ref, wgv_ref, gb_ref,
                 q_ref, k_ref, v_ref, xc_ref, g_ref, scx, scp, *, n_prompt_blocks, blocks_per_seq, sample_len):
    i = pl.program_id(0)
    rows = x_ref.shape[0]
    is_prompt = i < n_prompt_blocks
    no_prev = jnp.logical_or(i % blocks_per_seq == 0, jnp.logical_not(is_prompt))
    x = x_ref[...]
    scx[0:V7X_SUBLANES, :] = jnp.where(no_prev, 0.0, prev_ref[...])
    scx[V7X_SUBLANES:V7X_SUBLANES + rows, :] = x
    scp[0:rows, :] = p_ref[...]
    scp[rows:rows + V7X_SUBLANES, :] = jnp.zeros((V7X_SUBLANES, x.shape[1]), F32)
    r = lax.broadcasted_iota(jnp.int32, (rows, 1), 0)
    tpos = jnp.where(is_prompt, (i % blocks_per_seq) * rows + r, r % sample_len)
    acc = cb_ref[...] + x * cw_ref[ML_CONV - 1:ML_CONV, :]
    for s in range(1, ML_CONV):
        xs = scx[V7X_SUBLANES - s:V7X_SUBLANES - s + rows, :]
        ps = scp[ML_CONV - 1 - s:ML_CONV - 1 - s + rows, :]
        acc = acc + jnp.where(tpos >= s, xs, ps) * cw_ref[ML_CONV - 1 - s:ML_CONV - s, :]
    xc = _silu(acc)
    xc_ref[...] = xc
    g = jnp.zeros((2 * ML_HEADS, rows), F32)
    for t in range(ML_INNER // V7X_MXU_DIM):
        sl = slice(t * V7X_MXU_DIM, (t + 1) * V7X_MXU_DIM)
        q_t = _dot(xc[:, sl], wq_ref[t])
        k_t = _dot(xc[:, sl], wk_ref[t])
        v_t = _dot(x[:, sl], wv_ref[t])
        q_ref[:, sl] = q_t
        k_ref[:, sl] = k_t * (ML_DH ** -0.5)
        v_ref[:, sl] = v_t
        g = g + _dot_nt(wgq_ref[:, sl], q_t) + _dot_nt(wgk_ref[:, sl], k_t) + _dot_nt(wgv_ref[:, sl], v_t)
    g = g + gb_ref[...]
    gate_row = lax.broadcasted_iota(jnp.int32, g.shape, 0)
    g_ref[0] = jnp.where(gate_row < ML_HEADS, g, _log_sigmoid(g))


def _ml_pre(xmz, conv_rows, w, n_prompt_rows, seq_len, sample_len):
    t_all = xmz.shape[0]
    rb = ROW_BLOCK
    nblk = t_all // rb
    npb = n_prompt_rows // rb
    bps = seq_len // rb
    sub = rb // V7X_SUBLANES
    body = functools.partial(_ml_pre_body, n_prompt_blocks=npb, blocks_per_seq=bps, sample_len=sample_len)
    full2 = lambda shape: pl.BlockSpec(shape, lambda i: (0, 0))
    full3 = lambda shape: pl.BlockSpec(shape, lambda i: (0, 0, 0))
    nt = ML_INNER // V7X_MXU_DIM
    row_spec = pl.BlockSpec((rb, ML_INNER), lambda i: (i, 0))
    return pl.pallas_call(
        body,
        grid=(nblk,),
        in_specs=[row_spec,
                  pl.BlockSpec((V7X_SUBLANES, ML_INNER), lambda i: (jnp.maximum(i * sub - 1, 0), 0)),
                  pl.BlockSpec((rb, ML_INNER), lambda i: (jnp.maximum(i - npb + 1, 0), 0)),
                  full2((ML_CONV, ML_INNER)), full2((1, ML_INNER)),
                  full3((nt, V7X_MXU_DIM, V7X_MXU_DIM)), full3((nt, V7X_MXU_DIM, V7X_MXU_DIM)),
                  full3((nt, V7X_MXU_DIM, V7X_MXU_DIM)),
                  full2((2 * ML_HEADS, ML_INNER)), full2((2 * ML_HEADS, ML_INNER)), full2((2 * ML_HEADS, ML_INNER)),
                  full2((2 * ML_HEADS, 1))],
        out_specs=[row_spec, row_spec, row_spec, row_spec,
                   pl.BlockSpec((1, 2 * ML_HEADS, rb), lambda i: (i, 0, 0))],
        out_shape=[jax.ShapeDtypeStruct((t_all, ML_INNER), F32)] * 4
        + [jax.ShapeDtypeStruct((nblk, 2 * ML_HEADS, rb), F32)],
        scratch_shapes=[pltpu.VMEM((rb + V7X_SUBLANES, ML_INNER), F32), pltpu.VMEM((rb + V7X_SUBLANES, ML_INNER), F32)],
        compiler_params=_params(("arbitrary",), 48),
        name="ml_pre",
    )(xmz, xmz, conv_rows, w["conv_w"], w["conv_b"], w["wq"], w["wk"], w["wv"], w["wgq"], w["wgk"], w["wgv"], w["gb"])


def _ml_chunk(q, k, v, ip, fl, c_prev, n_prev, m_prev):
    L = q.shape[0]
    ri = lax.broadcasted_iota(jnp.int32, (L, L), 0)
    ci = lax.broadcasted_iota(jnp.int32, (L, L), 1)
    eye = ri == ci
    tril = ci <= ri
    f_col = jnp.sum(jnp.where(eye, fl, 0.0), axis=1, keepdims=True)
    b_col = jnp.sum(jnp.where(tril, fl, 0.0), axis=1, keepdims=True)
    b_row = jnp.sum(jnp.where(ri <= ci, f_col, 0.0), axis=0, keepdims=True)
    ib = ip - b_row
    d = jnp.where(tril, b_col + ib, -jnp.inf)
    m_inter = b_col + m_prev
    m_t = jnp.maximum(m_inter, jnp.max(d, axis=1, keepdims=True))
    a_inter = jnp.exp(m_inter - m_t)
    s = _dot_nt(q, k) * jnp.exp(d - m_t)
    num = _dot(s, v) + a_inter * _dot(q, c_prev)
    den = jnp.sum(s, axis=1, keepdims=True) + a_inter * jnp.sum(q * n_prev, axis=1, keepdims=True)
    hc = num / jnp.maximum(jnp.abs(den), jnp.exp(-m_t))
    m_new = m_t[L - 1:L, :]
    b_last = b_row[:, L - 1:L]
    w_row = jnp.exp(b_last + ib - m_new)
    w_col = jnp.sum(jnp.where(eye, w_row, 0.0), axis=1, keepdims=True)
    a_end = jnp.exp(b_last + m_prev - m_new)
    kw = k * w_col
    c_new = a_end * c_prev + _dot_tn(kw, v)
    n_new = a_end * n_prev + jnp.sum(kw, axis=0, keepdims=True)
    return hc, c_new, n_new, m_new


def _ml_gate_out(hc, gn, sk, xc, z):
    return (_head_norm(hc, gn) + sk * xc) * _silu(z)


def _ml_cell_body(q_ref, k_ref, v_ref, xc_ref, z_ref, g_ref, gn_ref, sk_ref, o_ref, co_ref, no_ref, mo_ref, cs, ns, ms):
    h = pl.program_id(1)
    c = pl.program_id(2)
    nc = pl.num_programs(2)

    @pl.when(c == 0)
    def _():
        cs[...] = jnp.zeros(cs.shape, F32)
        ns[...] = jnp.zeros(ns.shape, F32)
        ms[...] = jnp.zeros(ms.shape, F32)

    ip = g_ref[0, pl.ds(h, 1), :]
    fl = g_ref[0, pl.ds(ML_HEADS + h, 1), :]
    hc, c_new, n_new, m_new = _ml_chunk(q_ref[...], k_ref[...], v_ref[...], ip, fl, cs[...], ns[...], ms[...])
    cs[...] = c_new
    ns[...] = n_new
    ms[...] = m_new
    o_ref[...] = _ml_gate_out(hc, gn_ref[...], sk_ref[...], xc_ref[...], z_ref[...])

    @pl.when(c == nc - 1)
    def _():
        co_ref[0, 0] = cs[...]
        no_ref[0] = ns[...]
        mo_ref[0] = ms[...]


def _ml_cell(q, k, v, xc, xmz, gates, gn, skip, n_seq, seq_len, chunk):
    t_all = n_seq * seq_len
    nc = seq_len // chunk
    head_spec = pl.BlockSpec((chunk, ML_DH), lambda b, h, c: (b * nc + c, h))
    state_map = lambda b, h, c: (b * ML_HEADS + h, 0, 0)
    out = pl.pallas_call(
        _ml_cell_body,
        grid=(n_seq, ML_HEADS, nc),
        in_specs=[head_spec, head_spec, head_spec, head_spec,
                  pl.BlockSpec((chunk, ML_DH), lambda b, h, c: (b * nc + c, ML_HEADS + h)),
                  pl.BlockSpec((1, 2 * ML_HEADS, chunk), lambda b, h, c: (b * nc + c, 0, 0)),
                  pl.BlockSpec((1, ML_DH), lambda b, h, c: (0, h)),
                  pl.BlockSpec((1, ML_DH), lambda b, h, c: (0, h))],
        out_specs=[head_spec,
                   pl.BlockSpec((1, 1, ML_DH, ML_DH), lambda b, h, c: (b, h, 0, 0)),
                   pl.BlockSpec((1, 1, ML_DH), state_map),
                   pl.BlockSpec((1, 1, 1), state_map)],
        out_shape=[jax.ShapeDtypeStruct((t_all, ML_INNER), F32),
                   jax.ShapeDtypeStruct((n_seq, ML_HEADS, ML_DH, ML_DH), F32),
                   jax.ShapeDtypeStruct((n_seq * ML_HEADS, 1, ML_DH), F32),
                   jax.ShapeDtypeStruct((n_seq * ML_HEADS, 1, 1), F32)],
        scratch_shapes=[pltpu.VMEM((ML_DH, ML_DH), F32), pltpu.VMEM((1, ML_DH), F32), pltpu.VMEM((1, 1), F32)],
        compiler_params=_params(("arbitrary", "arbitrary", "arbitrary"), 40),
        name="ml_cell",
    )(q, k, v, xc, xmz, gates, gn, skip)
    gated, c_new, n_new, m_new = out
    return gated, c_new, n_new.reshape(n_seq, ML_HEADS, ML_DH), m_new.reshape(n_seq, ML_HEADS)


def _ml_cell_sample_body(*refs):
    q_ref, k_ref, v_ref, xc_ref, z_ref, g_ref, gn_ref, sk_ref, c0_ref, n0_ref, m0_ref = refs[:11]
    o_ref, co_ref, no_ref, mo_ref = refs[-4:]
    for h in range(ML_HEADS):
        sl = slice(h * ML_DH, (h + 1) * ML_DH)
        hc, c_new, n_new, m_new = _ml_chunk(
            q_ref[:, sl], k_ref[:, sl], v_ref[:, sl], g_ref[0, h:h + 1, :], g_ref[0, ML_HEADS + h:ML_HEADS + h + 1, :],
            c0_ref[0, h], n0_ref[0, h:h + 1, :], m0_ref[0, h:h + 1, :])
        o_ref[:, sl] = _ml_gate_out(hc, gn_ref[:, sl], sk_ref[:, sl], xc_ref[:, sl], z_ref[:, sl])
        co_ref[0, h] = c_new
        no_ref[0, h:h + 1, :] = n_new
        mo_ref[0, h:h + 1, :] = m_new


def _ml_cell_sample(q, k, v, xc, xmz, gates, gn, skip, n_seq, seq_len, row0, c_all, n_all, m_all, layer, n_layers,
                    c_out_prev):
    blk0 = row0 // seq_len
    s0 = layer * n_seq
    row_spec = pl.BlockSpec((seq_len, ML_INNER), lambda b: (blk0 + b, 0))
    full = pl.BlockSpec((1, ML_INNER), lambda b: (0, 0))
    c_spec = pl.BlockSpec((1, ML_HEADS, ML_DH, ML_DH), lambda b: (s0 + b, 0, 0, 0))
    in_specs = [row_spec, row_spec, row_spec, row_spec,
                pl.BlockSpec((seq_len, ML_INNER), lambda b: (blk0 + b, 1)),
                pl.BlockSpec((1, 2 * ML_HEADS, seq_len), lambda b: (b, 0, 0)), full, full,
                c_spec,
                pl.BlockSpec((1, ML_HEADS, ML_DH), lambda b: (s0 + b, 0, 0)),
                pl.BlockSpec((1, ML_HEADS, 1), lambda b: (s0 + b, 0, 0))]
    args = [q, k, v, xc, xmz, gates, gn, skip, c_all, n_all, m_all]
    aliases = {}
    if c_out_prev is not None:
        in_specs.append(pl.BlockSpec(memory_space=pl.ANY))
        args.append(c_out_prev)
        aliases[len(args) - 1] = 1
    return pl.pallas_call(
        _ml_cell_sample_body,
        grid=(n_seq,),
        in_specs=in_specs,
        out_specs=[pl.BlockSpec((seq_len, ML_INNER), lambda b: (b, 0)), c_spec,
                   pl.BlockSpec((1, ML_HEADS, ML_DH), lambda b: (b, 0, 0)),
                   pl.BlockSpec((1, ML_HEADS, 1), lambda b: (b, 0, 0))],
        out_shape=[jax.ShapeDtypeStruct((n_seq * seq_len, ML_INNER), F32),
                   jax.ShapeDtypeStruct((n_layers * n_seq, ML_HEADS, ML_DH, ML_DH), F32),
                   jax.ShapeDtypeStruct((n_seq, ML_HEADS, ML_DH), F32),
                   jax.ShapeDtypeStruct((n_seq, ML_HEADS, 1), F32)],
        input_output_aliases=aliases,
        compiler_params=_params(("arbitrary",), 48),
        name="ml_cell_sample",
    )(*args)


def _ml_weights(w_in, conv_w, conv_b, wq, wk, wv, w_i, b_i, w_f, b_f, gn_g, skip, w_out):
    nt = ML_INNER // V7X_MXU_DIM
    per = V7X_MXU_DIM // ML_QKV_BLOCK
    eye = jnp.eye(per, dtype=F32)

    def block_diag(w):
        wt = w.reshape(nt, per, ML_QKV_BLOCK, ML_QKV_BLOCK)
        return jnp.einsum("tncd,nm->tncmd", wt, eye).reshape(nt, V7X_MXU_DIM, V7X_MXU_DIM).astype(BF16)

    wg = jnp.concatenate([w_i, w_f], axis=1).T.astype(BF16)
    return dict(w_in=w_in.astype(BF16), conv_w=conv_w, conv_b=conv_b.reshape(1, ML_INNER),
                wq=block_diag(wq), wk=block_diag(wk), wv=block_diag(wv),
                wgq=wg[:, :ML_INNER], wgk=wg[:, ML_INNER:2 * ML_INNER], wgv=wg[:, 2 * ML_INNER:],
                gb=jnp.concatenate([b_i, b_f]).reshape(2 * ML_HEADS, 1),
                gn=gn_g.reshape(1, ML_INNER), skip=skip.reshape(1, ML_INNER), w_out=w_out.astype(BF16))


def _ml_layer(x_all, w, dims, c_all, n_all, m_all, state_conv, layer, n_layers, c_out_prev, ln_g, ln_b):
    n_p, seq_len, n_s, sample_len = dims
    t_p = n_p * seq_len
    rb = ROW_BLOCK
    xmz = _mm(x_all, w["w_in"], 2048)
    pad = jnp.pad(state_conv, ((0, 0), (0, sample_len - (ML_CONV - 1)), (0, 0))).reshape(n_s * sample_len, ML_INNER)
    conv_rows = jnp.concatenate([jnp.zeros((rb, ML_INNER), F32), pad], axis=0)
    q, k, v, xc, g3 = _ml_pre(xmz, conv_rows, w, t_p, seq_len, sample_len)
    npb = t_p // rb
    g_p = g3[:npb]
    if ML_CHUNK != rb:
        g_p = g_p.reshape(npb, 2 * ML_HEADS, rb // ML_CHUNK, ML_CHUNK).transpose(0, 2, 1, 3).reshape(-1, 2 * ML_HEADS, ML_CHUNK)
    g_s = g3[npb:].reshape(-1, 2 * ML_HEADS, rb // sample_len, sample_len).transpose(0, 2, 1, 3)
    g_s = g_s.reshape(n_s, 2 * ML_HEADS, sample_len)
    gated_p, pc, pn, pm = _ml_cell(q, k, v, xc, xmz, g_p, w["gn"], w["skip"], n_p, seq_len, ML_CHUNK)
    gated_s, c_out, sn, sm = _ml_cell_sample(q, k, v, xc, xmz, g_s, w["gn"], w["skip"], n_s, sample_len, t_p,
                                             c_all, n_all, m_all, layer, n_layers, c_out_prev)
    h1 = _mm_ln(gated_p, gated_s, w["w_out"], x_all, ln_g, ln_b)
    tail = ML_CONV - 1
    p_conv = jnp.stack([xmz[(b + 1) * seq_len - tail:(b + 1) * seq_len, :ML_INNER] for b in range(n_p)])
    s_conv = xmz[t_p:, :ML_INNER].reshape(n_s, sample_len, ML_INNER)[:, sample_len - tail:]
    return h1, (pc, pn, pm, p_conv), (sn, sm.reshape(n_s, ML_HEADS), s_conv), c_out


def _gla_in_body(x_ref, w_ref, wa_ref, wa2_ref, ba_ref, o_ref, la_ref):
    x = x_ref[...].astype(BF16)
    o_ref[...] = jnp.dot(x, w_ref[...], preferred_element_type=F32)
    a = jnp.dot(x, wa_ref[...], preferred_element_type=F32)
    la_ref[...] = _log_sigmoid(_dot(a, wa2_ref[...]) + ba_ref[...]) / GLA_TAU


def _gla_in(x, w):
    m, k = x.shape
    tm = ROW_BLOCK
    n1 = 2 * GLA_KT + 2 * GLA_VT
    full = lambda shape: pl.BlockSpec(shape, lambda i: (0, 0))
    return pl.pallas_call(
        _gla_in_body,
        grid=(m // tm,),
        in_specs=[pl.BlockSpec((tm, k), lambda i: (i, 0)), full((k, n1)), full((k, V7X_LANES)),
                  full((V7X_LANES, GLA_KT)), full((1, GLA_KT))],
        out_specs=[pl.BlockSpec((tm, n1), lambda i: (i, 0)), pl.BlockSpec((tm, GLA_KT), lambda i: (i, 0))],
        out_shape=[jax.ShapeDtypeStruct((m, n1), F32), jax.ShapeDtypeStruct((m, GLA_KT), F32)],
        compiler_params=_params(("arbitrary",), 48),
        name="gla_in",
    )(x, w["w_qkvr"], w["w_a"], w["w_a2"], w["b_a"])


def _gla_levels(chunk):
    t = jnp.arange(chunk)[:, None]
    s = jnp.arange(chunk)[None, :]
    mats = [(s <= t)]
    size = chunk
    while size >= 2:
        ref = t - t % size + size // 2 - 1
        mats.append(s <= ref)
        size //= 2
    levels = jnp.concatenate(mats, axis=0).astype(BF16)
    return jnp.pad(levels, ((0, 0), (0, max(V7X_LANES - chunk, 0))))


def _gla_chunk(q, k, v, la, lv, st):
    L = q.shape[0]
    kpad = lv.shape[1]
    parts = jnp.concatenate(_split3(la), axis=1)
    if kpad > L:
        parts = jnp.concatenate([parts, jnp.zeros((kpad - L, parts.shape[1]), BF16)], axis=0)
    cum = jnp.dot(lv, parts, preferred_element_type=F32)
    cum = cum[:, :GLA_DK] + cum[:, GLA_DK:2 * GLA_DK] + cum[:, 2 * GLA_DK:]
    b = cum[0:L]
    ri = lax.broadcasted_iota(jnp.int32, (L, L), 0)
    ci = lax.broadcasted_iota(jnp.int32, (L, L), 1)
    tpos = lax.broadcasted_iota(jnp.int32, (L, 1), 0)
    att = jnp.where(ri == ci, jnp.sum(q * k, axis=1, keepdims=True), 0.0)
    size = L
    lvl = 1
    while size >= 2:
        bref = cum[lvl * L:(lvl + 1) * L]
        upper = (tpos % size) >= (size // 2)
        qs = q * jnp.exp(jnp.where(upper, b - bref, -jnp.inf))
        ks = k * jnp.exp(jnp.where(upper, -jnp.inf, bref - b))
        att = att + jnp.where((ri // size) == (ci // size), _dot_nt(qs, ks), 0.0)
        size //= 2
        lvl += 1
    o = _dot(att, v) + _dot(q * jnp.exp(b), st)
    b_end = b[L - 1:L, :]
    e_end = jnp.exp(b_end)
    rk = lax.broadcasted_iota(jnp.int32, (GLA_DK, GLA_DK), 0)
    ck = lax.broadcasted_iota(jnp.int32, (GLA_DK, GLA_DK), 1)
    e_col = jnp.sum(jnp.where(rk == ck, e_end, 0.0), axis=1, keepdims=True)
    st_new = e_col * st + _dot_tn(k * jnp.exp(b_end - b), v)
    return o, st_new


def _gla_heads(x_ref, la_ref, lv_ref, gn_ref, o_ref, read_state, write_state):
    lv = lv_ref[...]
    for h in range(GLA_HEADS):
        qs = slice(h * GLA_DK, (h + 1) * GLA_DK)
        ks = slice(GLA_KT + h * GLA_DK, GLA_KT + (h + 1) * GLA_DK)
        vs = slice(2 * GLA_KT + h * GLA_DV, 2 * GLA_KT + (h + 1) * GLA_DV)
        rs = slice(2 * GLA_KT + GLA_VT + h * GLA_DV, 2 * GLA_KT + GLA_VT + (h + 1) * GLA_DV)
        os_ = slice(h * GLA_DV, (h + 1) * GLA_DV)
        o, st_new = _gla_chunk(x_ref[:, qs] * (GLA_DK ** -0.5), x_ref[:, ks], x_ref[:, vs], la_ref[:, qs], lv,
                               read_state(h))
        o_ref[:, os_] = _silu(x_ref[:, rs]) * _head_norm(o, gn_ref[:, os_])
        write_state(h, st_new)


def _gla_cell_body(x_ref, la_ref, lv_ref, gn_ref, o_ref, so_ref, ss):
    c = pl.program_id(1)
    nc = pl.num_programs(1)

    @pl.when(c == 0)
    def _():
        ss[...] = jnp.zeros(ss.shape, F32)

    def write(h, st):
        ss[h] = st

    _gla_heads(x_ref, la_ref, lv_ref, gn_ref, o_ref, lambda h: ss[h], write)

    @pl.when(c == nc - 1)
    def _():
        so_ref[0] = ss[...]


def _gla_cell(qkvr, la, gn, n_seq, seq_len, chunk):
    n1 = qkvr.shape[1]
    nc = seq_len // chunk
    levels = _gla_levels(chunk)
    return pl.pallas_call(
        _gla_cell_body,
        grid=(n_seq, nc),
        in_specs=[pl.BlockSpec((chunk, n1), lambda b, c: (b * nc + c, 0)),
                  pl.BlockSpec((chunk, GLA_KT), lambda b, c: (b * nc + c, 0)),
                  pl.BlockSpec(levels.shape, lambda b, c: (0, 0)),
                  pl.BlockSpec((1, GLA_VT), lambda b, c: (0, 0))],
        out_specs=[pl.BlockSpec((chunk, GLA_VT), lambda b, c: (b * nc + c, 0)),
                   pl.BlockSpec((1, GLA_HEADS, GLA_DK, GLA_DV), lambda b, c: (b, 0, 0, 0))],
        out_shape=[jax.ShapeDtypeStruct((n_seq * seq_len, GLA_VT), F32),
                   jax.ShapeDtypeStruct((n_seq, GLA_HEADS, GLA_DK, GLA_DV), F32)],
        scratch_shapes=[pltpu.VMEM((GLA_HEADS, GLA_DK, GLA_DV), F32)],
        compiler_params=_params(("arbitrary", "arbitrary"), 32),
        name="gla_cell",
    )(qkvr, la, levels, gn)


def _gla_cell_sample_body(x_ref, la_ref, lv_ref, gn_ref, s0_ref, o_ref, so_ref):
    def write(h, st):
        so_ref[0, h] = st

    _gla_heads(x_ref, la_ref, lv_ref, gn_ref, o_ref, lambda h: s0_ref[0, h], write)


def _gla_cell_sample(qkvr, la, gn, n_seq, seq_len, row0, state):
    n1 = qkvr.shape[1]
    blk0 = row0 // seq_len
    levels = _gla_levels(seq_len)
    st_spec = pl.BlockSpec((1, GLA_HEADS, GLA_DK, GLA_DV), lambda b: (b, 0, 0, 0))
    return pl.pallas_call(
        _gla_cell_sample_body,
        grid=(n_seq,),
        in_specs=[pl.BlockSpec((seq_len, n1), lambda b: (blk0 + b, 0)),
                  pl.BlockSpec((seq_len, GLA_KT), lambda b: (blk0 + b, 0)),
                  pl.BlockSpec(levels.shape, lambda b: (0, 0)),
                  pl.BlockSpec((1, GLA_VT), lambda b: (0, 0)),
                  st_spec],
        out_specs=[pl.BlockSpec((seq_len, GLA_VT), lambda b: (b, 0)), st_spec],
        out_shape=[jax.ShapeDtypeStruct((n_seq * seq_len, GLA_VT), F32),
                   jax.ShapeDtypeStruct((n_seq, GLA_HEADS, GLA_DK, GLA_DV), F32)],
        compiler_params=_params(("arbitrary",), 32),
        name="gla_cell_sample",
    )(qkvr, la, levels, gn, state)


def _gla_weights(w_in, w_a2, b_a, gn_g, w_out):
    n1 = 2 * GLA_KT + 2 * GLA_VT
    w_a = jnp.pad(w_in[:, n1:], ((0, 0), (0, V7X_LANES - GLA_RANK)))
    w_a2p = jnp.pad(w_a2, ((0, V7X_LANES - GLA_RANK), (0, 0)))
    return dict(w_qkvr=w_in[:, :n1].astype(BF16), w_a=w_a.astype(BF16), w_a2=w_a2p.astype(BF16),
                b_a=b_a.reshape(1, GLA_KT), gn=gn_g.reshape(1, GLA_VT), w_out=w_out.astype(BF16))


def _gla_layer(x_all, w, dims, state_s, ln_g, ln_b):
    n_p, seq_len, n_s, sample_len = dims
    t_p = n_p * seq_len
    qkvr, la = _gla_in(x_all, w)
    gated_p, p_s = _gla_cell(qkvr, la, w["gn"], n_p, seq_len, min(GLA_CHUNK, seq_len))
    gated_s, s_s = _gla_cell_sample(qkvr, la, w["gn"], n_s, sample_len, t_p, state_s)
    h1 = _mm_ln(gated_p, gated_s, w["w_out"], x_all, ln_g, ln_b)
    return h1, p_s, s_s


def _s5_body(x_ref, sre_ref, sim_ref, win_ref, bt_ref, cre_ref, cim_ref, lre_ref, lim_ref, d_ref, wglu_ref,
             g_ref, b_ref, o_ref, ore_ref, oim_ref, st_re, st_im, car_re, car_im, *, nb, lc):
    c = pl.program_id(1)
    nc = pl.num_programs(1)
    rows = nb * lc
    ntile = D_MODEL // S5_TILE_CH

    @pl.when(c == 0)
    def _():
        car_re[...] = sre_ref[...]
        car_im[...] = sim_ref[...]

    x = x_ref[...].reshape(rows, D_MODEL)
    u = _dot(x, win_ref[...])
    lpt = S5_TILE_ST // V7X_LANES
    for t in range(ntile):
        bu = _dot(u[:, t * S5_TILE_CH:(t + 1) * S5_TILE_CH], bt_ref[t])
        for j in range(lpt):
            st_re[t * lpt + j] = bu[:, j * V7X_LANES:(j + 1) * V7X_LANES]
            st_im[t * lpt + j] = bu[:, S5_TILE_ST + j * V7X_LANES:S5_TILE_ST + (j + 1) * V7X_LANES]

    for gidx in range(nb // V7X_SUBLANES):
        grp = slice(gidx * V7X_SUBLANES, (gidx + 1) * V7X_SUBLANES)
        for t in range(ntile):
            slabs = list(range(t * lpt, (t + 1) * lpt))
            lanes = [slice(j * V7X_LANES, (j + 1) * V7X_LANES) for j in slabs]
            lre = [lre_ref[:, ln] for ln in lanes]
            lim = [lim_ref[:, ln] for ln in lanes]

            def step(tok, carry, slabs=slabs, lre=lre, lim=lim, gidx=gidx):
                sel = pl.ds(pl.multiple_of(tok * nb + gidx * V7X_SUBLANES, V7X_SUBLANES), V7X_SUBLANES)
                out = []
                for n, j in enumerate(slabs):
                    pr, pi = carry[2 * n], carry[2 * n + 1]
                    nr = lre[n] * pr - lim[n] * pi + st_re[j, sel, :]
                    ni = lre[n] * pi + lim[n] * pr + st_im[j, sel, :]
                    st_re[j, sel, :] = nr
                    st_im[j, sel, :] = ni
                    out += [nr, ni]
                return tuple(out)

            init = []
            for ln in lanes:
                init += [car_re[grp, ln], car_im[grp, ln]]
            fin = lax.fori_loop(0, lc, step, tuple(init))
            for n, ln in enumerate(lanes):
                car_re[grp, ln] = fin[2 * n]
                car_im[grp, ln] = fin[2 * n + 1]

    ys = []
    for t in range(ntile):
        sre = jnp.concatenate([st_re[t * lpt + j] for j in range(lpt)], axis=1)
        sim = jnp.concatenate([st_im[t * lpt + j] for j in range(lpt)], axis=1)
        ys.append(_dot(sre, cre_ref[t]) - _dot(sim, cim_ref[t]))
    y = jnp.concatenate(ys, axis=1) + d_ref[...] * u
    y = jax.nn.gelu(y)
    vg = _dot(y, wglu_ref[...])
    mix = vg[:, :D_MODEL] * _sigmoid(vg[:, D_MODEL:])
    o_ref[...] = _layer_norm(DEEPNORM_ALPHA * x + mix, g_ref[...], b_ref[...]).reshape(lc, nb, D_MODEL)

    @pl.when(c == nc - 1)
    def _():
        ore_ref[...] = car_re[...]
        oim_ref[...] = car_im[...]


def _s5_call(x3, s_re, s_im, w, ln_g, ln_b, nb, lc):
    seq_len, n_seq, _ = x3.shape
    ntile = D_MODEL // S5_TILE_CH
    rows = nb * lc
    full2 = lambda shape: pl.BlockSpec(shape, lambda i, c: (0, 0))
    full3 = lambda shape: pl.BlockSpec(shape, lambda i, c: (0, 0, 0))
    st_spec = pl.BlockSpec((nb, S5_STATE), lambda i, c: (i, 0))
    x_spec = pl.BlockSpec((lc, nb, D_MODEL), lambda i, c: (c, i, 0))
    return pl.pallas_call(
        functools.partial(_s5_body, nb=nb, lc=lc),
        grid=(n_seq // nb, seq_len // lc),
        in_specs=[x_spec, st_spec, st_spec,
                  full2((D_MODEL, D_MODEL)), full3((ntile, S5_TILE_CH, 2 * S5_TILE_ST)),
                  full3((ntile, S5_TILE_ST, S5_TILE_CH)), full3((ntile, S5_TILE_ST, S5_TILE_CH)),
                  full2((V7X_SUBLANES, S5_STATE)), full2((V7X_SUBLANES, S5_STATE)), full2((1, D_MODEL)),
                  full2((D_MODEL, 2 * D_MODEL)), full2((1, D_MODEL)), full2((1, D_MODEL))],
        out_specs=[x_spec, st_spec, st_spec],
        out_shape=[jax.ShapeDtypeStruct(x3.shape, F32), jax.ShapeDtypeStruct((n_seq, S5_STATE), F32),
                   jax.ShapeDtypeStruct((n_seq, S5_STATE), F32)],
        scratch_shapes=[pltpu.VMEM((S5_STATE // V7X_LANES, rows, V7X_LANES), F32),
                        pltpu.VMEM((S5_STATE // V7X_LANES, rows, V7X_LANES), F32),
                        pltpu.VMEM((nb, S5_STATE), F32), pltpu.VMEM((nb, S5_STATE), F32)],
        compiler_params=_params(("arbitrary", "arbitrary"), 56),
        name="s5",
    )(x3, s_re, s_im, w["w_in"], w["b_tiles"], w["c_re"], w["c_im"], w["lam_re"], w["lam_im"], w["d"], w["w_glu"],
      ln_g.reshape(1, D_MODEL), ln_b.reshape(1, D_MODEL))


def _s5_weights(w_in, a_re, a_im, log_dt, b_re, b_im, c_re, c_im, d_skip, w_glu):
    lam = lax.complex(a_re.astype(F32), a_im.astype(F32))
    dt = jnp.exp(log_dt.astype(F32))[:, None]
    lam_bar = jnp.exp(lam * dt)
    b_bar = ((lam_bar - 1.0) / lam)[..., None] * lax.complex(b_re.astype(F32), b_im.astype(F32))
    ntile = D_MODEL // S5_TILE_CH
    gpt = S5_TILE_CH // S5_GC
    eye = jnp.eye(gpt, dtype=F32)

    def b_tiles(bb):
        return jnp.einsum("igpc,gh->igchp", bb.reshape(ntile, gpt, S5_P, S5_GC), eye).reshape(ntile, S5_TILE_CH, S5_TILE_ST)

    def c_tiles(cc):
        return jnp.einsum("igcp,gh->igphc", cc.reshape(ntile, gpt, S5_GC, S5_P), eye).reshape(ntile, S5_TILE_ST, S5_TILE_CH)

    bt = jnp.concatenate([b_tiles(b_bar.real), b_tiles(b_bar.imag)], axis=2).astype(BF16)
    bcast = lambda a: jnp.broadcast_to(a.reshape(1, S5_STATE), (V7X_SUBLANES, S5_STATE))
    return dict(w_in=w_in.astype(BF16), b_tiles=bt, c_re=c_tiles(c_re.astype(F32)).astype(BF16),
                c_im=c_tiles(c_im.astype(F32)).astype(BF16), lam_re=bcast(lam_bar.real), lam_im=bcast(lam_bar.imag),
                d=d_skip.reshape(1, D_MODEL).astype(F32), w_glu=w_glu.astype(BF16))


def _s5_layer(x_all, w, dims, state_re, state_im, ln_g, ln_b):
    n_p, seq_len, n_s, sample_len = dims
    t_p = n_p * seq_len
    xp = x_all[:t_p].reshape(n_p, seq_len, D_MODEL).transpose(1, 0, 2)
    xs = x_all[t_p:].reshape(n_s, sample_len, D_MODEL).transpose(1, 0, 2)
    zero = jnp.zeros((n_p, S5_STATE), F32)
    hp, p_re, p_im = _s5_call(xp, zero, zero, w, ln_g, ln_b, n_p, min(S5_CHUNK, seq_len))
    nb_s = min(n_s, ROW_BLOCK // sample_len)
    hs, s_re, s_im = _s5_call(xs, state_re.reshape(n_s, S5_STATE), state_im.reshape(n_s, S5_STATE), w, ln_g, ln_b,
                              nb_s, sample_len)
    h1 = jnp.concatenate([hp.transpose(1, 0, 2).reshape(t_p, D_MODEL),
                          hs.transpose(1, 0, 2).reshape(n_s * sample_len, D_MODEL)], axis=0)
    shp = lambda a, n: a.reshape(n, S5_GROUPS, S5_P)
    return h1, (shp(p_re, n_p), shp(p_im, n_p)), (shp(s_re, n_s), shp(s_im, n_s))


def _route_body(x_ref, wh_ref, wl_ref, br_ref, g_ref):
    x = x_ref[...]
    xh = x.astype(BF16)
    xl = (x - xh.astype(F32)).astype(BF16)
    wh = wh_ref[...]
    logits = _dot_nt(wh, xh) + _dot_nt(wh, xl) + _dot_nt(wl_ref[...], xh)
    s = _sigmoid(logits)
    work = s + br_ref[...]
    row = lax.broadcasted_iota(jnp.int32, s.shape, 0).astype(F32)
    chosen = jnp.zeros(s.shape, jnp.bool_)
    for _ in range(TOP_K):
        mx = jnp.max(work, axis=0, keepdims=True)
        idx = jnp.min(jnp.where(work == mx, row, float(N_EXPERTS)), axis=0, keepdims=True)
        hit = row == idx
        chosen = jnp.logical_or(chosen, hit)
        work = jnp.where(hit, -jnp.inf, work)
    sel = jnp.where(chosen, s, 0.0)
    g_ref[...] = sel / jnp.sum(sel, axis=0, keepdims=True) * ROUTE_SCALE


def _route(x, w):
    m, k = x.shape
    tm = ROW_BLOCK
    full = lambda shape: pl.BlockSpec(shape, lambda i: (0, 0))
    gates_t = pl.pallas_call(
        _route_body,
        grid=(m // tm,),
        in_specs=[pl.BlockSpec((tm, k), lambda i: (i, 0)), full((N_EXPERTS, k)), full((N_EXPERTS, k)),
                  full((N_EXPERTS, 1))],
        out_specs=pl.BlockSpec((N_EXPERTS, tm), lambda i: (0, i)),
        out_shape=jax.ShapeDtypeStruct((N_EXPERTS, m), F32),
        compiler_params=_params(("arbitrary",), 32),
        name="route",
    )(x, w["wr_hi"], w["wr_lo"], w["b_router"])
    return gates_t.T


def _moe_body(x_ref, gd_ref, wg_ref, wu_ref, wd_ref, sg_ref, su_ref, sd_ref, g_ref, b_ref, o_ref, acc, xb_ref):
    e = pl.program_id(1)
    ne = pl.num_programs(1)

    @pl.when(e == 0)
    def _():
        xb0 = x_ref[...].astype(BF16)
        xb_ref[...] = xb0
        hs = _silu(jnp.dot(xb0, sg_ref[...], preferred_element_type=F32)) * jnp.dot(xb0, su_ref[...], preferred_element_type=F32)
        acc[...] = _dot(hs, sd_ref[...])

    xb = xb_ref[...]
    gd = gd_ref[...]
    lane = lax.broadcasted_iota(jnp.int32, gd.shape, 1)
    total = None
    for j in range(MOE_EXPERTS_PER_STEP):
        h = _silu(jnp.dot(xb, wg_ref[j], preferred_element_type=F32)) * jnp.dot(xb, wu_ref[j], preferred_element_type=F32)
        gate = jnp.sum(jnp.where(lane == e * MOE_EXPERTS_PER_STEP + j, gd, 0.0), axis=1, keepdims=True)
        y = _dot(h, wd_ref[j]) * gate
        total = y if total is None else total + y
    acc[...] += total

    @pl.when(e == ne - 1)
    def _():
        o_ref[...] = _layer_norm(DEEPNORM_ALPHA * x_ref[...] + acc[...], g_ref[...], b_ref[...])


def _moe_ffn(x, gates, w, ln_g, ln_b):
    m, k = x.shape
    tm = MOE_ROWS
    full = lambda shape: pl.BlockSpec(shape, lambda i, e: (0, 0))
    return pl.pallas_call(
        _moe_body,
        grid=(m // tm, N_EXPERTS // MOE_EXPERTS_PER_STEP),
        in_specs=[pl.BlockSpec((tm, k), lambda i, e: (i, 0)), pl.BlockSpec((tm, N_EXPERTS), lambda i, e: (i, 0)),
                  pl.BlockSpec((MOE_EXPERTS_PER_STEP, k, EXPERT_FF), lambda i, e: (e, 0, 0)),
                  pl.BlockSpec((MOE_EXPERTS_PER_STEP, k, EXPERT_FF), lambda i, e: (e, 0, 0)),
                  pl.BlockSpec((MOE_EXPERTS_PER_STEP, EXPERT_FF, k), lambda i, e: (e, 0, 0)),
                  full((k, w["ws_gate"].shape[1])), full((k, w["ws_up"].shape[1])), full((w["ws_down"].shape[0], k)),
                  full((1, k)), full((1, k))],
        out_specs=pl.BlockSpec((tm, k), lambda i, e: (i, 0)),
        out_shape=jax.ShapeDtypeStruct((m, k), F32),
        scratch_shapes=[pltpu.VMEM((tm, k), F32), pltpu.VMEM((tm, k), BF16)],
        compiler_params=_params(("arbitrary", "arbitrary"), 48),
        name="moe_ffn",
    )(x, gates, w["w_gate"], w["w_up"], w["w_down"], w["ws_gate"], w["ws_up"], w["ws_down"],
      ln_g.reshape(1, k), ln_b.reshape(1, k))


def _moe_weights(w_router, b_router, w_gate, w_up, w_down, ws_gate, ws_up, ws_down):
    wr_t = w_router.T
    wr_hi = wr_t.astype(BF16)
    wr_lo = (wr_t - wr_hi.astype(F32)).astype(BF16)
    return dict(wr_hi=wr_hi, wr_lo=wr_lo, b_router=b_router.reshape(N_EXPERTS, 1).astype(F32),
                w_gate=w_gate.astype(BF16), w_up=w_up.astype(BF16), w_down=w_down.astype(BF16),
                ws_gate=ws_gate.astype(BF16), ws_up=ws_up.astype(BF16), ws_down=ws_down.astype(BF16))


def _moe_layer(x, w, ln_g, ln_b):
    return _moe_ffn(x, _route(x, w), w, ln_g, ln_b)


def kernel(x_prompt, x_sample, state_mlstm_C, state_mlstm_n, state_mlstm_m, state_mlstm_conv, state_gla_S, state_s5_re, state_s5_im, ln1_g, ln1_b, ln2_g, ln2_b, ml_w_in, ml_conv_w, ml_conv_b, ml_wq, ml_wk, ml_wv, ml_w_i, ml_b_i, ml_w_f, ml_b_f, ml_gn_g, ml_skip, ml_w_out, gla_w_in, gla_w_a2, gla_b_a, gla_gn_g, gla_w_out, s5_w_in, s5_a_re, s5_a_im, s5_log_dt, s5_b_re, s5_b_im, s5_c_re, s5_c_im, s5_d, s5_w_glu, moe_w_router, moe_b_router, moe_w_gate, moe_w_up, moe_w_down, moe_ws_gate, moe_ws_up, moe_ws_down):
    n_p, seq_len, _ = x_prompt.shape
    n_s, sample_len, _ = x_sample.shape
    dims = (n_p, seq_len, n_s, sample_len)
    t_p = n_p * seq_len
    t_s = n_s * sample_len
    assert seq_len % ROW_BLOCK == 0 and t_s % ROW_BLOCK == 0 and ROW_BLOCK % sample_len == 0
    assert sample_len == V7X_SUBLANES and n_p % V7X_SUBLANES == 0 and (t_p + t_s) % MOE_ROWS == 0
    x = jnp.concatenate([x_prompt.reshape(t_p, D_MODEL), x_sample.reshape(t_s, D_MODEL)], axis=0)
    n_ml = len(range(0, DEPTH, N_MIXERS))
    c_all = state_mlstm_C.reshape(-1, ML_HEADS, ML_DH, ML_DH)
    n_all = state_mlstm_n.reshape(-1, ML_HEADS, ML_DH)
    m_all = state_mlstm_m.reshape(-1, ML_HEADS, 1)
    c_out = None
    p_ml, s_ml, p_gla, s_gla, p_s5, s_s5 = [], [], [], [], [], []
    for i in range(DEPTH):
        j = i // N_MIXERS
        if i % N_MIXERS == 0:
            w = _ml_weights(ml_w_in[j], ml_conv_w[j], ml_conv_b[j], ml_wq[j], ml_wk[j], ml_wv[j], ml_w_i[j], ml_b_i[j],
                            ml_w_f[j], ml_b_f[j], ml_gn_g[j], ml_skip[j], ml_w_out[j])
            x, ps, ss, c_out = _ml_layer(x, w, dims, c_all, n_all, m_all, state_mlstm_conv[j], j, n_ml, c_out,
                                         ln1_g[i], ln1_b[i])
            p_ml.append(ps)
            s_ml.append(ss)
        elif i % N_MIXERS == 1:
            w = _gla_weights(gla_w_in[j], gla_w_a2[j], gla_b_a[j], gla_gn_g[j], gla_w_out[j])
            x, ps, ss = _gla_layer(x, w, dims, state_gla_S[j], ln1_g[i], ln1_b[i])
            p_gla.append(ps)
            s_gla.append(ss)
        else:
            w = _s5_weights(s5_w_in[j], s5_a_re[j], s5_a_im[j], s5_log_dt[j], s5_b_re[j], s5_b_im[j], s5_c_re[j],
                            s5_c_im[j], s5_d[j], s5_w_glu[j])
            x, ps, ss = _s5_layer(x, w, dims, state_s5_re[j], state_s5_im[j], ln1_g[i], ln1_b[i])
            p_s5.append(ps)
            s_s5.append(ss)
        wm = _moe_weights(moe_w_router[i], moe_b_router[i], moe_w_gate[i], moe_w_up[i], moe_w_down[i],
                          moe_ws_gate[i], moe_ws_up[i], moe_ws_down[i])
        x = _moe_layer(x, wm, ln2_g[i], ln2_b[i])
    stack = lambda items, idx: jnp.stack([it[idx] for it in items])
    return (x[:t_p].reshape(n_p, seq_len, D_MODEL), x[t_p:].reshape(n_s, sample_len, D_MODEL),
            stack(p_ml, 0), stack(p_ml, 1), stack(p_ml, 2), stack(p_ml, 3), jnp.stack(p_gla),
            stack(p_s5, 0), stack(p_s5, 1),
            c_out.reshape(n_ml, n_s, ML_HEADS, ML_DH, ML_DH), stack(s_ml, 0), stack(s_ml, 1), stack(s_ml, 2),
            jnp.stack(s_gla), stack(s_s5, 0), stack(s_s5, 1))
```

```python
import functools

import jax
import jax.numpy as jnp
from jax import lax
from jax.experimental import pallas as pl
from jax.experimental.pallas import tpu as pltpu

F32 = jnp.float32
BF16 = jnp.bfloat16

D_MODEL = 1024
DEPTH = 4
N_MIXERS = 3
ML_INNER = 2 * D_MODEL
ML_HEADS = 4
ML_DH = ML_INNER // ML_HEADS
ML_QKV_BLOCK = 4
ML_CONV = 4
GLA_HEADS = 4
GLA_KT = D_MODEL // 2
GLA_VT = D_MODEL
GLA_DK = GLA_KT // GLA_HEADS
GLA_DV = GLA_VT // GLA_HEADS
GLA_RANK = 16
GLA_TAU = 16.0
S5_GC = 16
S5_GROUPS = D_MODEL // S5_GC
S5_P = 64
S5_STATE = S5_GROUPS * S5_P
N_EXPERTS = 64
TOP_K = 8
EXPERT_FF = 256
ROUTE_SCALE = 2.5
DEEPNORM_ALPHA = (2.0 * DEPTH) ** 0.25
NORM_EPS = 1e-5

V7X_VMEM_BYTES = 64 * 1024 * 1024
V7X_LANES = 128
V7X_SUBLANES = 8
V7X_MXU_DIM = 256

ROW_BLOCK = 256
ML_CHUNK = 256
GLA_CHUNK = 128
S5_CHUNK = 32
S5_TILE_CH = 128
S5_TILE_ST = S5_TILE_CH // S5_GC * S5_P
MOE_ROWS = 1024
MOE_EXPERTS_PER_STEP = 4


def _params(semantics, vmem_mb):
    assert vmem_mb * 1024 * 1024 < V7X_VMEM_BYTES
    return pltpu.CompilerParams(dimension_semantics=semantics, vmem_limit_bytes=vmem_mb * 1024 * 1024)


def _dot(a, b):
    return jnp.dot(a.astype(BF16), b.astype(BF16), preferred_element_type=F32)


def _dot_nt(a, b):
    return lax.dot_general(a.astype(BF16), b.astype(BF16), (((1,), (1,)), ((), ())), preferred_element_type=F32)


def _dot_tn(a, b):
    return lax.dot_general(a.astype(BF16), b.astype(BF16), (((0,), (0,)), ((), ())), preferred_element_type=F32)


def _sigmoid(x):
    return 1.0 / (1.0 + jnp.exp(-x))


def _silu(x):
    return x * _sigmoid(x)


def _log_sigmoid(x):
    return jnp.minimum(x, 0.0) - jnp.log1p(jnp.exp(-jnp.abs(x)))


def _layer_norm(y, g, b):
    mu = jnp.mean(y, axis=-1, keepdims=True)
    d = y - mu
    var = jnp.mean(d * d, axis=-1, keepdims=True)
    return d * lax.rsqrt(var + NORM_EPS) * g + b


def _head_norm(h, g):
    mu = jnp.mean(h, axis=-1, keepdims=True)
    d = h - mu
    var = jnp.mean(d * d, axis=-1, keepdims=True)
    return d * lax.rsqrt(var + NORM_EPS) * g


def _split3(x):
    hi = x.astype(BF16)
    r1 = x - hi.astype(F32)
    mid = r1.astype(BF16)
    lo = (r1 - mid.astype(F32)).astype(BF16)
    return hi, mid, lo


def _mm_body(x_ref, w_ref, o_ref):
    o_ref[...] = _dot(x_ref[...], w_ref[...])


def _mm(x, w, tn):
    m, k = x.shape
    n = w.shape[1]
    tm = ROW_BLOCK
    return pl.pallas_call(
        _mm_body,
        grid=(n // tn, m // tm),
        in_specs=[pl.BlockSpec((tm, k), lambda j, i: (i, 0)), pl.BlockSpec((k, tn), lambda j, i: (0, j))],
        out_specs=pl.BlockSpec((tm, tn), lambda j, i: (i, j)),
        out_shape=jax.ShapeDtypeStruct((m, n), F32),
        compiler_params=_params(("arbitrary", "arbitrary"), 40),
        name="mm",
    )(x, w)


def _mm_ln_body(ap_ref, as_ref, w_ref, r_ref, g_ref, b_ref, o_ref, *, n_prompt_blocks):
    i = pl.program_id(0)

    def run(a_ref):
        y = _dot(a_ref[...], w_ref[...])
        o_ref[...] = _layer_norm(DEEPNORM_ALPHA * r_ref[...] + y, g_ref[...], b_ref[...])

    pl.when(i < n_prompt_blocks)(lambda: run(ap_ref))
    pl.when(i >= n_prompt_blocks)(lambda: run(as_ref))


def _mm_ln(a_p, a_s, w, resid, g, b):
    k = a_p.shape[1]
    m = a_p.shape[0] + a_s.shape[0]
    n = w.shape[1]
    tm = ROW_BLOCK
    npb = a_p.shape[0] // tm
    return pl.pallas_call(
        functools.partial(_mm_ln_body, n_prompt_blocks=npb),
        grid=(m // tm,),
        in_specs=[pl.BlockSpec((tm, k), lambda i: (jnp.minimum(i, npb - 1), 0)),
                  pl.BlockSpec((tm, k), lambda i: (jnp.maximum(i - npb, 0), 0)),
                  pl.BlockSpec((k, n), lambda i: (0, 0)),
                  pl.BlockSpec((tm, n), lambda i: (i, 0)), pl.BlockSpec((1, n), lambda i: (0, 0)),
                  pl.BlockSpec((1, n), lambda i: (0, 0))],
        out_specs=pl.BlockSpec((tm, n), lambda i: (i, 0)),
        out_shape=jax.ShapeDtypeStruct((m, n), F32),
        compiler_params=_params(("arbitrary",), 40),
        name="mm_ln",
    )(a_p, a_s, w, resid, g.reshape(1, n), b.reshape(1, n))


def _ml_pre_body(x_ref, prev_ref, p_ref, cw_ref, cb_ref, wq_ref, wk_ref, wv_ref, wgq_ref, wgk_ref, wgv_ref, gb_ref,
                 q_ref, k_ref, v_ref, xc_ref, g_ref, scx, scp, *, n_prompt_blocks, blocks_per_seq, sample_len):
    i = pl.program_id(0)
    rows = x_ref.shape[0]
    is_prompt = i < n_prompt_blocks
    no_prev = jnp.logical_or(i % blocks_per_seq == 0, jnp.logical_not(is_prompt))
    x = x_ref[...]
    scx[0:V7X_SUBLANES, :] = jnp.where(no_prev, 0.0, prev_ref[...])
    scx[V7X_SUBLANES:V7X_SUBLANES + rows, :] = x

    @pl.when(is_prompt)
    def _():
        acc = cb_ref[...] + x * cw_ref[ML_CONV - 1:ML_CONV, :]
        for s in range(1, ML_CONV):
            acc = acc + scx[V7X_SUBLANES - s:V7X_SUBLANES - s + rows, :] * cw_ref[ML_CONV - 1 - s:ML_CONV - s, :]
        scp[0:rows, :] = acc

    @pl.when(jnp.logical_not(is_prompt))
    def _():
        scp[0:rows, :] = p_ref[...]
        scp[rows:rows + V7X_SUBLANES, :] = jnp.zeros((V7X_SUBLANES, x.shape[1]), F32)
        tpos = lax.broadcasted_iota(jnp.int32, (rows, 1), 0) % sample_len
        acc = cb_ref[...] + x * cw_ref[ML_CONV - 1:ML_CONV, :]
        for s in range(1, ML_CONV):
            xs = scx[V7X_SUBLANES - s:V7X_SUBLANES - s + rows, :]
            ps = scp[ML_CONV - 1 - s:ML_CONV - 1 - s + rows, :]
            acc = acc + jnp.where(tpos >= s, xs, ps) * cw_ref[ML_CONV - 1 - s:ML_CONV - s, :]
        scp[0:rows, :] = acc

    xc = _silu(scp[0:rows, :])
    xc_ref[...] = xc
    g = jnp.zeros((2 * ML_HEADS, rows), F32)
    for t in range(ML_INNER // V7X_MXU_DIM):
        sl = slice(t * V7X_MXU_DIM, (t + 1) * V7X_MXU_DIM)
        q_t = _dot(xc[:, sl], wq_ref[t])
        k_t = _dot(xc[:, sl], wk_ref[t])
        v_t = _dot(x[:, sl], wv_ref[t])
        q_ref[:, sl] = q_t
        k_ref[:, sl] = k_t * (ML_DH ** -0.5)
        v_ref[:, sl] = v_t
        g = g + _dot_nt(wgq_ref[:, sl], q_t) + _dot_nt(wgk_ref[:, sl], k_t) + _dot_nt(wgv_ref[:, sl], v_t)
    g = g + gb_ref[...]
    gate_row = lax.broadcasted_iota(jnp.int32, g.shape, 0)
    g_ref[0] = jnp.where(gate_row < ML_HEADS, g, _log_sigmoid(g))


def _ml_pre(xmz, conv_rows, w, n_prompt_rows, seq_len, sample_len):
    t_all = xmz.shape[0]
    rb = ROW_BLOCK
    nblk = t_all // rb
    npb = n_prompt_rows // rb
    bps = seq_len // rb
    sub = rb // V7X_SUBLANES
    body = functools.partial(_ml_pre_body, n_prompt_blocks=npb, blocks_per_seq=bps, sample_len=sample_len)
    full2 = lambda shape: pl.BlockSpec(shape, lambda i: (0, 0))
    full3 = lambda shape: pl.BlockSpec(shape, lambda i: (0, 0, 0))
    nt = ML_INNER // V7X_MXU_DIM
    row_spec = pl.BlockSpec((rb, ML_INNER), lambda i: (i, 0))
    return pl.pallas_call(
        body,
        grid=(nblk,),
        in_specs=[row_spec,
                  pl.BlockSpec((V7X_SUBLANES, ML_INNER), lambda i: (jnp.maximum(i * sub - 1, 0), 0)),
                  pl.BlockSpec((rb, ML_INNER), lambda i: (jnp.maximum(i - npb + 1, 0), 0)),
                  full2((ML_CONV, ML_INNER)), full2((1, ML_INNER)),
                  full3((nt, V7X_MXU_DIM, V7X_MXU_DIM)), full3((nt, V7X_MXU_DIM, V7X_MXU_DIM)),
                  full3((nt, V7X_MXU_DIM, V7X_MXU_DIM)),
                  full2((2 * ML_HEADS, ML_INNER)), full2((2 * ML_HEADS, ML_INNER)), full2((2 * ML_HEADS, ML_INNER)),
                  full2((2 * ML_HEADS, 1))],
        out_specs=[row_spec, row_spec, row_spec, row_spec,
                   pl.BlockSpec((1, 2 * ML_HEADS, rb), lambda i: (i, 0, 0))],
        out_shape=[jax.ShapeDtypeStruct((t_all, ML_INNER), F32)] * 4
        + [jax.ShapeDtypeStruct((nblk, 2 * ML_HEADS, rb), F32)],
        scratch_shapes=[pltpu.VMEM((rb + V7X_SUBLANES, ML_INNER), F32), pltpu.VMEM((rb + V7X_SUBLANES, ML_INNER), F32)],
        compiler_params=_params(("arbitrary",), 48),
        name="ml_pre",
    )(xmz, xmz, conv_rows, w["conv_w"], w["conv_b"], w["wq"], w["wk"], w["wv"], w["wgq"], w["wgk"], w["wgv"], w["gb"])


def _ml_masks(L):
    ri = lax.broadcasted_iota(jnp.int32, (L, L), 0)
    ci = lax.broadcasted_iota(jnp.int32, (L, L), 1)
    return ri == ci, ci <= ri, ri <= ci


def _ml_chunk(q, k, v, ip, fl, c_prev, n_prev, m_prev, masks):
    L = q.shape[0]
    eye, tril, triu = masks
    f_col = jnp.sum(jnp.where(eye, fl, 0.0), axis=1, keepdims=True)
    b_col = jnp.sum(jnp.where(tril, fl, 0.0), axis=1, keepdims=True)
    b_row = jnp.sum(jnp.where(triu, f_col, 0.0), axis=0, keepdims=True)
    ib = ip - b_row
    d = jnp.where(tril, b_col + ib, -jnp.inf)
    m_inter = b_col + m_prev
    m_t = jnp.maximum(m_inter, jnp.max(d, axis=1, keepdims=True))
    a_inter = jnp.exp(m_inter - m_t)
    s = _dot_nt(q, k) * jnp.exp(d - m_t)
    num = _dot(s, v) + a_inter * _dot(q, c_prev)
    den = jnp.sum(s, axis=1, keepdims=True) + a_inter * jnp.sum(q * n_prev, axis=1, keepdims=True)
    hc = num * (1.0 / jnp.maximum(jnp.abs(den), jnp.exp(-m_t)))
    m_new = m_t[L - 1:L, :]
    b_last = b_row[:, L - 1:L]
    w_row = jnp.exp(b_last + ib - m_new)
    w_col = jnp.sum(jnp.where(eye, w_row, 0.0), axis=1, keepdims=True)
    a_end = jnp.exp(b_last + m_prev - m_new)
    kw = k * w_col
    c_new = a_end * c_prev + _dot_tn(kw, v)
    n_new = a_end * n_prev + jnp.sum(kw, axis=0, keepdims=True)
    return hc, c_new, n_new, m_new


def _ml_gate_out(hc, gn, sk, xc, z):
    return (_head_norm(hc, gn) + sk * xc) * _silu(z)


def _ml_cell_body(q_ref, k_ref, v_ref, xc_ref, z_ref, g_ref, gn_ref, sk_ref, o_ref, co_ref, no_ref, mo_ref, cs, ns, ms):
    c = pl.program_id(1)
    nc = pl.num_programs(1)

    @pl.when(c == 0)
    def _():
        cs[...] = jnp.zeros(cs.shape, F32)
        ns[...] = jnp.zeros(ns.shape, F32)
        ms[...] = jnp.zeros(ms.shape, F32)

    masks = _ml_masks(q_ref.shape[0])
    for h in range(ML_HEADS):
        sl = slice(h * ML_DH, (h + 1) * ML_DH)
        hc, c_new, n_new, m_new = _ml_chunk(
            q_ref[:, sl], k_ref[:, sl], v_ref[:, sl], g_ref[0, h:h + 1, :], g_ref[0, ML_HEADS + h:ML_HEADS + h + 1, :],
            cs[h], ns[h:h + 1, :], ms[h:h + 1, :], masks)
        cs[h] = c_new
        ns[h:h + 1, :] = n_new
        ms[h:h + 1, :] = m_new
        o_ref[:, sl] = _ml_gate_out(hc, gn_ref[:, sl], sk_ref[:, sl], xc_ref[:, sl], z_ref[:, sl])

    @pl.when(c == nc - 1)
    def _():
        co_ref[0] = cs[...]
        no_ref[0] = ns[...]
        mo_ref[0] = ms[...]


def _ml_cell(q, k, v, xc, xmz, gates, gn, skip, n_seq, seq_len, chunk):
    t_all = n_seq * seq_len
    nc = seq_len // chunk
    row_spec = pl.BlockSpec((chunk, ML_INNER), lambda b, c: (b * nc + c, 0))
    full = pl.BlockSpec((1, ML_INNER), lambda b, c: (0, 0))
    out = pl.pallas_call(
        _ml_cell_body,
        grid=(n_seq, nc),
        in_specs=[row_spec, row_spec, row_spec, row_spec,
                  pl.BlockSpec((chunk, ML_INNER), lambda b, c: (b * nc + c, 1)),
                  pl.BlockSpec((1, 2 * ML_HEADS, chunk), lambda b, c: (b * nc + c, 0, 0)), full, full],
        out_specs=[row_spec,
                   pl.BlockSpec((1, ML_HEADS, ML_DH, ML_DH), lambda b, c: (b, 0, 0, 0)),
                   pl.BlockSpec((1, ML_HEADS, ML_DH), lambda b, c: (b, 0, 0)),
                   pl.BlockSpec((1, ML_HEADS, 1), lambda b, c: (b, 0, 0))],
        out_shape=[jax.ShapeDtypeStruct((t_all, ML_INNER), F32),
                   jax.ShapeDtypeStruct((n_seq, ML_HEADS, ML_DH, ML_DH), F32),
                   jax.ShapeDtypeStruct((n_seq, ML_HEADS, ML_DH), F32),
                   jax.ShapeDtypeStruct((n_seq, ML_HEADS, 1), F32)],
        scratch_shapes=[pltpu.VMEM((ML_HEADS, ML_DH, ML_DH), F32), pltpu.VMEM((ML_HEADS, ML_DH), F32),
                        pltpu.VMEM((ML_HEADS, 1), F32)],
        compiler_params=_params(("arbitrary", "arbitrary"), 56),
        name="ml_cell",
    )(q, k, v, xc, xmz, gates, gn, skip)
    gated, c_new, n_new, m_new = out
    return gated, c_new, n_new, m_new.reshape(n_seq, ML_HEADS)


def _ml_cell_sample_body(*refs):
    q_ref, k_ref, v_ref, xc_ref, z_ref, g_ref, gn_ref, sk_ref, c0_ref, n0_ref, m0_ref = refs[:11]
    o_ref, co_ref, no_ref, mo_ref = refs[-4:]
    masks = _ml_masks(q_ref.shape[0])
    for h in range(ML_HEADS):
        sl = slice(h * ML_DH, (h + 1) * ML_DH)
        hc, c_new, n_new, m_new = _ml_chunk(
            q_ref[:, sl], k_ref[:, sl], v_ref[:, sl], g_ref[0, h:h + 1, :], g_ref[0, ML_HEADS + h:ML_HEADS + h + 1, :],
            c0_ref[0, h], n0_ref[0, h:h + 1, :], m0_ref[0, h:h + 1, :], masks)
        o_ref[:, sl] = _ml_gate_out(hc, gn_ref[:, sl], sk_ref[:, sl], xc_ref[:, sl], z_ref[:, sl])
        co_ref[0, h] = c_new
        no_ref[0, h:h + 1, :] = n_new
        mo_ref[0, h:h + 1, :] = m_new


def _ml_cell_sample(q, k, v, xc, xmz, gates, gn, skip, n_seq, seq_len, row0, c_all, n_all, m_all, layer, n_layers,
                    c_out_prev):
    blk0 = row0 // seq_len
    s0 = layer * n_seq
    row_spec = pl.BlockSpec((seq_len, ML_INNER), lambda b: (blk0 + b, 0))
    full = pl.BlockSpec((1, ML_INNER), lambda b: (0, 0))
    c_spec = pl.BlockSpec((1, ML_HEADS, ML_DH, ML_DH), lambda b: (s0 + b, 0, 0, 0))
    in_specs = [row_spec, row_spec, row_spec, row_spec,
                pl.BlockSpec((seq_len, ML_INNER), lambda b: (blk0 + b, 1)),
                pl.BlockSpec((1, 2 * ML_HEADS, seq_len), lambda b: (b, 0, 0)), full, full,
                c_spec,
                pl.BlockSpec((1, ML_HEADS, ML_DH), lambda b: (s0 + b, 0, 0)),
                pl.BlockSpec((1, ML_HEADS, 1), lambda b: (s0 + b, 0, 0))]
    args = [q, k, v, xc, xmz, gates, gn, skip, c_all, n_all, m_all]
    aliases = {}
    if c_out_prev is not None:
        in_specs.append(pl.BlockSpec(memory_space=pl.ANY))
        args.append(c_out_prev)
        aliases[len(args) - 1] = 1
    return pl.pallas_call(
        _ml_cell_sample_body,
        grid=(n_seq,),
        in_specs=in_specs,
        out_specs=[pl.BlockSpec((seq_len, ML_INNER), lambda b: (b, 0)), c_spec,
                   pl.BlockSpec((1, ML_HEADS, ML_DH), lambda b: (b, 0, 0)),
                   pl.BlockSpec((1, ML_HEADS, 1), lambda b: (b, 0, 0))],
        out_shape=[jax.ShapeDtypeStruct((n_seq * seq_len, ML_INNER), F32),
                   jax.ShapeDtypeStruct((n_layers * n_seq, ML_HEADS, ML_DH, ML_DH), F32),
                   jax.ShapeDtypeStruct((n_seq, ML_HEADS, ML_DH), F32),
                   jax.ShapeDtypeStruct((n_seq, ML_HEADS, 1), F32)],
        input_output_aliases=aliases,
        compiler_params=_params(("arbitrary",), 48),
        name="ml_cell_sample",
    )(*args)


def _ml_weights(w_in, conv_w, conv_b, wq, wk, wv, w_i, b_i, w_f, b_f, gn_g, skip, w_out):
    nt = ML_INNER // V7X_MXU_DIM
    per = V7X_MXU_DIM // ML_QKV_BLOCK
    eye = jnp.eye(per, dtype=F32)

    def block_diag(w):
        wt = w.reshape(nt, per, ML_QKV_BLOCK, ML_QKV_BLOCK)
        return jnp.einsum("tncd,nm->tncmd", wt, eye).reshape(nt, V7X_MXU_DIM, V7X_MXU_DIM).astype(BF16)

    wg = jnp.concatenate([w_i, w_f], axis=1).T.astype(BF16)
    return dict(w_in=w_in.astype(BF16), conv_w=conv_w, conv_b=conv_b.reshape(1, ML_INNER),
                wq=block_diag(wq), wk=block_diag(wk), wv=block_diag(wv),
                wgq=wg[:, :ML_INNER], wgk=wg[:, ML_INNER:2 * ML_INNER], wgv=wg[:, 2 * ML_INNER:],
                gb=jnp.concatenate([b_i, b_f]).reshape(2 * ML_HEADS, 1),
                gn=gn_g.reshape(1, ML_INNER), skip=skip.reshape(1, ML_INNER), w_out=w_out.astype(BF16))


def _ml_layer(x_all, w, dims, c_all, n_all, m_all, state_conv, layer, n_layers, c_out_prev, ln_g, ln_b):
    n_p, seq_len, n_s, sample_len = dims
    t_p = n_p * seq_len
    rb = ROW_BLOCK
    xmz = _mm(x_all, w["w_in"], 2048)
    pad = jnp.pad(state_conv, ((0, 0), (0, sample_len - (ML_CONV - 1)), (0, 0))).reshape(n_s * sample_len, ML_INNER)
    conv_rows = jnp.concatenate([jnp.zeros((rb, ML_INNER), F32), pad], axis=0)
    q, k, v, xc, g3 = _ml_pre(xmz, conv_rows, w, t_p, seq_len, sample_len)
    npb = t_p // rb
    g_p = g3[:npb]
    if ML_CHUNK != rb:
        g_p = g_p.reshape(npb, 2 * ML_HEADS, rb // ML_CHUNK, ML_CHUNK).transpose(0, 2, 1, 3).reshape(-1, 2 * ML_HEADS, ML_CHUNK)
    g_s = g3[npb:].reshape(-1, 2 * ML_HEADS, rb // sample_len, sample_len).transpose(0, 2, 1, 3)
    g_s = g_s.reshape(n_s, 2 * ML_HEADS, sample_len)
    gated_p, pc, pn, pm = _ml_cell(q, k, v, xc, xmz, g_p, w["gn"], w["skip"], n_p, seq_len, ML_CHUNK)
    gated_s, c_out, sn, sm = _ml_cell_sample(q, k, v, xc, xmz, g_s, w["gn"], w["skip"], n_s, sample_len, t_p,
                                             c_all, n_all, m_all, layer, n_layers, c_out_prev)
    h1 = _mm_ln(gated_p, gated_s, w["w_out"], x_all, ln_g, ln_b)
    tail = ML_CONV - 1
    p_conv = jnp.stack([xmz[(b + 1) * seq_len - tail:(b + 1) * seq_len, :ML_INNER] for b in range(n_p)])
    s_conv = xmz[t_p:, :ML_INNER].reshape(n_s, sample_len, ML_INNER)[:, sample_len - tail:]
    return h1, (pc, pn, pm, p_conv), (sn, sm.reshape(n_s, ML_HEADS), s_conv), c_out


def _gla_in_body(x_ref, w_ref, wa_ref, wa2_ref, ba_ref, o_ref, la_ref):
    x = x_ref[...].astype(BF16)
    o_ref[...] = jnp.dot(x, w_ref[...], preferred_element_type=F32)
    a = jnp.dot(x, wa_ref[...], preferred_element_type=F32)
    la_ref[...] = _log_sigmoid(_dot(a, wa2_ref[...]) + ba_ref[...]) / GLA_TAU


def _gla_in(x, w):
    m, k = x.shape
    tm = ROW_BLOCK
    n1 = 2 * GLA_KT + 2 * GLA_VT
    full = lambda shape: pl.BlockSpec(shape, lambda i: (0, 0))
    return pl.pallas_call(
        _gla_in_body,
        grid=(m // tm,),
        in_specs=[pl.BlockSpec((tm, k), lambda i: (i, 0)), full((k, n1)), full((k, V7X_LANES)),
                  full((V7X_LANES, GLA_KT)), full((1, GLA_KT))],
        out_specs=[pl.BlockSpec((tm, n1), lambda i: (i, 0)), pl.BlockSpec((tm, GLA_KT), lambda i: (i, 0))],
        out_shape=[jax.ShapeDtypeStruct((m, n1), F32), jax.ShapeDtypeStruct((m, GLA_KT), F32)],
        compiler_params=_params(("arbitrary",), 48),
        name="gla_in",
    )(x, w["w_qkvr"], w["w_a"], w["w_a2"], w["b_a"])


def _gla_levels(chunk):
    t = jnp.arange(chunk)[:, None]
    s = jnp.arange(chunk)[None, :]
    mats = [(s <= t)]
    size = chunk
    while size >= 2:
        ref = t - t % size + size // 2 - 1
        mats.append(s <= ref)
        size //= 2
    levels = jnp.concatenate(mats, axis=0).astype(BF16)
    return jnp.pad(levels, ((0, 0), (0, max(V7X_LANES - chunk, 0))))


def _gla_chunk(q, k, v, la, lv, st):
    L = q.shape[0]
    kpad = lv.shape[1]
    parts = jnp.concatenate(_split3(la), axis=1)
    if kpad > L:
        parts = jnp.concatenate([parts, jnp.zeros((kpad - L, parts.shape[1]), BF16)], axis=0)
    cum = jnp.dot(lv, parts, preferred_element_type=F32)
    cum = cum[:, :GLA_DK] + cum[:, GLA_DK:2 * GLA_DK] + cum[:, 2 * GLA_DK:]
    b = cum[0:L]
    ri = lax.broadcasted_iota(jnp.int32, (L, L), 0)
    ci = lax.broadcasted_iota(jnp.int32, (L, L), 1)
    tpos = lax.broadcasted_iota(jnp.int32, (L, 1), 0)
    att = jnp.where(ri == ci, jnp.sum(q * k, axis=1, keepdims=True), 0.0)
    size = L
    lvl = 1
    while size >= 2:
        bref = cum[lvl * L:(lvl + 1) * L]
        upper = (tpos % size) >= (size // 2)
        qs = q * jnp.exp(jnp.where(upper, b - bref, -jnp.inf))
        ks = k * jnp.exp(jnp.where(upper, -jnp.inf, bref - b))
        att = att + jnp.where((ri // size) == (ci // size), _dot_nt(qs, ks), 0.0)
        size //= 2
        lvl += 1
    o = _dot(att, v) + _dot(q * jnp.exp(b), st)
    b_end = b[L - 1:L, :]
    e_end = jnp.exp(b_end)
    rk = lax.broadcasted_iota(jnp.int32, (GLA_DK, GLA_DK), 0)
    ck = lax.broadcasted_iota(jnp.int32, (GLA_DK, GLA_DK), 1)
    e_col = jnp.sum(jnp.where(rk == ck, e_end, 0.0), axis=1, keepdims=True)
    st_new = e_col * st + _dot_tn(k * jnp.exp(b_end - b), v)
    return o, st_new


def _gla_heads(x_ref, la_ref, lv_ref, gn_ref, o_ref, read_state, write_state):
    lv = lv_ref[...]
    for h in range(GLA_HEADS):
        qs = slice(h * GLA_DK, (h + 1) * GLA_DK)
        ks = slice(GLA_KT + h * GLA_DK, GLA_KT + (h + 1) * GLA_DK)
        vs = slice(2 * GLA_KT + h * GLA_DV, 2 * GLA_KT + (h + 1) * GLA_DV)
        rs = slice(2 * GLA_KT + GLA_VT + h * GLA_DV, 2 * GLA_KT + GLA_VT + (h + 1) * GLA_DV)
        os_ = slice(h * GLA_DV, (h + 1) * GLA_DV)
        o, st_new = _gla_chunk(x_ref[:, qs] * (GLA_DK ** -0.5), x_ref[:, ks], x_ref[:, vs], la_ref[:, qs], lv,
                               read_state(h))
        o_ref[:, os_] = _silu(x_ref[:, rs]) * _head_norm(o, gn_ref[:, os_])
        write_state(h, st_new)


def _gla_cell_body(x_ref, la_ref, lv_ref, gn_ref, o_ref, so_ref, ss):
    c = pl.program_id(1)
    nc = pl.num_programs(1)

    @pl.when(c == 0)
    def _():
        ss[...] = jnp.zeros(ss.shape, F32)

    def write(h, st):
        ss[h] = st

    _gla_heads(x_ref, la_ref, lv_ref, gn_ref, o_ref, lambda h: ss[h], write)

    @pl.when(c == nc - 1)
    def _():
        so_ref[0] = ss[...]


def _gla_cell(qkvr, la, gn, n_seq, seq_len, chunk):
    n1 = qkvr.shape[1]
    nc = seq_len // chunk
    levels = _gla_levels(chunk)
    return pl.pallas_call(
        _gla_cell_body,
        grid=(n_seq, nc),
        in_specs=[pl.BlockSpec((chunk, n1), lambda b, c: (b * nc + c, 0)),
                  pl.BlockSpec((chunk, GLA_KT), lambda b, c: (b * nc + c, 0)),
                  pl.BlockSpec(levels.shape, lambda b, c: (0, 0)),
                  pl.BlockSpec((1, GLA_VT), lambda b, c: (0, 0))],
        out_specs=[pl.BlockSpec((chunk, GLA_VT), lambda b, c: (b * nc + c, 0)),
                   pl.BlockSpec((1, GLA_HEADS, GLA_DK, GLA_DV), lambda b, c: (b, 0, 0, 0))],
        out_shape=[jax.ShapeDtypeStruct((n_seq * seq_len, GLA_VT), F32),
                   jax.ShapeDtypeStruct((n_seq, GLA_HEADS, GLA_DK, GLA_DV), F32)],
        scratch_shapes=[pltpu.VMEM((GLA_HEADS, GLA_DK, GLA_DV), F32)],
        compiler_params=_params(("arbitrary", "arbitrary"), 32),
        name="gla_cell",
    )(qkvr, la, levels, gn)


def _gla_cell_sample_body(x_ref, la_ref, lv_ref, gn_ref, s0_ref, o_ref, so_ref):
    def write(h, st):
        so_ref[0, h] = st

    _gla_heads(x_ref, la_ref, lv_ref, gn_ref, o_ref, lambda h: s0_ref[0, h], write)


def _gla_cell_sample(qkvr, la, gn, n_seq, seq_len, row0, state):
    n1 = qkvr.shape[1]
    blk0 = row0 // seq_len
    levels = _gla_levels(seq_len)
    st_spec = pl.BlockSpec((1, GLA_HEADS, GLA_DK, GLA_DV), lambda b: (b, 0, 0, 0))
    return pl.pallas_call(
        _gla_cell_sample_body,
        grid=(n_seq,),
        in_specs=[pl.BlockSpec((seq_len, n1), lambda b: (blk0 + b, 0)),
                  pl.BlockSpec((seq_len, GLA_KT), lambda b: (blk0 + b, 0)),
                  pl.BlockSpec(levels.shape, lambda b: (0, 0)),
                  pl.BlockSpec((1, GLA_VT), lambda b: (0, 0)),
                  st_spec],
        out_specs=[pl.BlockSpec((seq_len, GLA_VT), lambda b: (b, 0)), st_spec],
        out_shape=[jax.ShapeDtypeStruct((n_seq * seq_len, GLA_VT), F32),
                   jax.ShapeDtypeStruct((n_seq, GLA_HEADS, GLA_DK, GLA_DV), F32)],
        compiler_params=_params(("arbitrary",), 32),
        name="gla_cell_sample",
    )(qkvr, la, levels, gn, state)


def _gla_weights(w_in, w_a2, b_a, gn_g, w_out):
    n1 = 2 * GLA_KT + 2 * GLA_VT
    w_a = jnp.pad(w_in[:, n1:], ((0, 0), (0, V7X_LANES - GLA_RANK)))
    w_a2p = jnp.pad(w_a2, ((0, V7X_LANES - GLA_RANK), (0, 0)))
    return dict(w_qkvr=w_in[:, :n1].astype(BF16), w_a=w_a.astype(BF16), w_a2=w_a2p.astype(BF16),
                b_a=b_a.reshape(1, GLA_KT), gn=gn_g.reshape(1, GLA_VT), w_out=w_out.astype(BF16))


def _gla_layer(x_all, w, dims, state_s, ln_g, ln_b):
    n_p, seq_len, n_s, sample_len = dims
    t_p = n_p * seq_len
    qkvr, la = _gla_in(x_all, w)
    gated_p, p_s = _gla_cell(qkvr, la, w["gn"], n_p, seq_len, min(GLA_CHUNK, seq_len))
    gated_s, s_s = _gla_cell_sample(qkvr, la, w["gn"], n_s, sample_len, t_p, state_s)
    h1 = _mm_ln(gated_p, gated_s, w["w_out"], x_all, ln_g, ln_b)
    return h1, p_s, s_s


def _s5_body(x_ref, sre_ref, sim_ref, win_ref, bt_ref, cre_ref, cim_ref, lre_ref, lim_ref, d_ref, wglu_ref,
             g_ref, b_ref, o_ref, ore_ref, oim_ref, st_re, st_im, car_re, car_im, *, nb, lc):
    c = pl.program_id(1)
    nc = pl.num_programs(1)
    rows = nb * lc
    ntile = D_MODEL // S5_TILE_CH

    @pl.when(c == 0)
    def _():
        car_re[...] = sre_ref[...]
        car_im[...] = sim_ref[...]

    x = x_ref[...].reshape(rows, D_MODEL)
    u = _dot(x, win_ref[...])
    lpt = S5_TILE_ST // V7X_LANES
    for t in range(ntile):
        bu = _dot(u[:, t * S5_TILE_CH:(t + 1) * S5_TILE_CH], bt_ref[t])
        for j in range(lpt):
            st_re[t * lpt + j] = bu[:, j * V7X_LANES:(j + 1) * V7X_LANES]
            st_im[t * lpt + j] = bu[:, S5_TILE_ST + j * V7X_LANES:S5_TILE_ST + (j + 1) * V7X_LANES]

    for gidx in range(nb // V7X_SUBLANES):
        grp = slice(gidx * V7X_SUBLANES, (gidx + 1) * V7X_SUBLANES)
        for t in range(ntile):
            slabs = list(range(t * lpt, (t + 1) * lpt))
            lanes = [slice(j * V7X_LANES, (j + 1) * V7X_LANES) for j in slabs]
            lre = [lre_ref[:, ln] for ln in lanes]
            lim = [lim_ref[:, ln] for ln in lanes]

            def step(tok, carry, slabs=slabs, lre=lre, lim=lim, gidx=gidx):
                sel = pl.ds(pl.multiple_of(tok * nb + gidx * V7X_SUBLANES, V7X_SUBLANES), V7X_SUBLANES)
                out = []
                for n, j in enumerate(slabs):
                    pr, pi = carry[2 * n], carry[2 * n + 1]
                    nr = lre[n] * pr - lim[n] * pi + st_re[j, sel, :]
                    ni = lre[n] * pi + lim[n] * pr + st_im[j, sel, :]
                    st_re[j, sel, :] = nr
                    st_im[j, sel, :] = ni
                    out += [nr, ni]
                return tuple(out)

            init = []
            for ln in lanes:
                init += [car_re[grp, ln], car_im[grp, ln]]
            fin = lax.fori_loop(0, lc, step, tuple(init))
            for n, ln in enumerate(lanes):
                car_re[grp, ln] = fin[2 * n]
                car_im[grp, ln] = fin[2 * n + 1]

    ys = []
    for t in range(ntile):
        sre = jnp.concatenate([st_re[t * lpt + j] for j in range(lpt)], axis=1)
        sim = jnp.concatenate([st_im[t * lpt + j] for j in range(lpt)], axis=1)
        ys.append(_dot(sre, cre_ref[t]) - _dot(sim, cim_ref[t]))
    y = jnp.concatenate(ys, axis=1) + d_ref[...] * u
    y = jax.nn.gelu(y)
    vg = _dot(y, wglu_ref[...])
    mix = vg[:, :D_MODEL] * _sigmoid(vg[:, D_MODEL:])
    o_ref[...] = _layer_norm(DEEPNORM_ALPHA * x + mix, g_ref[...], b_ref[...]).reshape(lc, nb, D_MODEL)

    @pl.when(c == nc - 1)
    def _():
        ore_ref[...] = car_re[...]
        oim_ref[...] = car_im[...]


def _s5_call(x3, s_re, s_im, w, ln_g, ln_b, nb, lc):
    seq_len, n_seq, _ = x3.shape
    ntile = D_MODEL // S5_TILE_CH
    rows = nb * lc
    full2 = lambda shape: pl.BlockSpec(shape, lambda i, c: (0, 0))
    full3 = lambda shape: pl.BlockSpec(shape, lambda i, c: (0, 0, 0))
    st_spec = pl.BlockSpec((nb, S5_STATE), lambda i, c: (i, 0))
    x_spec = pl.BlockSpec((lc, nb, D_MODEL), lambda i, c: (c, i, 0))
    return pl.pallas_call(
        functools.partial(_s5_body, nb=nb, lc=lc),
        grid=(n_seq // nb, seq_len // lc),
        in_specs=[x_spec, st_spec, st_spec,
                  full2((D_MODEL, D_MODEL)), full3((ntile, S5_TILE_CH, 2 * S5_TILE_ST)),
                  full3((ntile, S5_TILE_ST, S5_TILE_CH)), full3((ntile, S5_TILE_ST, S5_TILE_CH)),
                  full2((V7X_SUBLANES, S5_STATE)), full2((V7X_SUBLANES, S5_STATE)), full2((1, D_MODEL)),
                  full2((D_MODEL, 2 * D_MODEL)), full2((1, D_MODEL)), full2((1, D_MODEL))],
        out_specs=[x_spec, st_spec, st_spec],
        out_shape=[jax.ShapeDtypeStruct(x3.shape, F32), jax.ShapeDtypeStruct((n_seq, S5_STATE), F32),
                   jax.ShapeDtypeStruct((n_seq, S5_STATE), F32)],
        scratch_shapes=[pltpu.VMEM((S5_STATE // V7X_LANES, rows, V7X_LANES), F32),
                        pltpu.VMEM((S5_STATE // V7X_LANES, rows, V7X_LANES), F32),
                        pltpu.VMEM((nb, S5_STATE), F32), pltpu.VMEM((nb, S5_STATE), F32)],
        compiler_params=_params(("arbitrary", "arbitrary"), 56),
        name="s5",
    )(x3, s_re, s_im, w["w_in"], w["b_tiles"], w["c_re"], w["c_im"], w["lam_re"], w["lam_im"], w["d"], w["w_glu"],
      ln_g.reshape(1, D_MODEL), ln_b.reshape(1, D_MODEL))


def _s5_weights(w_in, a_re, a_im, log_dt, b_re, b_im, c_re, c_im, d_skip, w_glu):
    lam = lax.complex(a_re.astype(F32), a_im.astype(F32))
    dt = jnp.exp(log_dt.astype(F32))[:, None]
    lam_bar = jnp.exp(lam * dt)
    b_bar = ((lam_bar - 1.0) / lam)[..., None] * lax.complex(b_re.astype(F32), b_im.astype(F32))
    ntile = D_MODEL // S5_TILE_CH
    gpt = S5_TILE_CH // S5_GC
    eye = jnp.eye(gpt, dtype=F32)

    def b_tiles(bb):
        return jnp.einsum("igpc,gh->igchp", bb.reshape(ntile, gpt, S5_P, S5_GC), eye).reshape(ntile, S5_TILE_CH, S5_TILE_ST)

    def c_tiles(cc):
        return jnp.einsum("igcp,gh->igphc", cc.reshape(ntile, gpt, S5_GC, S5_P), eye).reshape(ntile, S5_TILE_ST, S5_TILE_CH)

    bt = jnp.concatenate([b_tiles(b_bar.real), b_tiles(b_bar.imag)], axis=2).astype(BF16)
    bcast = lambda a: jnp.broadcast_to(a.reshape(1, S5_STATE), (V7X_SUBLANES, S5_STATE))
    return dict(w_in=w_in.astype(BF16), b_tiles=bt, c_re=c_tiles(c_re.astype(F32)).astype(BF16),
                c_im=c_tiles(c_im.astype(F32)).astype(BF16), lam_re=bcast(lam_bar.real), lam_im=bcast(lam_bar.imag),
                d=d_skip.reshape(1, D_MODEL).astype(F32), w_glu=w_glu.astype(BF16))


def _s5_layer(x_all, w, dims, state_re, state_im, ln_g, ln_b):
    n_p, seq_len, n_s, sample_len = dims
    t_p = n_p * seq_len
    xp = x_all[:t_p].reshape(n_p, seq_len, D_MODEL).transpose(1, 0, 2)
    xs = x_all[t_p:].reshape(n_s, sample_len, D_MODEL).transpose(1, 0, 2)
    zero = jnp.zeros((n_p, S5_STATE), F32)
    hp, p_re, p_im = _s5_call(xp, zero, zero, w, ln_g, ln_b, n_p, min(S5_CHUNK, seq_len))
    nb_s = min(n_s, ROW_BLOCK // sample_len)
    hs, s_re, s_im = _s5_call(xs, state_re.reshape(n_s, S5_STATE), state_im.reshape(n_s, S5_STATE), w, ln_g, ln_b,
                              nb_s, sample_len)
    h1 = jnp.concatenate([hp.transpose(1, 0, 2).reshape(t_p, D_MODEL),
                          hs.transpose(1, 0, 2).reshape(n_s * sample_len, D_MODEL)], axis=0)
    shp = lambda a, n: a.reshape(n, S5_GROUPS, S5_P)
    return h1, (shp(p_re, n_p), shp(p_im, n_p)), (shp(s_re, n_s), shp(s_im, n_s))


def _route_body(x_ref, wh_ref, wl_ref, br_ref, g_ref):
    x = x_ref[...]
    xh = x.astype(BF16)
    xl = (x - xh.astype(F32)).astype(BF16)
    wh = wh_ref[...]
    logits = _dot_nt(wh, xh) + _dot_nt(wh, xl) + _dot_nt(wl_ref[...], xh)
    s = _sigmoid(logits)
    work = s + br_ref[...]
    row = lax.broadcasted_iota(jnp.int32, s.shape, 0).astype(F32)
    chosen = jnp.zeros(s.shape, jnp.bool_)
    for _ in range(TOP_K):
        mx = jnp.max(work, axis=0, keepdims=True)
        idx = jnp.min(jnp.where(work == mx, row, float(N_EXPERTS)), axis=0, keepdims=True)
        hit = row == idx
        chosen = jnp.logical_or(chosen, hit)
        work = jnp.where(hit, -jnp.inf, work)
    sel = jnp.where(chosen, s, 0.0)
    g_ref[...] = sel / jnp.sum(sel, axis=0, keepdims=True) * ROUTE_SCALE


def _route(x, w):
    m, k = x.shape
    tm = ROW_BLOCK
    full = lambda shape: pl.BlockSpec(shape, lambda i: (0, 0))
    gates_t = pl.pallas_call(
        _route_body,
        grid=(m // tm,),
        in_specs=[pl.BlockSpec((tm, k), lambda i: (i, 0)), full((N_EXPERTS, k)), full((N_EXPERTS, k)),
                  full((N_EXPERTS, 1))],
        out_specs=pl.BlockSpec((N_EXPERTS, tm), lambda i: (0, i)),
        out_shape=jax.ShapeDtypeStruct((N_EXPERTS, m), F32),
        compiler_params=_params(("arbitrary",), 32),
        name="route",
    )(x, w["wr_hi"], w["wr_lo"], w["b_router"])
    return gates_t.T


def _moe_body(x_ref, gd_ref, wg_ref, wu_ref, wd_ref, sg_ref, su_ref, sd_ref, g_ref, b_ref, o_ref, acc, xb_ref):
    e = pl.program_id(1)
    ne = pl.num_programs(1)

    @pl.when(e == 0)
    def _():
        xb0 = x_ref[...].astype(BF16)
        xb_ref[...] = xb0
        hs = _silu(jnp.dot(xb0, sg_ref[...], preferred_element_type=F32)) * jnp.dot(xb0, su_ref[...], preferred_element_type=F32)
        acc[...] = _dot(hs, sd_ref[...])

    xb = xb_ref[...]
    gd = gd_ref[...]
    lane = lax.broadcasted_iota(jnp.int32, gd.shape, 1)
    total = None
    for j in range(MOE_EXPERTS_PER_STEP):
        h = _silu(jnp.dot(xb, wg_ref[j], preferred_element_type=F32)) * jnp.dot(xb, wu_ref[j], preferred_element_type=F32)
        gate = jnp.sum(jnp.where(lane == e * MOE_EXPERTS_PER_STEP + j, gd, 0.0), axis=1, keepdims=True)
        y = _dot(h, wd_ref[j]) * gate
        total = y if total is None else total + y
    acc[...] += total

    @pl.when(e == ne - 1)
    def _():
        o_ref[...] = _layer_norm(DEEPNORM_ALPHA * x_ref[...] + acc[...], g_ref[...], b_ref[...])


def _moe_ffn(x, gates, w, experts, layer, ln_g, ln_b):
    m, k = x.shape
    tm = MOE_ROWS
    steps = N_EXPERTS // MOE_EXPERTS_PER_STEP
    full = lambda shape: pl.BlockSpec(shape, lambda i, e: (0, 0))
    expert_map = lambda i, e: (layer * steps + e, 0, 0)
    return pl.pallas_call(
        _moe_body,
        grid=(m // tm, steps),
        in_specs=[pl.BlockSpec((tm, k), lambda i, e: (i, 0)), pl.BlockSpec((tm, N_EXPERTS), lambda i, e: (i, 0)),
                  pl.BlockSpec((MOE_EXPERTS_PER_STEP, k, EXPERT_FF), expert_map),
                  pl.BlockSpec((MOE_EXPERTS_PER_STEP, k, EXPERT_FF), expert_map),
                  pl.BlockSpec((MOE_EXPERTS_PER_STEP, EXPERT_FF, k), expert_map),
                  full((k, w["ws_gate"].shape[1])), full((k, w["ws_up"].shape[1])), full((w["ws_down"].shape[0], k)),
                  full((1, k)), full((1, k))],
        out_specs=pl.BlockSpec((tm, k), lambda i, e: (i, 0)),
        out_shape=jax.ShapeDtypeStruct((m, k), F32),
        scratch_shapes=[pltpu.VMEM((tm, k), F32), pltpu.VMEM((tm, k), BF16)],
        compiler_params=_params(("arbitrary", "arbitrary"), 48),
        name="moe_ffn",
    )(x, gates, experts[0], experts[1], experts[2], w["ws_gate"], w["ws_up"], w["ws_down"],
      ln_g.reshape(1, k), ln_b.reshape(1, k))


def _moe_weights(w_router, b_router, ws_gate, ws_up, ws_down):
    wr_t = w_router.T
    wr_hi = wr_t.astype(BF16)
    wr_lo = (wr_t - wr_hi.astype(F32)).astype(BF16)
    return dict(wr_hi=wr_hi, wr_lo=wr_lo, b_router=b_router.reshape(N_EXPERTS, 1).astype(F32),
                ws_gate=ws_gate.astype(BF16), ws_up=ws_up.astype(BF16), ws_down=ws_down.astype(BF16))


def _moe_layer(x, w, experts, layer, ln_g, ln_b):
    return _moe_ffn(x, _route(x, w), w, experts, layer, ln_g, ln_b)


def kernel(x_prompt, x_sample, state_mlstm_C, state_mlstm_n, state_mlstm_m, state_mlstm_conv, state_gla_S, state_s5_re, state_s5_im, ln1_g, ln1_b, ln2_g, ln2_b, ml_w_in, ml_conv_w, ml_conv_b, ml_wq, ml_wk, ml_wv, ml_w_i, ml_b_i, ml_w_f, ml_b_f, ml_gn_g, ml_skip, ml_w_out, gla_w_in, gla_w_a2, gla_b_a, gla_gn_g, gla_w_out, s5_w_in, s5_a_re, s5_a_im, s5_log_dt, s5_b_re, s5_b_im, s5_c_re, s5_c_im, s5_d, s5_w_glu, moe_w_router, moe_b_router, moe_w_gate, moe_w_up, moe_w_down, moe_ws_gate, moe_ws_up, moe_ws_down):
    n_p, seq_len, _ = x_prompt.shape
    n_s, sample_len, _ = x_sample.shape
    dims = (n_p, seq_len, n_s, sample_len)
    t_p = n_p * seq_len
    t_s = n_s * sample_len
    assert seq_len % ROW_BLOCK == 0 and t_s % ROW_BLOCK == 0 and ROW_BLOCK % sample_len == 0
    assert sample_len == V7X_SUBLANES and n_p % V7X_SUBLANES == 0 and (t_p + t_s) % MOE_ROWS == 0
    x = jnp.concatenate([x_prompt.reshape(t_p, D_MODEL), x_sample.reshape(t_s, D_MODEL)], axis=0)
    n_ml = len(range(0, DEPTH, N_MIXERS))
    c_all = state_mlstm_C.reshape(-1, ML_HEADS, ML_DH, ML_DH)
    n_all = state_mlstm_n.reshape(-1, ML_HEADS, ML_DH)
    m_all = state_mlstm_m.reshape(-1, ML_HEADS, 1)
    c_out = None
    flat_experts = lambda w: w.astype(BF16).reshape((-1,) + w.shape[2:])
    experts = (flat_experts(moe_w_gate), flat_experts(moe_w_up), flat_experts(moe_w_down))
    p_ml, s_ml, p_gla, s_gla, p_s5, s_s5 = [], [], [], [], [], []
    for i in range(DEPTH):
        j = i // N_MIXERS
        if i % N_MIXERS == 0:
            w = _ml_weights(ml_w_in[j], ml_conv_w[j], ml_conv_b[j], ml_wq[j], ml_wk[j], ml_wv[j], ml_w_i[j], ml_b_i[j],
                            ml_w_f[j], ml_b_f[j], ml_gn_g[j], ml_skip[j], ml_w_out[j])
            x, ps, ss, c_out = _ml_layer(x, w, dims, c_all, n_all, m_all, state_mlstm_conv[j], j, n_ml, c_out,
                                         ln1_g[i], ln1_b[i])
            p_ml.append(ps)
            s_ml.append(ss)
        elif i % N_MIXERS == 1:
            w = _gla_weights(gla_w_in[j], gla_w_a2[j], gla_b_a[j], gla_gn_g[j], gla_w_out[j])
            x, ps, ss = _gla_layer(x, w, dims, state_gla_S[j], ln1_g[i], ln1_b[i])
            p_gla.append(ps)
            s_gla.append(ss)
        else:
            w = _s5_weights(s5_w_in[j], s5_a_re[j], s5_a_im[j], s5_log_dt[j], s5_b_re[j], s5_b_im[j], s5_c_re[j],
                            s5_c_im[j], s5_d[j], s5_w_glu[j])
            x, ps, ss = _s5_layer(x, w, dims, state_s5_re[j], state_s5_im[j], ln1_g[i], ln1_b[i])
            p_s5.append(ps)
            s_s5.append(ss)
        wm = _moe_weights(moe_w_router[i], moe_b_router[i], moe_ws_gate[i], moe_ws_up[i], moe_ws_down[i])
        x = _moe_layer(x, wm, experts, i, ln2_g[i], ln2_b[i])
    stack = lambda items, idx: jnp.stack([it[idx] for it in items])
    return (x[:t_p].reshape(n_p, seq_len, D_MODEL), x[t_p:].reshape(n_s, sample_len, D_MODEL),
            stack(p_ml, 0), stack(p_ml, 1), stack(p_ml, 2), stack(p_ml, 3), jnp.stack(p_gla),
            stack(p_s5, 0), stack(p_s5, 1),
            c_out.reshape(n_ml, n_s, ML_HEADS, ML_DH, ML_DH), stack(s_ml, 0), stack(s_ml, 1), stack(s_ml, 2),
            jnp.stack(s_gla), stack(s_s5, 0), stack(s_s5, 1))
```

```python
import functools

import jax
import jax.numpy as jnp
from jax import lax
from jax.experimental import pallas as pl
from jax.experimental.pallas import tpu as pltpu

F32 = jnp.float32
BF16 = jnp.bfloat16

D_MODEL = 1024
DEPTH = 4
N_MIXERS = 3
ML_INNER = 2 * D_MODEL
ML_HEADS = 4
ML_DH = ML_INNER // ML_HEADS
ML_QKV_BLOCK = 4
ML_CONV = 4
GLA_HEADS = 4
GLA_KT = D_MODEL // 2
GLA_VT = D_MODEL
GLA_DK = GLA_KT // GLA_HEADS
GLA_DV = GLA_VT // GLA_HEADS
GLA_RANK = 16
GLA_TAU = 16.0
S5_GC = 16
S5_GROUPS = D_MODEL // S5_GC
S5_P = 64
S5_STATE = S5_GROUPS * S5_P
N_EXPERTS = 64
TOP_K = 8
EXPERT_FF = 256
ROUTE_SCALE = 2.5
DEEPNORM_ALPHA = (2.0 * DEPTH) ** 0.25
NORM_EPS = 1e-5

V7X_VMEM_BYTES = 64 * 1024 * 1024
V7X_LANES = 128
V7X_SUBLANES = 8
V7X_MXU_DIM = 256

ROW_BLOCK = 256
MM_ROWS = 512
ML_CHUNK = 256
GLA_CHUNK = 128
S5_CHUNK = 32
S5_TILE_CH = 128
S5_TILE_ST = S5_TILE_CH // S5_GC * S5_P
MOE_ROWS = 1024
MOE_EXPERTS_PER_STEP = 4


def _params(semantics, vmem_mb):
    assert vmem_mb * 1024 * 1024 < V7X_VMEM_BYTES
    return pltpu.CompilerParams(dimension_semantics=semantics, vmem_limit_bytes=vmem_mb * 1024 * 1024)


def _dot(a, b):
    return jnp.dot(a.astype(BF16), b.astype(BF16), preferred_element_type=F32)


def _dot_nt(a, b):
    return lax.dot_general(a.astype(BF16), b.astype(BF16), (((1,), (1,)), ((), ())), preferred_element_type=F32)


def _dot_tn(a, b):
    return lax.dot_general(a.astype(BF16), b.astype(BF16), (((0,), (0,)), ((), ())), preferred_element_type=F32)


def _sigmoid(x):
    return 1.0 / (1.0 + jnp.exp(-x))


def _silu(x):
    return x * _sigmoid(x)


def _log_sigmoid(x):
    return jnp.minimum(x, 0.0) - jnp.log1p(jnp.exp(-jnp.abs(x)))


def _layer_norm(y, g, b):
    mu = jnp.mean(y, axis=-1, keepdims=True)
    d = y - mu
    var = jnp.mean(d * d, axis=-1, keepdims=True)
    return d * lax.rsqrt(var + NORM_EPS) * g + b


def _head_norm(h, g):
    mu = jnp.mean(h, axis=-1, keepdims=True)
    d = h - mu
    var = jnp.mean(d * d, axis=-1, keepdims=True)
    return d * lax.rsqrt(var + NORM_EPS) * g


def _split3(x):
    hi = x.astype(BF16)
    r1 = x - hi.astype(F32)
    mid = r1.astype(BF16)
    lo = (r1 - mid.astype(F32)).astype(BF16)
    return hi, mid, lo


def _mm_body(x_ref, w_ref, o_ref):
    o_ref[...] = _dot(x_ref[...], w_ref[...])


def _mm(x, w, tn):
    m, k = x.shape
    n = w.shape[1]
    tm = MM_ROWS
    return pl.pallas_call(
        _mm_body,
        grid=(n // tn, m // tm),
        in_specs=[pl.BlockSpec((tm, k), lambda j, i: (i, 0)), pl.BlockSpec((k, tn), lambda j, i: (0, j))],
        out_specs=pl.BlockSpec((tm, tn), lambda j, i: (i, j)),
        out_shape=jax.ShapeDtypeStruct((m, n), F32),
        compiler_params=_params(("arbitrary", "arbitrary"), 40),
        name="mm",
    )(x, w)


def _mm_ln_body(ap_ref, as_ref, w_ref, r_ref, g_ref, b_ref, o_ref, *, n_prompt_blocks):
    i = pl.program_id(0)

    def run(a_ref):
        y = _dot(a_ref[...], w_ref[...])
        o_ref[...] = _layer_norm(DEEPNORM_ALPHA * r_ref[...] + y, g_ref[...], b_ref[...])

    pl.when(i < n_prompt_blocks)(lambda: run(ap_ref))
    pl.when(i >= n_prompt_blocks)(lambda: run(as_ref))


def _mm_ln(a_p, a_s, w, resid, g, b):
    k = a_p.shape[1]
    m = a_p.shape[0] + a_s.shape[0]
    n = w.shape[1]
    tm = MM_ROWS
    npb = a_p.shape[0] // tm
    return pl.pallas_call(
        functools.partial(_mm_ln_body, n_prompt_blocks=npb),
        grid=(m // tm,),
        in_specs=[pl.BlockSpec((tm, k), lambda i: (jnp.minimum(i, npb - 1), 0)),
                  pl.BlockSpec((tm, k), lambda i: (jnp.maximum(i - npb, 0), 0)),
                  pl.BlockSpec((k, n), lambda i: (0, 0)),
                  pl.BlockSpec((tm, n), lambda i: (i, 0)), pl.BlockSpec((1, n), lambda i: (0, 0)),
                  pl.BlockSpec((1, n), lambda i: (0, 0))],
        out_specs=pl.BlockSpec((tm, n), lambda i: (i, 0)),
        out_shape=jax.ShapeDtypeStruct((m, n), F32),
        compiler_params=_params(("arbitrary",), 48),
        name="mm_ln",
    )(a_p, a_s, w, resid, g.reshape(1, n), b.reshape(1, n))


def _ml_pre_body(x_ref, prev_ref, p_ref, cw_ref, cb_ref, wq_ref, wk_ref, wv_ref, wgq_ref, wgk_ref, wgv_ref, gb_ref,
                 q_ref, k_ref, v_ref, xc_ref, g_ref, scx, scp, *, n_prompt_blocks, blocks_per_seq, sample_len):
    i = pl.program_id(0)
    rows = x_ref.shape[0]
    is_prompt = i < n_prompt_blocks
    no_prev = jnp.logical_or(i % blocks_per_seq == 0, jnp.logical_not(is_prompt))
    x = x_ref[...]
    scx[0:V7X_SUBLANES, :] = jnp.where(no_prev, 0.0, prev_ref[...])
    scx[V7X_SUBLANES:V7X_SUBLANES + rows, :] = x

    scp[0:rows, :] = p_ref[...]
    scp[rows:rows + V7X_SUBLANES, :] = jnp.zeros((V7X_SUBLANES, x.shape[1]), F32)
    r = lax.broadcasted_iota(jnp.int32, (rows, 1), 0)
    tpos = jnp.where(is_prompt, (i % blocks_per_seq) * rows + r, r % sample_len)
    acc = cb_ref[...] + x * cw_ref[ML_CONV - 1:ML_CONV, :]
    for s in range(1, ML_CONV):
        xs = scx[V7X_SUBLANES - s:V7X_SUBLANES - s + rows, :]
        ps = scp[ML_CONV - 1 - s:ML_CONV - 1 - s + rows, :]
        acc = acc + jnp.where(tpos >= s, xs, ps) * cw_ref[ML_CONV - 1 - s:ML_CONV - s, :]
    xc = _silu(acc)
    xc_ref[...] = xc
    g = jnp.zeros((2 * ML_HEADS, rows), F32)
    for t in range(ML_INNER // V7X_MXU_DIM):
        sl = slice(t * V7X_MXU_DIM, (t + 1) * V7X_MXU_DIM)
        q_t = _dot(xc[:, sl], wq_ref[t])
        k_t = _dot(xc[:, sl], wk_ref[t])
        v_t = _dot(x[:, sl], wv_ref[t])
        q_ref[:, sl] = q_t
        k_ref[:, sl] = k_t * (ML_DH ** -0.5)
        v_ref[:, sl] = v_t
        g = g + _dot_nt(wgq_ref[:, sl], q_t) + _dot_nt(wgk_ref[:, sl], k_t) + _dot_nt(wgv_ref[:, sl], v_t)
    g = g + gb_ref[...]
    gate_row = lax.broadcasted_iota(jnp.int32, g.shape, 0)
    g_ref[0] = jnp.where(gate_row < ML_HEADS, g, _log_sigmoid(g))


def _ml_pre(xmz, conv_rows, w, n_prompt_rows, seq_len, sample_len):
    t_all = xmz.shape[0]
    rb = ROW_BLOCK
    nblk = t_all // rb
    npb = n_prompt_rows // rb
    bps = seq_len // rb
    sub = rb // V7X_SUBLANES
    body = functools.partial(_ml_pre_body, n_prompt_blocks=npb, blocks_per_seq=bps, sample_len=sample_len)
    full2 = lambda shape: pl.BlockSpec(shape, lambda i: (0, 0))
    full3 = lambda shape: pl.BlockSpec(shape, lambda i: (0, 0, 0))
    nt = ML_INNER // V7X_MXU_DIM
    row_spec = pl.BlockSpec((rb, ML_INNER), lambda i: (i, 0))
    return pl.pallas_call(
        body,
        grid=(nblk,),
        in_specs=[row_spec,
                  pl.BlockSpec((V7X_SUBLANES, ML_INNER), lambda i: (jnp.maximum(i * sub - 1, 0), 0)),
                  pl.BlockSpec((rb, ML_INNER), lambda i: (jnp.maximum(i - npb + 1, 0), 0)),
                  full2((ML_CONV, ML_INNER)), full2((1, ML_INNER)),
                  full3((nt, V7X_MXU_DIM, V7X_MXU_DIM)), full3((nt, V7X_MXU_DIM, V7X_MXU_DIM)),
                  full3((nt, V7X_MXU_DIM, V7X_MXU_DIM)),
                  full2((2 * ML_HEADS, ML_INNER)), full2((2 * ML_HEADS, ML_INNER)), full2((2 * ML_HEADS, ML_INNER)),
                  full2((2 * ML_HEADS, 1))],
        out_specs=[row_spec, row_spec, row_spec, row_spec,
                   pl.BlockSpec((1, 2 * ML_HEADS, rb), lambda i: (i, 0, 0))],
        out_shape=[jax.ShapeDtypeStruct((t_all, ML_INNER), F32)] * 4
        + [jax.ShapeDtypeStruct((nblk, 2 * ML_HEADS, rb), F32)],
        scratch_shapes=[pltpu.VMEM((rb + V7X_SUBLANES, ML_INNER), F32), pltpu.VMEM((rb + V7X_SUBLANES, ML_INNER), F32)],
        compiler_params=_params(("arbitrary",), 48),
        name="ml_pre",
    )(xmz, xmz, conv_rows, w["conv_w"], w["conv_b"], w["wq"], w["wk"], w["wv"], w["wgq"], w["wgk"], w["wgv"], w["gb"])


def _ml_masks(L):
    ri = lax.broadcasted_iota(jnp.int32, (L, L), 0)
    ci = lax.broadcasted_iota(jnp.int32, (L, L), 1)
    return ri == ci, ci <= ri, ri <= ci


def _ml_chunk(q, k, v, ip, fl, c_prev, n_prev, m_prev, masks):
    L = q.shape[0]
    eye, tril, triu = masks
    f_col = jnp.sum(jnp.where(eye, fl, 0.0), axis=1, keepdims=True)
    b_col = jnp.sum(jnp.where(tril, fl, 0.0), axis=1, keepdims=True)
    b_row = jnp.sum(jnp.where(triu, f_col, 0.0), axis=0, keepdims=True)
    ib = ip - b_row
    d = jnp.where(tril, b_col + ib, -jnp.inf)
    m_inter = b_col + m_prev
    m_t = jnp.maximum(m_inter, jnp.max(d, axis=1, keepdims=True))
    a_inter = jnp.exp(m_inter - m_t)
    s = _dot_nt(q, k) * jnp.exp(d - m_t)
    num = _dot(s, v) + a_inter * _dot(q, c_prev)
    den = jnp.sum(s, axis=1, keepdims=True) + a_inter * jnp.sum(q * n_prev, axis=1, keepdims=True)
    hc = num * (1.0 / jnp.maximum(jnp.abs(den), jnp.exp(-m_t)))
    m_new = m_t[L - 1:L, :]
    b_last = b_row[:, L - 1:L]
    w_row = jnp.exp(b_last + ib - m_new)
    w_col = jnp.sum(jnp.where(eye, w_row, 0.0), axis=1, keepdims=True)
    a_end = jnp.exp(b_last + m_prev - m_new)
    kw = k * w_col
    c_new = a_end * c_prev + _dot_tn(kw, v)
    n_new = a_end * n_prev + jnp.sum(kw, axis=0, keepdims=True)
    return hc, c_new, n_new, m_new


def _ml_gate_out(hc, gn, sk, xc, z):
    return (_head_norm(hc, gn) + sk * xc) * _silu(z)


def _ml_cell_body(q_ref, k_ref, v_ref, xc_ref, z_ref, g_ref, gn_ref, sk_ref, o_ref, co_ref, no_ref, mo_ref, cs, ns, ms):
    c = pl.program_id(1)
    nc = pl.num_programs(1)

    @pl.when(c == 0)
    def _():
        cs[...] = jnp.zeros(cs.shape, F32)
        ns[...] = jnp.zeros(ns.shape, F32)
        ms[...] = jnp.zeros(ms.shape, F32)

    masks = _ml_masks(q_ref.shape[0])
    for h in range(ML_HEADS):
        sl = slice(h * ML_DH, (h + 1) * ML_DH)
        hc, c_new, n_new, m_new = _ml_chunk(
            q_ref[:, sl], k_ref[:, sl], v_ref[:, sl], g_ref[0, h:h + 1, :], g_ref[0, ML_HEADS + h:ML_HEADS + h + 1, :],
            cs[h], ns[h:h + 1, :], ms[h:h + 1, :], masks)
        cs[h] = c_new
        ns[h:h + 1, :] = n_new
        ms[h:h + 1, :] = m_new
        o_ref[:, sl] = _ml_gate_out(hc, gn_ref[:, sl], sk_ref[:, sl], xc_ref[:, sl], z_ref[:, sl])

    @pl.when(c == nc - 1)
    def _():
        co_ref[0] = cs[...]
        no_ref[0] = ns[...]
        mo_ref[0] = ms[...]


def _ml_cell(q, k, v, xc, xmz, gates, gn, skip, n_seq, seq_len, chunk):
    t_all = n_seq * seq_len
    nc = seq_len // chunk
    row_spec = pl.BlockSpec((chunk, ML_INNER), lambda b, c: (b * nc + c, 0))
    full = pl.BlockSpec((1, ML_INNER), lambda b, c: (0, 0))
    out = pl.pallas_call(
        _ml_cell_body,
        grid=(n_seq, nc),
        in_specs=[row_spec, row_spec, row_spec, row_spec,
                  pl.BlockSpec((chunk, ML_INNER), lambda b, c: (b * nc + c, 1)),
                  pl.BlockSpec((1, 2 * ML_HEADS, chunk), lambda b, c: (b * nc + c, 0, 0)), full, full],
        out_specs=[row_spec,
                   pl.BlockSpec((1, ML_HEADS, ML_DH, ML_DH), lambda b, c: (b, 0, 0, 0)),
                   pl.BlockSpec((1, ML_HEADS, ML_DH), lambda b, c: (b, 0, 0)),
                   pl.BlockSpec((1, ML_HEADS, 1), lambda b, c: (b, 0, 0))],
        out_shape=[jax.ShapeDtypeStruct((t_all, ML_INNER), F32),
                   jax.ShapeDtypeStruct((n_seq, ML_HEADS, ML_DH, ML_DH), F32),
                   jax.ShapeDtypeStruct((n_seq, ML_HEADS, ML_DH), F32),
                   jax.ShapeDtypeStruct((n_seq, ML_HEADS, 1), F32)],
        scratch_shapes=[pltpu.VMEM((ML_HEADS, ML_DH, ML_DH), F32), pltpu.VMEM((ML_HEADS, ML_DH), F32),
                        pltpu.VMEM((ML_HEADS, 1), F32)],
        compiler_params=_params(("arbitrary", "arbitrary"), 56),
        name="ml_cell",
    )(q, k, v, xc, xmz, gates, gn, skip)
    gated, c_new, n_new, m_new = out
    return gated, c_new, n_new, m_new.reshape(n_seq, ML_HEADS)


def _ml_cell_sample_body(*refs):
    q_ref, k_ref, v_ref, xc_ref, z_ref, g_ref, gn_ref, sk_ref, c0_ref, n0_ref, m0_ref = refs[:11]
    o_ref, co_ref, no_ref, mo_ref = refs[-4:]
    masks = _ml_masks(q_ref.shape[0])
    for h in range(ML_HEADS):
        sl = slice(h * ML_DH, (h + 1) * ML_DH)
        hc, c_new, n_new, m_new = _ml_chunk(
            q_ref[:, sl], k_ref[:, sl], v_ref[:, sl], g_ref[0, h:h + 1, :], g_ref[0, ML_HEADS + h:ML_HEADS + h + 1, :],
            c0_ref[0, h], n0_ref[0, h:h + 1, :], m0_ref[0, h:h + 1, :], masks)
        o_ref[:, sl] = _ml_gate_out(hc, gn_ref[:, sl], sk_ref[:, sl], xc_ref[:, sl], z_ref[:, sl])
        co_ref[0, h] = c_new
        no_ref[0, h:h + 1, :] = n_new
        mo_ref[0, h:h + 1, :] = m_new


def _ml_cell_sample(q, k, v, xc, xmz, gates, gn, skip, n_seq, seq_len, row0, c_all, n_all, m_all, layer, n_layers,
                    c_out_prev):
    blk0 = row0 // seq_len
    s0 = layer * n_seq
    row_spec = pl.BlockSpec((seq_len, ML_INNER), lambda b: (blk0 + b, 0))
    full = pl.BlockSpec((1, ML_INNER), lambda b: (0, 0))
    c_spec = pl.BlockSpec((1, ML_HEADS, ML_DH, ML_DH), lambda b: (s0 + b, 0, 0, 0))
    in_specs = [row_spec, row_spec, row_spec, row_spec,
                pl.BlockSpec((seq_len, ML_INNER), lambda b: (blk0 + b, 1)),
                pl.BlockSpec((1, 2 * ML_HEADS, seq_len), lambda b: (b, 0, 0)), full, full,
                c_spec,
                pl.BlockSpec((1, ML_HEADS, ML_DH), lambda b: (s0 + b, 0, 0)),
                pl.BlockSpec((1, ML_HEADS, 1), lambda b: (s0 + b, 0, 0))]
    args = [q, k, v, xc, xmz, gates, gn, skip, c_all, n_all, m_all]
    aliases = {}
    if c_out_prev is not None:
        in_specs.append(pl.BlockSpec(memory_space=pl.ANY))
        args.append(c_out_prev)
        aliases[len(args) - 1] = 1
    return pl.pallas_call(
        _ml_cell_sample_body,
        grid=(n_seq,),
        in_specs=in_specs,
        out_specs=[pl.BlockSpec((seq_len, ML_INNER), lambda b: (b, 0)), c_spec,
                   pl.BlockSpec((1, ML_HEADS, ML_DH), lambda b: (b, 0, 0)),
                   pl.BlockSpec((1, ML_HEADS, 1), lambda b: (b, 0, 0))],
        out_shape=[jax.ShapeDtypeStruct((n_seq * seq_len, ML_INNER), F32),
                   jax.ShapeDtypeStruct((n_layers * n_seq, ML_HEADS, ML_DH, ML_DH), F32),
                   jax.ShapeDtypeStruct((n_seq, ML_HEADS, ML_DH), F32),
                   jax.ShapeDtypeStruct((n_seq, ML_HEADS, 1), F32)],
        input_output_aliases=aliases,
        compiler_params=_params(("arbitrary",), 48),
        name="ml_cell_sample",
    )(*args)


def _ml_weights(w_in, conv_w, conv_b, wq, wk, wv, w_i, b_i, w_f, b_f, gn_g, skip, w_out):
    nt = ML_INNER // V7X_MXU_DIM
    per = V7X_MXU_DIM // ML_QKV_BLOCK
    eye = jnp.eye(per, dtype=F32)

    def block_diag(w):
        wt = w.reshape(nt, per, ML_QKV_BLOCK, ML_QKV_BLOCK)
        return jnp.einsum("tncd,nm->tncmd", wt, eye).reshape(nt, V7X_MXU_DIM, V7X_MXU_DIM).astype(BF16)

    wg = jnp.concatenate([w_i, w_f], axis=1).T.astype(BF16)
    return dict(w_in=w_in.astype(BF16), conv_w=conv_w, conv_b=conv_b.reshape(1, ML_INNER),
                wq=block_diag(wq), wk=block_diag(wk), wv=block_diag(wv),
                wgq=wg[:, :ML_INNER], wgk=wg[:, ML_INNER:2 * ML_INNER], wgv=wg[:, 2 * ML_INNER:],
                gb=jnp.concatenate([b_i, b_f]).reshape(2 * ML_HEADS, 1),
                gn=gn_g.reshape(1, ML_INNER), skip=skip.reshape(1, ML_INNER), w_out=w_out.astype(BF16))


def _ml_layer(x_all, w, dims, c_all, n_all, m_all, state_conv, layer, n_layers, c_out_prev, ln_g, ln_b):
    n_p, seq_len, n_s, sample_len = dims
    t_p = n_p * seq_len
    rb = ROW_BLOCK
    xmz = _mm(x_all, w["w_in"], 2048)
    pad = jnp.pad(state_conv, ((0, 0), (0, sample_len - (ML_CONV - 1)), (0, 0))).reshape(n_s * sample_len, ML_INNER)
    conv_rows = jnp.concatenate([jnp.zeros((rb, ML_INNER), F32), pad], axis=0)
    q, k, v, xc, g3 = _ml_pre(xmz, conv_rows, w, t_p, seq_len, sample_len)
    npb = t_p // rb
    g_p = g3[:npb]
    if ML_CHUNK != rb:
        g_p = g_p.reshape(npb, 2 * ML_HEADS, rb // ML_CHUNK, ML_CHUNK).transpose(0, 2, 1, 3).reshape(-1, 2 * ML_HEADS, ML_CHUNK)
    g_s = g3[npb:].reshape(-1, 2 * ML_HEADS, rb // sample_len, sample_len).transpose(0, 2, 1, 3)
    g_s = g_s.reshape(n_s, 2 * ML_HEADS, sample_len)
    gated_p, pc, pn, pm = _ml_cell(q, k, v, xc, xmz, g_p, w["gn"], w["skip"], n_p, seq_len, ML_CHUNK)
    gated_s, c_out, sn, sm = _ml_cell_sample(q, k, v, xc, xmz, g_s, w["gn"], w["skip"], n_s, sample_len, t_p,
                                             c_all, n_all, m_all, layer, n_layers, c_out_prev)
    h1 = _mm_ln(gated_p, gated_s, w["w_out"], x_all, ln_g, ln_b)
    tail = ML_CONV - 1
    p_conv = jnp.stack([xmz[(b + 1) * seq_len - tail:(b + 1) * seq_len, :ML_INNER] for b in range(n_p)])
    s_conv = xmz[t_p:, :ML_INNER].reshape(n_s, sample_len, ML_INNER)[:, sample_len - tail:]
    return h1, (pc, pn, pm, p_conv), (sn, sm.reshape(n_s, ML_HEADS), s_conv), c_out


def _gla_in_body(x_ref, w_ref, wa_ref, wa2_ref, ba_ref, o_ref, la_ref):
    x = x_ref[...].astype(BF16)
    o_ref[...] = jnp.dot(x, w_ref[...], preferred_element_type=F32)
    a = jnp.dot(x, wa_ref[...], preferred_element_type=F32)
    la_ref[...] = _log_sigmoid(_dot(a, wa2_ref[...]) + ba_ref[...]) / GLA_TAU


def _gla_in(x, w):
    m, k = x.shape
    tm = MM_ROWS
    n1 = 2 * GLA_KT + 2 * GLA_VT
    full = lambda shape: pl.BlockSpec(shape, lambda i: (0, 0))
    return pl.pallas_call(
        _gla_in_body,
        grid=(m // tm,),
        in_specs=[pl.BlockSpec((tm, k), lambda i: (i, 0)), full((k, n1)), full((k, V7X_LANES)),
                  full((V7X_LANES, GLA_KT)), full((1, GLA_KT))],
        out_specs=[pl.BlockSpec((tm, n1), lambda i: (i, 0)), pl.BlockSpec((tm, GLA_KT), lambda i: (i, 0))],
        out_shape=[jax.ShapeDtypeStruct((m, n1), F32), jax.ShapeDtypeStruct((m, GLA_KT), F32)],
        compiler_params=_params(("arbitrary",), 48),
        name="gla_in",
    )(x, w["w_qkvr"], w["w_a"], w["w_a2"], w["b_a"])


def _gla_levels(chunk):
    t = jnp.arange(chunk)[:, None]
    s = jnp.arange(chunk)[None, :]
    mats = [(s <= t)]
    size = chunk
    while size >= 2:
        ref = t - t % size + size // 2 - 1
        mats.append(s <= ref)
        size //= 2
    levels = jnp.concatenate(mats, axis=0).astype(BF16)
    return jnp.pad(levels, ((0, 0), (0, max(V7X_LANES - chunk, 0))))


def _gla_chunk(q, k, v, la, lv, st):
    L = q.shape[0]
    kpad = lv.shape[1]
    parts = jnp.concatenate(_split3(la), axis=1)
    if kpad > L:
        parts = jnp.concatenate([parts, jnp.zeros((kpad - L, parts.shape[1]), BF16)], axis=0)
    cum = jnp.dot(lv, parts, preferred_element_type=F32)
    cum = cum[:, :GLA_DK] + cum[:, GLA_DK:2 * GLA_DK] + cum[:, 2 * GLA_DK:]
    b = cum[0:L]
    ri = lax.broadcasted_iota(jnp.int32, (L, L), 0)
    ci = lax.broadcasted_iota(jnp.int32, (L, L), 1)
    tpos = lax.broadcasted_iota(jnp.int32, (L, 1), 0)
    att = jnp.where(ri == ci, jnp.sum(q * k, axis=1, keepdims=True), 0.0)
    size = L
    lvl = 1
    while size >= 2:
        bref = cum[lvl * L:(lvl + 1) * L]
        upper = (tpos % size) >= (size // 2)
        qs = q * jnp.exp(jnp.where(upper, b - bref, -jnp.inf))
        ks = k * jnp.exp(jnp.where(upper, -jnp.inf, bref - b))
        att = att + jnp.where((ri // size) == (ci // size), _dot_nt(qs, ks), 0.0)
        size //= 2
        lvl += 1
    o = _dot(att, v) + _dot(q * jnp.exp(b), st)
    b_end = b[L - 1:L, :]
    e_end = jnp.exp(b_end)
    rk = lax.broadcasted_iota(jnp.int32, (GLA_DK, GLA_DK), 0)
    ck = lax.broadcasted_iota(jnp.int32, (GLA_DK, GLA_DK), 1)
    e_col = jnp.sum(jnp.where(rk == ck, e_end, 0.0), axis=1, keepdims=True)
    st_new = e_col * st + _dot_tn(k * jnp.exp(b_end - b), v)
    return o, st_new


def _gla_heads(x_ref, la_ref, lv_ref, gn_ref, o_ref, read_state, write_state):
    lv = lv_ref[...]
    for h in range(GLA_HEADS):
        qs = slice(h * GLA_DK, (h + 1) * GLA_DK)
        ks = slice(GLA_KT + h * GLA_DK, GLA_KT + (h + 1) * GLA_DK)
        vs = slice(2 * GLA_KT + h * GLA_DV, 2 * GLA_KT + (h + 1) * GLA_DV)
        rs = slice(2 * GLA_KT + GLA_VT + h * GLA_DV, 2 * GLA_KT + GLA_VT + (h + 1) * GLA_DV)
        os_ = slice(h * GLA_DV, (h + 1) * GLA_DV)
        o, st_new = _gla_chunk(x_ref[:, qs] * (GLA_DK ** -0.5), x_ref[:, ks], x_ref[:, vs], la_ref[:, qs], lv,
                               read_state(h))
        o_ref[:, os_] = _silu(x_ref[:, rs]) * _head_norm(o, gn_ref[:, os_])
        write_state(h, st_new)


def _gla_cell_body(x_ref, la_ref, lv_ref, gn_ref, o_ref, so_ref, ss):
    c = pl.program_id(1)
    nc = pl.num_programs(1)

    @pl.when(c == 0)
    def _():
        ss[...] = jnp.zeros(ss.shape, F32)

    def write(h, st):
        ss[h] = st

    _gla_heads(x_ref, la_ref, lv_ref, gn_ref, o_ref, lambda h: ss[h], write)

    @pl.when(c == nc - 1)
    def _():
        so_ref[0] = ss[...]


def _gla_cell(qkvr, la, gn, n_seq, seq_len, chunk):
    n1 = qkvr.shape[1]
    nc = seq_len // chunk
    levels = _gla_levels(chunk)
    return pl.pallas_call(
        _gla_cell_body,
        grid=(n_seq, nc),
        in_specs=[pl.BlockSpec((chunk, n1), lambda b, c: (b * nc + c, 0)),
                  pl.BlockSpec((chunk, GLA_KT), lambda b, c: (b * nc + c, 0)),
                  pl.BlockSpec(levels.shape, lambda b, c: (0, 0)),
                  pl.BlockSpec((1, GLA_VT), lambda b, c: (0, 0))],
        out_specs=[pl.BlockSpec((chunk, GLA_VT), lambda b, c: (b * nc + c, 0)),
                   pl.BlockSpec((1, GLA_HEADS, GLA_DK, GLA_DV), lambda b, c: (b, 0, 0, 0))],
        out_shape=[jax.ShapeDtypeStruct((n_seq * seq_len, GLA_VT), F32),
                   jax.ShapeDtypeStruct((n_seq, GLA_HEADS, GLA_DK, GLA_DV), F32)],
        scratch_shapes=[pltpu.VMEM((GLA_HEADS, GLA_DK, GLA_DV), F32)],
        compiler_params=_params(("arbitrary", "arbitrary"), 32),
        name="gla_cell",
    )(qkvr, la, levels, gn)


def _gla_cell_sample_body(x_ref, la_ref, lv_ref, gn_ref, s0_ref, o_ref, so_ref):
    def write(h, st):
        so_ref[0, h] = st

    _gla_heads(x_ref, la_ref, lv_ref, gn_ref, o_ref, lambda h: s0_ref[0, h], write)


def _gla_cell_sample(qkvr, la, gn, n_seq, seq_len, row0, state):
    n1 = qkvr.shape[1]
    blk0 = row0 // seq_len
    levels = _gla_levels(seq_len)
    st_spec = pl.BlockSpec((1, GLA_HEADS, GLA_DK, GLA_DV), lambda b: (b, 0, 0, 0))
    return pl.pallas_call(
        _gla_cell_sample_body,
        grid=(n_seq,),
        in_specs=[pl.BlockSpec((seq_len, n1), lambda b: (blk0 + b, 0)),
                  pl.BlockSpec((seq_len, GLA_KT), lambda b: (blk0 + b, 0)),
                  pl.BlockSpec(levels.shape, lambda b: (0, 0)),
                  pl.BlockSpec((1, GLA_VT), lambda b: (0, 0)),
                  st_spec],
        out_specs=[pl.BlockSpec((seq_len, GLA_VT), lambda b: (b, 0)), st_spec],
        out_shape=[jax.ShapeDtypeStruct((n_seq * seq_len, GLA_VT), F32),
                   jax.ShapeDtypeStruct((n_seq, GLA_HEADS, GLA_DK, GLA_DV), F32)],
        compiler_params=_params(("arbitrary",), 32),
        name="gla_cell_sample",
    )(qkvr, la, levels, gn, state)


def _gla_weights(w_in, w_a2, b_a, gn_g, w_out):
    n1 = 2 * GLA_KT + 2 * GLA_VT
    w_a = jnp.pad(w_in[:, n1:], ((0, 0), (0, V7X_LANES - GLA_RANK)))
    w_a2p = jnp.pad(w_a2, ((0, V7X_LANES - GLA_RANK), (0, 0)))
    return dict(w_qkvr=w_in[:, :n1].astype(BF16), w_a=w_a.astype(BF16), w_a2=w_a2p.astype(BF16),
                b_a=b_a.reshape(1, GLA_KT), gn=gn_g.reshape(1, GLA_VT), w_out=w_out.astype(BF16))


def _gla_layer(x_all, w, dims, state_s, ln_g, ln_b):
    n_p, seq_len, n_s, sample_len = dims
    t_p = n_p * seq_len
    qkvr, la = _gla_in(x_all, w)
    gated_p, p_s = _gla_cell(qkvr, la, w["gn"], n_p, seq_len, min(GLA_CHUNK, seq_len))
    gated_s, s_s = _gla_cell_sample(qkvr, la, w["gn"], n_s, sample_len, t_p, state_s)
    h1 = _mm_ln(gated_p, gated_s, w["w_out"], x_all, ln_g, ln_b)
    return h1, p_s, s_s


def _s5_body(x_ref, sre_ref, sim_ref, win_ref, bt_ref, cre_ref, cim_ref, lre_ref, lim_ref, d_ref, wglu_ref,
             g_ref, b_ref, o_ref, ore_ref, oim_ref, st_re, st_im, car_re, car_im, *, nb, lc):
    c = pl.program_id(1)
    nc = pl.num_programs(1)
    rows = nb * lc
    ntile = D_MODEL // S5_TILE_CH

    @pl.when(c == 0)
    def _():
        car_re[...] = sre_ref[...]
        car_im[...] = sim_ref[...]

    x = x_ref[...].reshape(rows, D_MODEL)
    u = _dot(x, win_ref[...])
    lpt = S5_TILE_ST // V7X_LANES
    for t in range(ntile):
        bu = _dot(u[:, t * S5_TILE_CH:(t + 1) * S5_TILE_CH], bt_ref[t])
        for j in range(lpt):
            st_re[t * lpt + j] = bu[:, j * V7X_LANES:(j + 1) * V7X_LANES]
            st_im[t * lpt + j] = bu[:, S5_TILE_ST + j * V7X_LANES:S5_TILE_ST + (j + 1) * V7X_LANES]

    for gidx in range(nb // V7X_SUBLANES):
        grp = slice(gidx * V7X_SUBLANES, (gidx + 1) * V7X_SUBLANES)
        for t in range(ntile):
            slabs = list(range(t * lpt, (t + 1) * lpt))
            lanes = [slice(j * V7X_LANES, (j + 1) * V7X_LANES) for j in slabs]
            lre = [lre_ref[:, ln] for ln in lanes]
            lim = [lim_ref[:, ln] for ln in lanes]

            def step(tok, carry, slabs=slabs, lre=lre, lim=lim, gidx=gidx):
                sel = pl.ds(pl.multiple_of(tok * nb + gidx * V7X_SUBLANES, V7X_SUBLANES), V7X_SUBLANES)
                out = []
                for n, j in enumerate(slabs):
                    pr, pi = carry[2 * n], carry[2 * n + 1]
                    nr = lre[n] * pr - lim[n] * pi + st_re[j, sel, :]
                    ni = lre[n] * pi + lim[n] * pr + st_im[j, sel, :]
                    st_re[j, sel, :] = nr
                    st_im[j, sel, :] = ni
                    out += [nr, ni]
                return tuple(out)

            init = []
            for ln in lanes:
                init += [car_re[grp, ln], car_im[grp, ln]]
            fin = lax.fori_loop(0, lc, step, tuple(init))
            for n, ln in enumerate(lanes):
                car_re[grp, ln] = fin[2 * n]
                car_im[grp, ln] = fin[2 * n + 1]

    ys = []
    for t in range(ntile):
        sre = jnp.concatenate([st_re[t * lpt + j] for j in range(lpt)], axis=1)
        sim = jnp.concatenate([st_im[t * lpt + j] for j in range(lpt)], axis=1)
        ys.append(_dot(sre, cre_ref[t]) - _dot(sim, cim_ref[t]))
    y = jnp.concatenate(ys, axis=1) + d_ref[...] * u
    y = jax.nn.gelu(y)
    vg = _dot(y, wglu_ref[...])
    mix = vg[:, :D_MODEL] * _sigmoid(vg[:, D_MODEL:])
    o_ref[...] = _layer_norm(DEEPNORM_ALPHA * x + mix, g_ref[...], b_ref[...]).reshape(lc, nb, D_MODEL)

    @pl.when(c == nc - 1)
    def _():
        ore_ref[...] = car_re[...]
        oim_ref[...] = car_im[...]


def _s5_call(x3, s_re, s_im, w, ln_g, ln_b, nb, lc):
    seq_len, n_seq, _ = x3.shape
    ntile = D_MODEL // S5_TILE_CH
    rows = nb * lc
    full2 = lambda shape: pl.BlockSpec(shape, lambda i, c: (0, 0))
    full3 = lambda shape: pl.BlockSpec(shape, lambda i, c: (0, 0, 0))
    st_spec = pl.BlockSpec((nb, S5_STATE), lambda i, c: (i, 0))
    x_spec = pl.BlockSpec((lc, nb, D_MODEL), lambda i, c: (c, i, 0))
    return pl.pallas_call(
        functools.partial(_s5_body, nb=nb, lc=lc),
        grid=(n_seq // nb, seq_len // lc),
        in_specs=[x_spec, st_spec, st_spec,
                  full2((D_MODEL, D_MODEL)), full3((ntile, S5_TILE_CH, 2 * S5_TILE_ST)),
                  full3((ntile, S5_TILE_ST, S5_TILE_CH)), full3((ntile, S5_TILE_ST, S5_TILE_CH)),
                  full2((V7X_SUBLANES, S5_STATE)), full2((V7X_SUBLANES, S5_STATE)), full2((1, D_MODEL)),
                  full2((D_MODEL, 2 * D_MODEL)), full2((1, D_MODEL)), full2((1, D_MODEL))],
        out_specs=[x_spec, st_spec, st_spec],
        out_shape=[jax.ShapeDtypeStruct(x3.shape, F32), jax.ShapeDtypeStruct((n_seq, S5_STATE), F32),
                   jax.ShapeDtypeStruct((n_seq, S5_STATE), F32)],
        scratch_shapes=[pltpu.VMEM((S5_STATE // V7X_LANES, rows, V7X_LANES), F32),
                        pltpu.VMEM((S5_STATE // V7X_LANES, rows, V7X_LANES), F32),
                        pltpu.VMEM((nb, S5_STATE), F32), pltpu.VMEM((nb, S5_STATE), F32)],
        compiler_params=_params(("arbitrary", "arbitrary"), 56),
        name="s5",
    )(x3, s_re, s_im, w["w_in"], w["b_tiles"], w["c_re"], w["c_im"], w["lam_re"], w["lam_im"], w["d"], w["w_glu"],
      ln_g.reshape(1, D_MODEL), ln_b.reshape(1, D_MODEL))


def _s5_weights(w_in, a_re, a_im, log_dt, b_re, b_im, c_re, c_im, d_skip, w_glu):
    lam = lax.complex(a_re.astype(F32), a_im.astype(F32))
    dt = jnp.exp(log_dt.astype(F32))[:, None]
    lam_bar = jnp.exp(lam * dt)
    b_bar = ((lam_bar - 1.0) / lam)[..., None] * lax.complex(b_re.astype(F32), b_im.astype(F32))
    ntile = D_MODEL // S5_TILE_CH
    gpt = S5_TILE_CH // S5_GC
    eye = jnp.eye(gpt, dtype=F32)

    def b_tiles(bb):
        return jnp.einsum("igpc,gh->igchp", bb.reshape(ntile, gpt, S5_P, S5_GC), eye).reshape(ntile, S5_TILE_CH, S5_TILE_ST)

    def c_tiles(cc):
        return jnp.einsum("igcp,gh->igphc", cc.reshape(ntile, gpt, S5_GC, S5_P), eye).reshape(ntile, S5_TILE_ST, S5_TILE_CH)

    bt = jnp.concatenate([b_tiles(b_bar.real), b_tiles(b_bar.imag)], axis=2).astype(BF16)
    bcast = lambda a: jnp.broadcast_to(a.reshape(1, S5_STATE), (V7X_SUBLANES, S5_STATE))
    return dict(w_in=w_in.astype(BF16), b_tiles=bt, c_re=c_tiles(c_re.astype(F32)).astype(BF16),
                c_im=c_tiles(c_im.astype(F32)).astype(BF16), lam_re=bcast(lam_bar.real), lam_im=bcast(lam_bar.imag),
                d=d_skip.reshape(1, D_MODEL).astype(F32), w_glu=w_glu.astype(BF16))


def _s5_layer(x_all, w, dims, state_re, state_im, ln_g, ln_b):
    n_p, seq_len, n_s, sample_len = dims
    t_p = n_p * seq_len
    xp = x_all[:t_p].reshape(n_p, seq_len, D_MODEL).transpose(1, 0, 2)
    xs = x_all[t_p:].reshape(n_s, sample_len, D_MODEL).transpose(1, 0, 2)
    zero = jnp.zeros((n_p, S5_STATE), F32)
    hp, p_re, p_im = _s5_call(xp, zero, zero, w, ln_g, ln_b, n_p, min(S5_CHUNK, seq_len))
    nb_s = min(n_s, ROW_BLOCK // sample_len)
    hs, s_re, s_im = _s5_call(xs, state_re.reshape(n_s, S5_STATE), state_im.reshape(n_s, S5_STATE), w, ln_g, ln_b,
                              nb_s, sample_len)
    h1 = jnp.concatenate([hp.transpose(1, 0, 2).reshape(t_p, D_MODEL),
                          hs.transpose(1, 0, 2).reshape(n_s * sample_len, D_MODEL)], axis=0)
    shp = lambda a, n: a.reshape(n, S5_GROUPS, S5_P)
    return h1, (shp(p_re, n_p), shp(p_im, n_p)), (shp(s_re, n_s), shp(s_im, n_s))


def _route_body(x_ref, wh_ref, wl_ref, br_ref, g_ref):
    x = x_ref[...]
    xh = x.astype(BF16)
    xl = (x - xh.astype(F32)).astype(BF16)
    wh = wh_ref[...]
    logits = _dot_nt(wh, xh) + _dot_nt(wh, xl) + _dot_nt(wl_ref[...], xh)
    s = _sigmoid(logits)
    work = s + br_ref[...]
    row = lax.broadcasted_iota(jnp.int32, s.shape, 0).astype(F32)
    chosen = jnp.zeros(s.shape, jnp.bool_)
    for _ in range(TOP_K):
        mx = jnp.max(work, axis=0, keepdims=True)
        idx = jnp.min(jnp.where(work == mx, row, float(N_EXPERTS)), axis=0, keepdims=True)
        hit = row == idx
        chosen = jnp.logical_or(chosen, hit)
        work = jnp.where(hit, -jnp.inf, work)
    sel = jnp.where(chosen, s, 0.0)
    g_ref[...] = sel / jnp.sum(sel, axis=0, keepdims=True) * ROUTE_SCALE


def _route(x, w):
    m, k = x.shape
    tm = ROW_BLOCK
    full = lambda shape: pl.BlockSpec(shape, lambda i: (0, 0))
    gates_t = pl.pallas_call(
        _route_body,
        grid=(m // tm,),
        in_specs=[pl.BlockSpec((tm, k), lambda i: (i, 0)), full((N_EXPERTS, k)), full((N_EXPERTS, k)),
                  full((N_EXPERTS, 1))],
        out_specs=pl.BlockSpec((N_EXPERTS, tm), lambda i: (0, i)),
        out_shape=jax.ShapeDtypeStruct((N_EXPERTS, m), F32),
        compiler_params=_params(("arbitrary",), 32),
        name="route",
    )(x, w["wr_hi"], w["wr_lo"], w["b_router"])
    return gates_t.T


def _moe_body(x_ref, gd_ref, wg_ref, wu_ref, wd_ref, sg_ref, su_ref, sd_ref, g_ref, b_ref, o_ref, acc, xb_ref):
    e = pl.program_id(1)
    ne = pl.num_programs(1)

    @pl.when(e == 0)
    def _():
        xb0 = x_ref[...].astype(BF16)
        xb_ref[...] = xb0
        hs = _silu(jnp.dot(xb0, sg_ref[...], preferred_element_type=F32)) * jnp.dot(xb0, su_ref[...], preferred_element_type=F32)
        acc[...] = _dot(hs, sd_ref[...])

    xb = xb_ref[...]
    gd = gd_ref[...]
    lane = lax.broadcasted_iota(jnp.int32, gd.shape, 1)
    total = None
    for j in range(MOE_EXPERTS_PER_STEP):
        h = _silu(jnp.dot(xb, wg_ref[j], preferred_element_type=F32)) * jnp.dot(xb, wu_ref[j], preferred_element_type=F32)
        gate = jnp.sum(jnp.where(lane == e * MOE_EXPERTS_PER_STEP + j, gd, 0.0), axis=1, keepdims=True)
        y = _dot(h, wd_ref[j]) * gate
        total = y if total is None else total + y
    acc[...] += total

    @pl.when(e == ne - 1)
    def _():
        o_ref[...] = _layer_norm(DEEPNORM_ALPHA * x_ref[...] + acc[...], g_ref[...], b_ref[...])


def _moe_ffn(x, gates, w, experts, layer, ln_g, ln_b):
    m, k = x.shape
    tm = MOE_ROWS
    steps = N_EXPERTS // MOE_EXPERTS_PER_STEP
    full = lambda shape: pl.BlockSpec(shape, lambda i, e: (0, 0))
    expert_map = lambda i, e: (layer * steps + e, 0, 0)
    return pl.pallas_call(
        _moe_body,
        grid=(m // tm, steps),
        in_specs=[pl.BlockSpec((tm, k), lambda i, e: (i, 0)), pl.BlockSpec((tm, N_EXPERTS), lambda i, e: (i, 0)),
                  pl.BlockSpec((MOE_EXPERTS_PER_STEP, k, EXPERT_FF), expert_map),
                  pl.BlockSpec((MOE_EXPERTS_PER_STEP, k, EXPERT_FF), expert_map),
                  pl.BlockSpec((MOE_EXPERTS_PER_STEP, EXPERT_FF, k), expert_map),
                  full((k, w["ws_gate"].shape[1])), full((k, w["ws_up"].shape[1])), full((w["ws_down"].shape[0], k)),
                  full((1, k)), full((1, k))],
        out_specs=pl.BlockSpec((tm, k), lambda i, e: (i, 0)),
        out_shape=jax.ShapeDtypeStruct((m, k), F32),
        scratch_shapes=[pltpu.VMEM((tm, k), F32), pltpu.VMEM((tm, k), BF16)],
        compiler_params=_params(("arbitrary", "arbitrary"), 48),
        name="moe_ffn",
    )(x, gates, experts[0], experts[1], experts[2], w["ws_gate"], w["ws_up"], w["ws_down"],
      ln_g.reshape(1, k), ln_b.reshape(1, k))


def _moe_weights(w_router, b_router, ws_gate, ws_up, ws_down):
    wr_t = w_router.T
    wr_hi = wr_t.astype(BF16)
    wr_lo = (wr_t - wr_hi.astype(F32)).astype(BF16)
    return dict(wr_hi=wr_hi, wr_lo=wr_lo, b_router=b_router.reshape(N_EXPERTS, 1).astype(F32),
                ws_gate=ws_gate.astype(BF16), ws_up=ws_up.astype(BF16), ws_down=ws_down.astype(BF16))


def _moe_layer(x, w, experts, layer, ln_g, ln_b):
    return _moe_ffn(x, _route(x, w), w, experts, layer, ln_g, ln_b)


def kernel(x_prompt, x_sample, state_mlstm_C, state_mlstm_n, state_mlstm_m, state_mlstm_conv, state_gla_S, state_s5_re, state_s5_im, ln1_g, ln1_b, ln2_g, ln2_b, ml_w_in, ml_conv_w, ml_conv_b, ml_wq, ml_wk, ml_wv, ml_w_i, ml_b_i, ml_w_f, ml_b_f, ml_gn_g, ml_skip, ml_w_out, gla_w_in, gla_w_a2, gla_b_a, gla_gn_g, gla_w_out, s5_w_in, s5_a_re, s5_a_im, s5_log_dt, s5_b_re, s5_b_im, s5_c_re, s5_c_im, s5_d, s5_w_glu, moe_w_router, moe_b_router, moe_w_gate, moe_w_up, moe_w_down, moe_ws_gate, moe_ws_up, moe_ws_down):
    n_p, seq_len, _ = x_prompt.shape
    n_s, sample_len, _ = x_sample.shape
    dims = (n_p, seq_len, n_s, sample_len)
    t_p = n_p * seq_len
    t_s = n_s * sample_len
    assert seq_len % MM_ROWS == 0 and t_s % MM_ROWS == 0 and MM_ROWS % ROW_BLOCK == 0 and ROW_BLOCK % sample_len == 0
    assert sample_len == V7X_SUBLANES and n_p % V7X_SUBLANES == 0 and (t_p + t_s) % MOE_ROWS == 0
    x = jnp.concatenate([x_prompt.reshape(t_p, D_MODEL), x_sample.reshape(t_s, D_MODEL)], axis=0)
    n_ml = len(range(0, DEPTH, N_MIXERS))
    c_all = state_mlstm_C.reshape(-1, ML_HEADS, ML_DH, ML_DH)
    n_all = state_mlstm_n.reshape(-1, ML_HEADS, ML_DH)
    m_all = state_mlstm_m.reshape(-1, ML_HEADS, 1)
    c_out = None
    flat_experts = lambda w: w.astype(BF16).reshape((-1,) + w.shape[2:])
    experts = (flat_experts(moe_w_gate), flat_experts(moe_w_up), flat_experts(moe_w_down))
    p_ml, s_ml, p_gla, s_gla, p_s5, s_s5 = [], [], [], [], [], []
    for i in range(DEPTH):
        j = i // N_MIXERS
        if i % N_MIXERS == 0:
            w = _ml_weights(ml_w_in[j], ml_conv_w[j], ml_conv_b[j], ml_wq[j], ml_wk[j], ml_wv[j], ml_w_i[j], ml_b_i[j],
                            ml_w_f[j], ml_b_f[j], ml_gn_g[j], ml_skip[j], ml_w_out[j])
            x, ps, ss, c_out = _ml_layer(x, w, dims, c_all, n_all, m_all, state_mlstm_conv[j], j, n_ml, c_out,
                                         ln1_g[i], ln1_b[i])
            p_ml.append(ps)
            s_ml.append(ss)
        elif i % N_MIXERS == 1:
            w = _gla_weights(gla_w_in[j], gla_w_a2[j], gla_b_a[j], gla_gn_g[j], gla_w_out[j])
            x, ps, ss = _gla_layer(x, w, dims, state_gla_S[j], ln1_g[i], ln1_b[i])
            p_gla.append(ps)
            s_gla.append(ss)
        else:
            w = _s5_weights(s5_w_in[j], s5_a_re[j], s5_a_im[j], s5_log_dt[j], s5_b_re[j], s5_b_im[j], s5_c_re[j],
                            s5_c_im[j], s5_d[j], s5_w_glu[j])
            x, ps, ss = _s5_layer(x, w, dims, state_s5_re[j], state_s5_im[j], ln1_g[i], ln1_b[i])
            p_s5.append(ps)
            s_s5.append(ss)
        wm = _moe_weights(moe_w_router[i], moe_b_router[i], moe_ws_gate[i], moe_ws_up[i], moe_ws_down[i])
        x = _moe_layer(x, wm, experts, i, ln2_g[i], ln2_b[i])
    stack = lambda items, idx: jnp.stack([it[idx] for it in items])
    return (x[:t_p].reshape(n_p, seq_len, D_MODEL), x[t_p:].reshape(n_s, sample_len, D_MODEL),
            stack(p_ml, 0), stack(p_ml, 1), stack(p_ml, 2), stack(p_ml, 3), jnp.stack(p_gla),
            stack(p_s5, 0), stack(p_s5, 1),
            c_out.reshape(n_ml, n_s, ML_HEADS, ML_DH, ML_DH), stack(s_ml, 0), stack(s_ml, 1), stack(s_ml, 2),
            jnp.stack(s_gla), stack(s_s5, 0), stack(s_s5, 1))
```

```python
import functools

import jax
import jax.numpy as jnp
from jax import lax
from jax.experimental import pallas as pl
from jax.experimental.pallas import tpu as pltpu

F32 = jnp.float32
BF16 = jnp.bfloat16

D_MODEL = 1024
DEPTH = 4
N_MIXERS = 3
ML_INNER = 2 * D_MODEL
ML_HEADS = 4
ML_DH = ML_INNER // ML_HEADS
ML_QKV_BLOCK = 4
ML_CONV = 4
GLA_HEADS = 4
GLA_KT = D_MODEL // 2
GLA_VT = D_MODEL
GLA_DK = GLA_KT // GLA_HEADS
GLA_DV = GLA_VT // GLA_HEADS
GLA_RANK = 16
GLA_TAU = 16.0
S5_GC = 16
S5_GROUPS = D_MODEL // S5_GC
S5_P = 64
S5_STATE = S5_GROUPS * S5_P
N_EXPERTS = 64
TOP_K = 8
EXPERT_FF = 256
ROUTE_SCALE = 2.5
DEEPNORM_ALPHA = (2.0 * DEPTH) ** 0.25
NORM_EPS = 1e-5

V7X_VMEM_BYTES = 64 * 1024 * 1024
V7X_LANES = 128
V7X_SUBLANES = 8
V7X_MXU_DIM = 256

ROW_BLOCK = 256
MM_ROWS = 512
ML_SAMPLE_SEQS = 2
GLA_SAMPLE_SEQS = 4
ML_CHUNK = 256
GLA_CHUNK = 128
S5_CHUNK = 64
S5_TILE_CH = 128
S5_TILE_ST = S5_TILE_CH // S5_GC * S5_P
MOE_ROWS = 1024
MOE_EXPERTS_PER_STEP = 4


def _params(semantics, vmem_mb):
    assert vmem_mb * 1024 * 1024 < V7X_VMEM_BYTES
    return pltpu.CompilerParams(dimension_semantics=semantics, vmem_limit_bytes=vmem_mb * 1024 * 1024)


def _dot(a, b):
    return jnp.dot(a.astype(BF16), b.astype(BF16), preferred_element_type=F32)


def _dot_nt(a, b):
    return lax.dot_general(a.astype(BF16), b.astype(BF16), (((1,), (1,)), ((), ())), preferred_element_type=F32)


def _dot_tn(a, b):
    return lax.dot_general(a.astype(BF16), b.astype(BF16), (((0,), (0,)), ((), ())), preferred_element_type=F32)


def _sigmoid(x):
    return 1.0 / (1.0 + jnp.exp(-x))


def _silu(x):
    return x * _sigmoid(x)


def _log_sigmoid(x):
    return jnp.minimum(x, 0.0) - jnp.log1p(jnp.exp(-jnp.abs(x)))


def _layer_norm(y, g, b):
    mu = jnp.mean(y, axis=-1, keepdims=True)
    d = y - mu
    var = jnp.mean(d * d, axis=-1, keepdims=True)
    return d * lax.rsqrt(var + NORM_EPS) * g + b


def _head_norm(h, g):
    mu = jnp.mean(h, axis=-1, keepdims=True)
    d = h - mu
    var = jnp.mean(d * d, axis=-1, keepdims=True)
    return d * lax.rsqrt(var + NORM_EPS) * g


def _split3(x):
    hi = x.astype(BF16)
    r1 = x - hi.astype(F32)
    mid = r1.astype(BF16)
    lo = (r1 - mid.astype(F32)).astype(BF16)
    return hi, mid, lo


def _mm_body(x_ref, w_ref, o_ref):
    o_ref[...] = _dot(x_ref[...], w_ref[...])


def _mm(x, w, tn):
    m, k = x.shape
    n = w.shape[1]
    tm = MM_ROWS
    return pl.pallas_call(
        _mm_body,
        grid=(n // tn, m // tm),
        in_specs=[pl.BlockSpec((tm, k), lambda j, i: (i, 0)), pl.BlockSpec((k, tn), lambda j, i: (0, j))],
        out_specs=pl.BlockSpec((tm, tn), lambda j, i: (i, j)),
        out_shape=jax.ShapeDtypeStruct((m, n), F32),
        compiler_params=_params(("arbitrary", "arbitrary"), 40),
        name="mm",
    )(x, w)


def _mm_ln_body(ap_ref, as_ref, w_ref, r_ref, g_ref, b_ref, o_ref, *, n_prompt_blocks):
    i = pl.program_id(0)

    def run(a_ref):
        y = _dot(a_ref[...], w_ref[...])
        o_ref[...] = _layer_norm(DEEPNORM_ALPHA * r_ref[...] + y, g_ref[...], b_ref[...])

    pl.when(i < n_prompt_blocks)(lambda: run(ap_ref))
    pl.when(i >= n_prompt_blocks)(lambda: run(as_ref))


def _mm_ln(a_p, a_s, w, resid, g, b):
    k = a_p.shape[1]
    m = a_p.shape[0] + a_s.shape[0]
    n = w.shape[1]
    tm = MM_ROWS
    npb = a_p.shape[0] // tm
    return pl.pallas_call(
        functools.partial(_mm_ln_body, n_prompt_blocks=npb),
        grid=(m // tm,),
        in_specs=[pl.BlockSpec((tm, k), lambda i: (jnp.minimum(i, npb - 1), 0)),
                  pl.BlockSpec((tm, k), lambda i: (jnp.maximum(i - npb, 0), 0)),
                  pl.BlockSpec((k, n), lambda i: (0, 0)),
                  pl.BlockSpec((tm, n), lambda i: (i, 0)), pl.BlockSpec((1, n), lambda i: (0, 0)),
                  pl.BlockSpec((1, n), lambda i: (0, 0))],
        out_specs=pl.BlockSpec((tm, n), lambda i: (i, 0)),
        out_shape=jax.ShapeDtypeStruct((m, n), F32),
        compiler_params=_params(("arbitrary",), 48),
        name="mm_ln",
    )(a_p, a_s, w, resid, g.reshape(1, n), b.reshape(1, n))


def _ml_pre_body(x_ref, prev_ref, p_ref, cw_ref, cb_ref, wq_ref, wk_ref, wv_ref, wgq_ref, wgk_ref, wgv_ref, gb_ref,
                 q_ref, k_ref, v_ref, xc_ref, g_ref, scx, scp, *, n_prompt_blocks, blocks_per_seq, sample_len):
    i = pl.program_id(0)
    rows = x_ref.shape[0]
    is_prompt = i < n_prompt_blocks
    no_prev = jnp.logical_or(i % blocks_per_seq == 0, jnp.logical_not(is_prompt))
    x = x_ref[...]
    scx[0:V7X_SUBLANES, :] = jnp.where(no_prev, 0.0, prev_ref[...])
    scx[V7X_SUBLANES:V7X_SUBLANES + rows, :] = x

    scp[0:rows, :] = p_ref[...]
    scp[rows:rows + V7X_SUBLANES, :] = jnp.zeros((V7X_SUBLANES, x.shape[1]), F32)
    r = lax.broadcasted_iota(jnp.int32, (rows, 1), 0)
    tpos = jnp.where(is_prompt, (i % blocks_per_seq) * rows + r, r % sample_len)
    acc = cb_ref[...] + x * cw_ref[ML_CONV - 1:ML_CONV, :]
    for s in range(1, ML_CONV):
        xs = scx[V7X_SUBLANES - s:V7X_SUBLANES - s + rows, :]
        ps = scp[ML_CONV - 1 - s:ML_CONV - 1 - s + rows, :]
        acc = acc + jnp.where(tpos >= s, xs, ps) * cw_ref[ML_CONV - 1 - s:ML_CONV - s, :]
    xc = _silu(acc)
    xc_ref[...] = xc
    g = jnp.zeros((2 * ML_HEADS, rows), F32)
    for t in range(ML_INNER // V7X_MXU_DIM):
        sl = slice(t * V7X_MXU_DIM, (t + 1) * V7X_MXU_DIM)
        q_t = _dot(xc[:, sl], wq_ref[t])
        k_t = _dot(xc[:, sl], wk_ref[t])
        v_t = _dot(x[:, sl], wv_ref[t])
        q_ref[:, sl] = q_t
        k_ref[:, sl] = k_t * (ML_DH ** -0.5)
        v_ref[:, sl] = v_t
        g = g + _dot_nt(wgq_ref[:, sl], q_t) + _dot_nt(wgk_ref[:, sl], k_t) + _dot_nt(wgv_ref[:, sl], v_t)
    g = g + gb_ref[...]
    gate_row = lax.broadcasted_iota(jnp.int32, g.shape, 0)
    g_ref[0] = jnp.where(gate_row < ML_HEADS, g, _log_sigmoid(g))


def _ml_pre(xmz, conv_rows, w, n_prompt_rows, seq_len, sample_len):
    t_all = xmz.shape[0]
    rb = ROW_BLOCK
    nblk = t_all // rb
    npb = n_prompt_rows // rb
    bps = seq_len // rb
    sub = rb // V7X_SUBLANES
    body = functools.partial(_ml_pre_body, n_prompt_blocks=npb, blocks_per_seq=bps, sample_len=sample_len)
    full2 = lambda shape: pl.BlockSpec(shape, lambda i: (0, 0))
    full3 = lambda shape: pl.BlockSpec(shape, lambda i: (0, 0, 0))
    nt = ML_INNER // V7X_MXU_DIM
    row_spec = pl.BlockSpec((rb, ML_INNER), lambda i: (i, 0))
    return pl.pallas_call(
        body,
        grid=(nblk,),
        in_specs=[row_spec,
                  pl.BlockSpec((V7X_SUBLANES, ML_INNER), lambda i: (jnp.maximum(i * sub - 1, 0), 0)),
                  pl.BlockSpec((rb, ML_INNER), lambda i: (jnp.maximum(i - npb + 1, 0), 0)),
                  full2((ML_CONV, ML_INNER)), full2((1, ML_INNER)),
                  full3((nt, V7X_MXU_DIM, V7X_MXU_DIM)), full3((nt, V7X_MXU_DIM, V7X_MXU_DIM)),
                  full3((nt, V7X_MXU_DIM, V7X_MXU_DIM)),
                  full2((2 * ML_HEADS, ML_INNER)), full2((2 * ML_HEADS, ML_INNER)), full2((2 * ML_HEADS, ML_INNER)),
                  full2((2 * ML_HEADS, 1))],
        out_specs=[row_spec, row_spec, row_spec, row_spec,
                   pl.BlockSpec((1, 2 * ML_HEADS, rb), lambda i: (i, 0, 0))],
        out_shape=[jax.ShapeDtypeStruct((t_all, ML_INNER), F32)] * 4
        + [jax.ShapeDtypeStruct((nblk, 2 * ML_HEADS, rb), F32)],
        scratch_shapes=[pltpu.VMEM((rb + V7X_SUBLANES, ML_INNER), F32), pltpu.VMEM((rb + V7X_SUBLANES, ML_INNER), F32)],
        compiler_params=_params(("arbitrary",), 48),
        name="ml_pre",
    )(xmz, xmz, conv_rows, w["conv_w"], w["conv_b"], w["wq"], w["wk"], w["wv"], w["wgq"], w["wgk"], w["wgv"], w["gb"])


def _ml_masks(L):
    ri = lax.broadcasted_iota(jnp.int32, (L, L), 0)
    ci = lax.broadcasted_iota(jnp.int32, (L, L), 1)
    return ri == ci, ci <= ri, ri <= ci


def _ml_chunk(q, k, v, ip, fl, c_prev, n_prev, m_prev, masks):
    L = q.shape[0]
    eye, tril, triu = masks
    f_col = jnp.sum(jnp.where(eye, fl, 0.0), axis=1, keepdims=True)
    b_col = jnp.sum(jnp.where(tril, fl, 0.0), axis=1, keepdims=True)
    b_row = jnp.sum(jnp.where(triu, f_col, 0.0), axis=0, keepdims=True)
    ib = ip - b_row
    d = jnp.where(tril, b_col + ib, -jnp.inf)
    m_inter = b_col + m_prev
    m_t = jnp.maximum(m_inter, jnp.max(d, axis=1, keepdims=True))
    a_inter = jnp.exp(m_inter - m_t)
    s = _dot_nt(q, k) * jnp.exp(d - m_t)
    num = _dot(s, v) + a_inter * _dot(q, c_prev)
    den = jnp.sum(s, axis=1, keepdims=True) + a_inter * jnp.sum(q * n_prev, axis=1, keepdims=True)
    hc = num * (1.0 / jnp.maximum(jnp.abs(den), jnp.exp(-m_t)))
    m_new = m_t[L - 1:L, :]
    b_last = b_row[:, L - 1:L]
    w_row = jnp.exp(b_last + ib - m_new)
    w_col = jnp.sum(jnp.where(eye, w_row, 0.0), axis=1, keepdims=True)
    a_end = jnp.exp(b_last + m_prev - m_new)
    kw = k * w_col
    n_new = a_end * n_prev + jnp.sum(kw, axis=0, keepdims=True)
    return hc, (a_end, kw, v), n_new, m_new


def _ml_new_c(c_prev, update):
    a_end, kw, v = update
    return a_end * c_prev + _dot_tn(kw, v)


def _ml_gate_out(hc, gn, sk, xc, z):
    return (_head_norm(hc, gn) + sk * xc) * _silu(z)


def _ml_cell_body(q_ref, k_ref, v_ref, xc_ref, z_ref, g_ref, gn_ref, sk_ref, o_ref, co_ref, no_ref, mo_ref, cs, ns, ms):
    c = pl.program_id(1)
    nc = pl.num_programs(1)

    @pl.when(c == 0)
    def _():
        cs[...] = jnp.zeros(cs.shape, F32)
        ns[...] = jnp.zeros(ns.shape, F32)
        ms[...] = jnp.zeros(ms.shape, F32)

    masks = _ml_masks(q_ref.shape[0])
    for h in range(ML_HEADS):
        sl = slice(h * ML_DH, (h + 1) * ML_DH)
        hc, update, n_new, m_new = _ml_chunk(
            q_ref[:, sl], k_ref[:, sl], v_ref[:, sl], g_ref[0, h:h + 1, :], g_ref[0, ML_HEADS + h:ML_HEADS + h + 1, :],
            cs[h], ns[h:h + 1, :], ms[h:h + 1, :], masks)
        cs[h] = _ml_new_c(cs[h], update)
        ns[h:h + 1, :] = n_new
        ms[h:h + 1, :] = m_new
        o_ref[:, sl] = _ml_gate_out(hc, gn_ref[:, sl], sk_ref[:, sl], xc_ref[:, sl], z_ref[:, sl])

    @pl.when(c == nc - 1)
    def _():
        co_ref[0] = cs[...]
        no_ref[0] = ns[...]
        mo_ref[0] = ms[...]


def _ml_cell(q, k, v, xc, xmz, gates, gn, skip, n_seq, seq_len, chunk):
    t_all = n_seq * seq_len
    nc = seq_len // chunk
    row_spec = pl.BlockSpec((chunk, ML_INNER), lambda b, c: (b * nc + c, 0))
    full = pl.BlockSpec((1, ML_INNER), lambda b, c: (0, 0))
    out = pl.pallas_call(
        _ml_cell_body,
        grid=(n_seq, nc),
        in_specs=[row_spec, row_spec, row_spec, row_spec,
                  pl.BlockSpec((chunk, ML_INNER), lambda b, c: (b * nc + c, 1)),
                  pl.BlockSpec((1, 2 * ML_HEADS, chunk), lambda b, c: (b * nc + c, 0, 0)), full, full],
        out_specs=[row_spec,
                   pl.BlockSpec((1, ML_HEADS, ML_DH, ML_DH), lambda b, c: (b, 0, 0, 0)),
                   pl.BlockSpec((1, ML_HEADS, ML_DH), lambda b, c: (b, 0, 0)),
                   pl.BlockSpec((1, ML_HEADS, 1), lambda b, c: (b, 0, 0))],
        out_shape=[jax.ShapeDtypeStruct((t_all, ML_INNER), F32),
                   jax.ShapeDtypeStruct((n_seq, ML_HEADS, ML_DH, ML_DH), F32),
                   jax.ShapeDtypeStruct((n_seq, ML_HEADS, ML_DH), F32),
                   jax.ShapeDtypeStruct((n_seq, ML_HEADS, 1), F32)],
        scratch_shapes=[pltpu.VMEM((ML_HEADS, ML_DH, ML_DH), F32), pltpu.VMEM((ML_HEADS, ML_DH), F32),
                        pltpu.VMEM((ML_HEADS, 1), F32)],
        compiler_params=_params(("arbitrary", "arbitrary"), 56),
        name="ml_cell",
    )(q, k, v, xc, xmz, gates, gn, skip)
    gated, c_new, n_new, m_new = out
    return gated, c_new, n_new, m_new.reshape(n_seq, ML_HEADS)


def _ml_cell_sample_body(*refs):
    q_ref, k_ref, v_ref, xc_ref, z_ref, g_ref, gn_ref, sk_ref, c0_ref, n0_ref, m0_ref = refs[:11]
    o_ref, co_ref, no_ref, mo_ref = refs[-4:]
    seq_len = g_ref.shape[2]
    masks = _ml_masks(seq_len)
    pending = []
    for j in range(g_ref.shape[0]):
        rows = slice(j * seq_len, (j + 1) * seq_len)
        for h in range(ML_HEADS):
            sl = slice(h * ML_DH, (h + 1) * ML_DH)
            hc, update, n_new, m_new = _ml_chunk(
                q_ref[rows, sl], k_ref[rows, sl], v_ref[rows, sl], g_ref[j, h:h + 1, :],
                g_ref[j, ML_HEADS + h:ML_HEADS + h + 1, :], c0_ref[j, h], n0_ref[j, h:h + 1, :], m0_ref[j, h:h + 1, :],
                masks)
            out = _ml_gate_out(hc, gn_ref[:, sl], sk_ref[:, sl], xc_ref[rows, sl], z_ref[rows, sl])
            pending.append((j, h, rows, sl, out, update, n_new, m_new))
    for j, h, rows, sl, out, update, n_new, m_new in pending:
        o_ref[rows, sl] = out
        no_ref[j, h:h + 1, :] = n_new
        mo_ref[j, h:h + 1, :] = m_new
        co_ref[j, h] = _ml_new_c(c0_ref[j, h], update)


def _ml_cell_sample(q, k, v, xc, xmz, gates, gn, skip, n_seq, seq_len, row0, c_all, n_all, m_all, layer, n_layers,
                    c_out_prev):
    ns = ML_SAMPLE_SEQS
    rows = ns * seq_len
    blk0 = row0 // rows
    s0 = layer * n_seq // ns
    row_spec = pl.BlockSpec((rows, ML_INNER), lambda b: (blk0 + b, 0))
    full = pl.BlockSpec((1, ML_INNER), lambda b: (0, 0))
    c_spec = pl.BlockSpec((ns, ML_HEADS, ML_DH, ML_DH), lambda b: (s0 + b, 0, 0, 0))
    in_specs = [row_spec, row_spec, row_spec, row_spec,
                pl.BlockSpec((rows, ML_INNER), lambda b: (blk0 + b, 1)),
                pl.BlockSpec((ns, 2 * ML_HEADS, seq_len), lambda b: (b, 0, 0)), full, full,
                c_spec,
                pl.BlockSpec((ns, ML_HEADS, ML_DH), lambda b: (s0 + b, 0, 0)),
                pl.BlockSpec((ns, ML_HEADS, 1), lambda b: (s0 + b, 0, 0))]
    args = [q, k, v, xc, xmz, gates, gn, skip, c_all, n_all, m_all]
    aliases = {}
    if c_out_prev is not None:
        in_specs.append(pl.BlockSpec(memory_space=pl.ANY))
        args.append(c_out_prev)
        aliases[len(args) - 1] = 1
    return pl.pallas_call(
        _ml_cell_sample_body,
        grid=(n_seq // ns,),
        in_specs=in_specs,
        out_specs=[pl.BlockSpec((rows, ML_INNER), lambda b: (b, 0)), c_spec,
                   pl.BlockSpec((ns, ML_HEADS, ML_DH), lambda b: (b, 0, 0)),
                   pl.BlockSpec((ns, ML_HEADS, 1), lambda b: (b, 0, 0))],
        out_shape=[jax.ShapeDtypeStruct((n_seq * seq_len, ML_INNER), F32),
                   jax.ShapeDtypeStruct((n_layers * n_seq, ML_HEADS, ML_DH, ML_DH), F32),
                   jax.ShapeDtypeStruct((n_seq, ML_HEADS, ML_DH), F32),
                   jax.ShapeDtypeStruct((n_seq, ML_HEADS, 1), F32)],
        input_output_aliases=aliases,
        compiler_params=_params(("arbitrary",), 56),
        name="ml_cell_sample",
    )(*args)


def _ml_weights(w_in, conv_w, conv_b, wq, wk, wv, w_i, b_i, w_f, b_f, gn_g, skip, w_out):
    nt = ML_INNER // V7X_MXU_DIM
    per = V7X_MXU_DIM // ML_QKV_BLOCK
    eye = jnp.eye(per, dtype=F32)

    def block_diag(w):
        wt = w.reshape(nt, per, ML_QKV_BLOCK, ML_QKV_BLOCK)
        return jnp.einsum("tncd,nm->tncmd", wt, eye).reshape(nt, V7X_MXU_DIM, V7X_MXU_DIM).astype(BF16)

    wg = jnp.concatenate([w_i, w_f], axis=1).T.astype(BF16)
    return dict(w_in=w_in.astype(BF16), conv_w=conv_w, conv_b=conv_b.reshape(1, ML_INNER),
                wq=block_diag(wq), wk=block_diag(wk), wv=block_diag(wv),
                wgq=wg[:, :ML_INNER], wgk=wg[:, ML_INNER:2 * ML_INNER], wgv=wg[:, 2 * ML_INNER:],
                gb=jnp.concatenate([b_i, b_f]).reshape(2 * ML_HEADS, 1),
                gn=gn_g.reshape(1, ML_INNER), skip=skip.reshape(1, ML_INNER), w_out=w_out.astype(BF16))


def _ml_layer(x_all, w, dims, c_all, n_all, m_all, state_conv, layer, n_layers, c_out_prev, ln_g, ln_b):
    n_p, seq_len, n_s, sample_len = dims
    t_p = n_p * seq_len
    rb = ROW_BLOCK
    xmz = _mm(x_all, w["w_in"], 2048)
    pad = jnp.pad(state_conv, ((0, 0), (0, sample_len - (ML_CONV - 1)), (0, 0))).reshape(n_s * sample_len, ML_INNER)
    conv_rows = jnp.concatenate([jnp.zeros((rb, ML_INNER), F32), pad], axis=0)
    q, k, v, xc, g3 = _ml_pre(xmz, conv_rows, w, t_p, seq_len, sample_len)
    npb = t_p // rb
    g_p = g3[:npb]
    if ML_CHUNK != rb:
        g_p = g_p.reshape(npb, 2 * ML_HEADS, rb // ML_CHUNK, ML_CHUNK).transpose(0, 2, 1, 3).reshape(-1, 2 * ML_HEADS, ML_CHUNK)
    g_s = g3[npb:].reshape(-1, 2 * ML_HEADS, rb // sample_len, sample_len).transpose(0, 2, 1, 3)
    g_s = g_s.reshape(n_s, 2 * ML_HEADS, sample_len)
    gated_p, pc, pn, pm = _ml_cell(q, k, v, xc, xmz, g_p, w["gn"], w["skip"], n_p, seq_len, ML_CHUNK)
    gated_s, c_out, sn, sm = _ml_cell_sample(q, k, v, xc, xmz, g_s, w["gn"], w["skip"], n_s, sample_len, t_p,
                                             c_all, n_all, m_all, layer, n_layers, c_out_prev)
    h1 = _mm_ln(gated_p, gated_s, w["w_out"], x_all, ln_g, ln_b)
    tail = ML_CONV - 1
    p_conv = jnp.stack([xmz[(b + 1) * seq_len - tail:(b + 1) * seq_len, :ML_INNER] for b in range(n_p)])
    s_conv = xmz[t_p:, :ML_INNER].reshape(n_s, sample_len, ML_INNER)[:, sample_len - tail:]
    return h1, (pc, pn, pm, p_conv), (sn, sm.reshape(n_s, ML_HEADS), s_conv), c_out


def _gla_in_body(x_ref, w_ref, wa_ref, wa2_ref, ba_ref, o_ref, la_ref):
    x = x_ref[...].astype(BF16)
    o_ref[...] = jnp.dot(x, w_ref[...], preferred_element_type=F32)
    a = jnp.dot(x, wa_ref[...], preferred_element_type=F32)
    la_ref[...] = _log_sigmoid(_dot(a, wa2_ref[...]) + ba_ref[...]) / GLA_TAU


def _gla_in(x, w):
    m, k = x.shape
    tm = MM_ROWS
    n1 = 2 * GLA_KT + 2 * GLA_VT
    full = lambda shape: pl.BlockSpec(shape, lambda i: (0, 0))
    return pl.pallas_call(
        _gla_in_body,
        grid=(m // tm,),
        in_specs=[pl.BlockSpec((tm, k), lambda i: (i, 0)), full((k, n1)), full((k, V7X_LANES)),
                  full((V7X_LANES, GLA_KT)), full((1, GLA_KT))],
        out_specs=[pl.BlockSpec((tm, n1), lambda i: (i, 0)), pl.BlockSpec((tm, GLA_KT), lambda i: (i, 0))],
        out_shape=[jax.ShapeDtypeStruct((m, n1), F32), jax.ShapeDtypeStruct((m, GLA_KT), F32)],
        compiler_params=_params(("arbitrary",), 48),
        name="gla_in",
    )(x, w["w_qkvr"], w["w_a"], w["w_a2"], w["b_a"])


def _gla_levels(chunk):
    t = jnp.arange(chunk)[:, None]
    s = jnp.arange(chunk)[None, :]
    mats = [(s <= t)]
    size = chunk
    while size >= 2:
        ref = t - t % size + size // 2 - 1
        mats.append(s <= ref)
        size //= 2
    levels = jnp.concatenate(mats, axis=0).astype(BF16)
    return jnp.pad(levels, ((0, 0), (0, max(V7X_LANES - chunk, 0))))


def _gla_chunk(q, k, v, la, lv, st):
    L = q.shape[0]
    kpad = lv.shape[1]
    parts = jnp.concatenate(_split3(la), axis=1)
    if kpad > L:
        parts = jnp.concatenate([parts, jnp.zeros((kpad - L, parts.shape[1]), BF16)], axis=0)
    cum = jnp.dot(lv, parts, preferred_element_type=F32)
    cum = cum[:, :GLA_DK] + cum[:, GLA_DK:2 * GLA_DK] + cum[:, 2 * GLA_DK:]
    b = cum[0:L]
    ri = lax.broadcasted_iota(jnp.int32, (L, L), 0)
    ci = lax.broadcasted_iota(jnp.int32, (L, L), 1)
    tpos = lax.broadcasted_iota(jnp.int32, (L, 1), 0)
    att = jnp.where(ri == ci, jnp.sum(q * k, axis=1, keepdims=True), 0.0)
    size = L
    lvl = 1
    while size >= 2:
        bref = cum[lvl * L:(lvl + 1) * L]
        upper = (tpos % size) >= (size // 2)
        qs = q * jnp.exp(jnp.where(upper, b - bref, -jnp.inf))
        ks = k * jnp.exp(jnp.where(upper, -jnp.inf, bref - b))
        att = att + jnp.where((ri // size) == (ci // size), _dot_nt(qs, ks), 0.0)
        size //= 2
        lvl += 1
    o = _dot(att, v) + _dot(q * jnp.exp(b), st)
    b_end = b[L - 1:L, :]
    e_end = jnp.exp(b_end)
    rk = lax.broadcasted_iota(jnp.int32, (GLA_DK, GLA_DK), 0)
    ck = lax.broadcasted_iota(jnp.int32, (GLA_DK, GLA_DK), 1)
    e_col = jnp.sum(jnp.where(rk == ck, e_end, 0.0), axis=1, keepdims=True)
    st_new = e_col * st + _dot_tn(k * jnp.exp(b_end - b), v)
    return o, st_new


def _gla_heads(x_ref, la_ref, lv_ref, gn_ref, o_ref, read_state, write_state, rows=slice(None), defer_stores=False):
    lv = lv_ref[...]
    stores = []
    for h in range(GLA_HEADS):
        qs = slice(h * GLA_DK, (h + 1) * GLA_DK)
        ks = slice(GLA_KT + h * GLA_DK, GLA_KT + (h + 1) * GLA_DK)
        vs = slice(2 * GLA_KT + h * GLA_DV, 2 * GLA_KT + (h + 1) * GLA_DV)
        rs = slice(2 * GLA_KT + GLA_VT + h * GLA_DV, 2 * GLA_KT + GLA_VT + (h + 1) * GLA_DV)
        os_ = slice(h * GLA_DV, (h + 1) * GLA_DV)
        o, st_new = _gla_chunk(x_ref[rows, qs] * (GLA_DK ** -0.5), x_ref[rows, ks], x_ref[rows, vs], la_ref[rows, qs],
                               lv, read_state(h))
        out = _silu(x_ref[rows, rs]) * _head_norm(o, gn_ref[:, os_])

        def store(h=h, os_=os_, out=out, st_new=st_new):
            o_ref[rows, os_] = out
            write_state(h, st_new)

        if defer_stores:
            stores.append(store)
        else:
            store()
    return stores


def _gla_cell_body(x_ref, la_ref, lv_ref, gn_ref, o_ref, so_ref, ss):
    c = pl.program_id(1)
    nc = pl.num_programs(1)

    @pl.when(c == 0)
    def _():
        ss[...] = jnp.zeros(ss.shape, F32)

    def write(h, st):
        ss[h] = st

    _gla_heads(x_ref, la_ref, lv_ref, gn_ref, o_ref, lambda h: ss[h], write)

    @pl.when(c == nc - 1)
    def _():
        so_ref[0] = ss[...]


def _gla_cell(qkvr, la, gn, n_seq, seq_len, chunk):
    n1 = qkvr.shape[1]
    nc = seq_len // chunk
    levels = _gla_levels(chunk)
    return pl.pallas_call(
        _gla_cell_body,
        grid=(n_seq, nc),
        in_specs=[pl.BlockSpec((chunk, n1), lambda b, c: (b * nc + c, 0)),
                  pl.BlockSpec((chunk, GLA_KT), lambda b, c: (b * nc + c, 0)),
                  pl.BlockSpec(levels.shape, lambda b, c: (0, 0)),
                  pl.BlockSpec((1, GLA_VT), lambda b, c: (0, 0))],
        out_specs=[pl.BlockSpec((chunk, GLA_VT), lambda b, c: (b * nc + c, 0)),
                   pl.BlockSpec((1, GLA_HEADS, GLA_DK, GLA_DV), lambda b, c: (b, 0, 0, 0))],
        out_shape=[jax.ShapeDtypeStruct((n_seq * seq_len, GLA_VT), F32),
                   jax.ShapeDtypeStruct((n_seq, GLA_HEADS, GLA_DK, GLA_DV), F32)],
        scratch_shapes=[pltpu.VMEM((GLA_HEADS, GLA_DK, GLA_DV), F32)],
        compiler_params=_params(("arbitrary", "arbitrary"), 32),
        name="gla_cell",
    )(qkvr, la, levels, gn)


def _gla_cell_sample_body(x_ref, la_ref, lv_ref, gn_ref, s0_ref, o_ref, so_ref):
    seq_len = x_ref.shape[0] // s0_ref.shape[0]
    stores = []
    for j in range(s0_ref.shape[0]):

        def write(h, st, j=j):
            so_ref[j, h] = st

        stores += _gla_heads(x_ref, la_ref, lv_ref, gn_ref, o_ref, lambda h, j=j: s0_ref[j, h], write,
                             rows=slice(j * seq_len, (j + 1) * seq_len), defer_stores=True)
    for store in stores:
        store()


def _gla_cell_sample(qkvr, la, gn, n_seq, seq_len, row0, state):
    n1 = qkvr.shape[1]
    ns = GLA_SAMPLE_SEQS
    rows = ns * seq_len
    blk0 = row0 // rows
    levels = _gla_levels(seq_len)
    st_spec = pl.BlockSpec((ns, GLA_HEADS, GLA_DK, GLA_DV), lambda b: (b, 0, 0, 0))
    return pl.pallas_call(
        _gla_cell_sample_body,
        grid=(n_seq // ns,),
        in_specs=[pl.BlockSpec((rows, n1), lambda b: (blk0 + b, 0)),
                  pl.BlockSpec((rows, GLA_KT), lambda b: (blk0 + b, 0)),
                  pl.BlockSpec(levels.shape, lambda b: (0, 0)),
                  pl.BlockSpec((1, GLA_VT), lambda b: (0, 0)),
                  st_spec],
        out_specs=[pl.BlockSpec((rows, GLA_VT), lambda b: (b, 0)), st_spec],
        out_shape=[jax.ShapeDtypeStruct((n_seq * seq_len, GLA_VT), F32),
                   jax.ShapeDtypeStruct((n_seq, GLA_HEADS, GLA_DK, GLA_DV), F32)],
        compiler_params=_params(("arbitrary",), 32),
        name="gla_cell_sample",
    )(qkvr, la, levels, gn, state)


def _gla_weights(w_in, w_a2, b_a, gn_g, w_out):
    n1 = 2 * GLA_KT + 2 * GLA_VT
    w_a = jnp.pad(w_in[:, n1:], ((0, 0), (0, V7X_LANES - GLA_RANK)))
    w_a2p = jnp.pad(w_a2, ((0, V7X_LANES - GLA_RANK), (0, 0)))
    return dict(w_qkvr=w_in[:, :n1].astype(BF16), w_a=w_a.astype(BF16), w_a2=w_a2p.astype(BF16),
                b_a=b_a.reshape(1, GLA_KT), gn=gn_g.reshape(1, GLA_VT), w_out=w_out.astype(BF16))


def _gla_layer(x_all, w, dims, state_s, ln_g, ln_b):
    n_p, seq_len, n_s, sample_len = dims
    t_p = n_p * seq_len
    qkvr, la = _gla_in(x_all, w)
    gated_p, p_s = _gla_cell(qkvr, la, w["gn"], n_p, seq_len, min(GLA_CHUNK, seq_len))
    gated_s, s_s = _gla_cell_sample(qkvr, la, w["gn"], n_s, sample_len, t_p, state_s)
    h1 = _mm_ln(gated_p, gated_s, w["w_out"], x_all, ln_g, ln_b)
    return h1, p_s, s_s


def _s5_body(x_ref, sre_ref, sim_ref, win_ref, bt_ref, cre_ref, cim_ref, lre_ref, lim_ref, d_ref, wglu_ref,
             g_ref, b_ref, o_ref, ore_ref, oim_ref, st_re, st_im, car_re, car_im, *, nb, lc):
    c = pl.program_id(1)
    nc = pl.num_programs(1)
    rows = nb * lc
    ntile = D_MODEL // S5_TILE_CH

    @pl.when(c == 0)
    def _():
        car_re[...] = sre_ref[...]
        car_im[...] = sim_ref[...]

    x = x_ref[...].reshape(rows, D_MODEL)
    u = _dot(x, win_ref[...])
    lpt = S5_TILE_ST // V7X_LANES
    for t in range(ntile):
        bu = _dot(u[:, t * S5_TILE_CH:(t + 1) * S5_TILE_CH], bt_ref[t])
        for j in range(lpt):
            st_re[t * lpt + j] = bu[:, j * V7X_LANES:(j + 1) * V7X_LANES]
            st_im[t * lpt + j] = bu[:, S5_TILE_ST + j * V7X_LANES:S5_TILE_ST + (j + 1) * V7X_LANES]

    for gidx in range(nb // V7X_SUBLANES):
        grp = slice(gidx * V7X_SUBLANES, (gidx + 1) * V7X_SUBLANES)
        for t in range(ntile):
            slabs = list(range(t * lpt, (t + 1) * lpt))
            lanes = [slice(j * V7X_LANES, (j + 1) * V7X_LANES) for j in slabs]
            lre = [lre_ref[:, ln] for ln in lanes]
            lim = [lim_ref[:, ln] for ln in lanes]

            def step(tok, carry, slabs=slabs, lre=lre, lim=lim, gidx=gidx):
                sel = pl.ds(pl.multiple_of(tok * nb + gidx * V7X_SUBLANES, V7X_SUBLANES), V7X_SUBLANES)
                out = []
                for n, j in enumerate(slabs):
                    pr, pi = carry[2 * n], carry[2 * n + 1]
                    nr = lre[n] * pr - lim[n] * pi + st_re[j, sel, :]
                    ni = lre[n] * pi + lim[n] * pr + st_im[j, sel, :]
                    st_re[j, sel, :] = nr
                    st_im[j, sel, :] = ni
                    out += [nr, ni]
                return tuple(out)

            init = []
            for ln in lanes:
                init += [car_re[grp, ln], car_im[grp, ln]]
            fin = lax.fori_loop(0, lc, step, tuple(init))
            for n, ln in enumerate(lanes):
                car_re[grp, ln] = fin[2 * n]
                car_im[grp, ln] = fin[2 * n + 1]

    ys = []
    for t in range(ntile):
        sre = jnp.concatenate([st_re[t * lpt + j] for j in range(lpt)], axis=1)
        sim = jnp.concatenate([st_im[t * lpt + j] for j in range(lpt)], axis=1)
        ys.append(_dot(sre, cre_ref[t]) - _dot(sim, cim_ref[t]))
    y = jnp.concatenate(ys, axis=1) + d_ref[...] * u
    y = jax.nn.gelu(y)
    vg = _dot(y, wglu_ref[...])
    mix = vg[:, :D_MODEL] * _sigmoid(vg[:, D_MODEL:])
    o_ref[...] = _layer_norm(DEEPNORM_ALPHA * x + mix, g_ref[...], b_ref[...]).reshape(lc, nb, D_MODEL)

    @pl.when(c == nc - 1)
    def _():
        ore_ref[...] = car_re[...]
        oim_ref[...] = car_im[...]


def _s5_call(x3, s_re, s_im, w, ln_g, ln_b, nb, lc):
    seq_len, n_seq, _ = x3.shape
    ntile = D_MODEL // S5_TILE_CH
    rows = nb * lc
    full2 = lambda shape: pl.BlockSpec(shape, lambda i, c: (0, 0), pipeline_mode=pl.Buffered(1))
    full3 = lambda shape: pl.BlockSpec(shape, lambda i, c: (0, 0, 0), pipeline_mode=pl.Buffered(1))
    st_spec = pl.BlockSpec((nb, S5_STATE), lambda i, c: (i, 0))
    x_spec = pl.BlockSpec((lc, nb, D_MODEL), lambda i, c: (c, i, 0))
    return pl.pallas_call(
        functools.partial(_s5_body, nb=nb, lc=lc),
        grid=(n_seq // nb, seq_len // lc),
        in_specs=[x_spec, st_spec, st_spec,
                  full2((D_MODEL, D_MODEL)), full3((ntile, S5_TILE_CH, 2 * S5_TILE_ST)),
                  full3((ntile, S5_TILE_ST, S5_TILE_CH)), full3((ntile, S5_TILE_ST, S5_TILE_CH)),
                  full2((V7X_SUBLANES, S5_STATE)), full2((V7X_SUBLANES, S5_STATE)), full2((1, D_MODEL)),
                  full2((D_MODEL, 2 * D_MODEL)), full2((1, D_MODEL)), full2((1, D_MODEL))],
        out_specs=[x_spec, st_spec, st_spec],
        out_shape=[jax.ShapeDtypeStruct(x3.shape, F32), jax.ShapeDtypeStruct((n_seq, S5_STATE), F32),
                   jax.ShapeDtypeStruct((n_seq, S5_STATE), F32)],
        scratch_shapes=[pltpu.VMEM((S5_STATE // V7X_LANES, rows, V7X_LANES), F32),
                        pltpu.VMEM((S5_STATE // V7X_LANES, rows, V7X_LANES), F32),
                        pltpu.VMEM((nb, S5_STATE), F32), pltpu.VMEM((nb, S5_STATE), F32)],
        compiler_params=_params(("arbitrary", "arbitrary"), 56),
        name="s5",
    )(x3, s_re, s_im, w["w_in"], w["b_tiles"], w["c_re"], w["c_im"], w["lam_re"], w["lam_im"], w["d"], w["w_glu"],
      ln_g.reshape(1, D_MODEL), ln_b.reshape(1, D_MODEL))


def _s5_weights(w_in, a_re, a_im, log_dt, b_re, b_im, c_re, c_im, d_skip, w_glu):
    lam = lax.complex(a_re.astype(F32), a_im.astype(F32))
    dt = jnp.exp(log_dt.astype(F32))[:, None]
    lam_bar = jnp.exp(lam * dt)
    b_bar = ((lam_bar - 1.0) / lam)[..., None] * lax.complex(b_re.astype(F32), b_im.astype(F32))
    ntile = D_MODEL // S5_TILE_CH
    gpt = S5_TILE_CH // S5_GC
    eye = jnp.eye(gpt, dtype=F32)

    def b_tiles(bb):
        return jnp.einsum("igpc,gh->igchp", bb.reshape(ntile, gpt, S5_P, S5_GC), eye).reshape(ntile, S5_TILE_CH, S5_TILE_ST)

    def c_tiles(cc):
        return jnp.einsum("igcp,gh->igphc", cc.reshape(ntile, gpt, S5_GC, S5_P), eye).reshape(ntile, S5_TILE_ST, S5_TILE_CH)

    bt = jnp.concatenate([b_tiles(b_bar.real), b_tiles(b_bar.imag)], axis=2).astype(BF16)
    bcast = lambda a: jnp.broadcast_to(a.reshape(1, S5_STATE), (V7X_SUBLANES, S5_STATE))
    return dict(w_in=w_in.astype(BF16), b_tiles=bt, c_re=c_tiles(c_re.astype(F32)).astype(BF16),
                c_im=c_tiles(c_im.astype(F32)).astype(BF16), lam_re=bcast(lam_bar.real), lam_im=bcast(lam_bar.imag),
                d=d_skip.reshape(1, D_MODEL).astype(F32), w_glu=w_glu.astype(BF16))


def _s5_layer(x_all, w, dims, state_re, state_im, ln_g, ln_b):
    n_p, seq_len, n_s, sample_len = dims
    t_p = n_p * seq_len
    xp = x_all[:t_p].reshape(n_p, seq_len, D_MODEL).transpose(1, 0, 2)
    xs = x_all[t_p:].reshape(n_s, sample_len, D_MODEL).transpose(1, 0, 2)
    zero = jnp.zeros((n_p, S5_STATE), F32)
    hp, p_re, p_im = _s5_call(xp, zero, zero, w, ln_g, ln_b, n_p, min(S5_CHUNK, seq_len))
    nb_s = min(n_s, ROW_BLOCK // sample_len)
    hs, s_re, s_im = _s5_call(xs, state_re.reshape(n_s, S5_STATE), state_im.reshape(n_s, S5_STATE), w, ln_g, ln_b,
                              nb_s, sample_len)
    h1 = jnp.concatenate([hp.transpose(1, 0, 2).reshape(t_p, D_MODEL),
                          hs.transpose(1, 0, 2).reshape(n_s * sample_len, D_MODEL)], axis=0)
    shp = lambda a, n: a.reshape(n, S5_GROUPS, S5_P)
    return h1, (shp(p_re, n_p), shp(p_im, n_p)), (shp(s_re, n_s), shp(s_im, n_s))


def _route_body(x_ref, wh_ref, wl_ref, br_ref, g_ref):
    x = x_ref[...]
    xh = x.astype(BF16)
    xl = (x - xh.astype(F32)).astype(BF16)
    wh = wh_ref[...]
    logits = _dot_nt(wh, xh) + _dot_nt(wh, xl) + _dot_nt(wl_ref[...], xh)
    s = _sigmoid(logits)
    work = s + br_ref[...]
    row = lax.broadcasted_iota(jnp.int32, s.shape, 0).astype(F32)
    chosen = jnp.zeros(s.shape, jnp.bool_)
    for _ in range(TOP_K):
        mx = jnp.max(work, axis=0, keepdims=True)
        idx = jnp.min(jnp.where(work == mx, row, float(N_EXPERTS)), axis=0, keepdims=True)
        hit = row == idx
        chosen = jnp.logical_or(chosen, hit)
        work = jnp.where(hit, -jnp.inf, work)
    sel = jnp.where(chosen, s, 0.0)
    g_ref[...] = sel / jnp.sum(sel, axis=0, keepdims=True) * ROUTE_SCALE


def _route(x, w):
    m, k = x.shape
    tm = ROW_BLOCK
    full = lambda shape: pl.BlockSpec(shape, lambda i: (0, 0))
    gates_t = pl.pallas_call(
        _route_body,
        grid=(m // tm,),
        in_specs=[pl.BlockSpec((tm, k), lambda i: (i, 0)), full((N_EXPERTS, k)), full((N_EXPERTS, k)),
                  full((N_EXPERTS, 1))],
        out_specs=pl.BlockSpec((N_EXPERTS, tm), lambda i: (0, i)),
        out_shape=jax.ShapeDtypeStruct((N_EXPERTS, m), F32),
        compiler_params=_params(("arbitrary",), 32),
        name="route",
    )(x, w["wr_hi"], w["wr_lo"], w["b_router"])
    return gates_t.T


def _moe_body(x_ref, gd_ref, wg_ref, wu_ref, wd_ref, sg_ref, su_ref, sd_ref, g_ref, b_ref, o_ref, acc, xb_ref):
    e = pl.program_id(1)
    ne = pl.num_programs(1)

    @pl.when(e == 0)
    def _():
        xb0 = x_ref[...].astype(BF16)
        xb_ref[...] = xb0
        hs = _silu(jnp.dot(xb0, sg_ref[...], preferred_element_type=F32)) * jnp.dot(xb0, su_ref[...], preferred_element_type=F32)
        acc[...] = _dot(hs, sd_ref[...])

    xb = xb_ref[...]
    gd = gd_ref[...]
    lane = lax.broadcasted_iota(jnp.int32, gd.shape, 1)
    total = None
    for j in range(MOE_EXPERTS_PER_STEP):
        h = _silu(jnp.dot(xb, wg_ref[j], preferred_element_type=F32)) * jnp.dot(xb, wu_ref[j], preferred_element_type=F32)
        gate = jnp.sum(jnp.where(lane == e * MOE_EXPERTS_PER_STEP + j, gd, 0.0), axis=1, keepdims=True)
        y = _dot(h, wd_ref[j]) * gate
        total = y if total is None else total + y
    acc[...] += total

    @pl.when(e == ne - 1)
    def _():
        o_ref[...] = _layer_norm(DEEPNORM_ALPHA * x_ref[...] + acc[...], g_ref[...], b_ref[...])


def _moe_ffn(x, gates, w, experts, layer, ln_g, ln_b):
    m, k = x.shape
    tm = MOE_ROWS
    steps = N_EXPERTS // MOE_EXPERTS_PER_STEP
    full = lambda shape: pl.BlockSpec(shape, lambda i, e: (0, 0))
    expert_map = lambda i, e: (layer * steps + e, 0, 0)
    return pl.pallas_call(
        _moe_body,
        grid=(m // tm, steps),
        in_specs=[pl.BlockSpec((tm, k), lambda i, e: (i, 0)), pl.BlockSpec((tm, N_EXPERTS), lambda i, e: (i, 0)),
                  pl.BlockSpec((MOE_EXPERTS_PER_STEP, k, EXPERT_FF), expert_map),
                  pl.BlockSpec((MOE_EXPERTS_PER_STEP, k, EXPERT_FF), expert_map),
                  pl.BlockSpec((MOE_EXPERTS_PER_STEP, EXPERT_FF, k), expert_map),
                  full((k, w["ws_gate"].shape[1])), full((k, w["ws_up"].shape[1])), full((w["ws_down"].shape[0], k)),
                  full((1, k)), full((1, k))],
        out_specs=pl.BlockSpec((tm, k), lambda i, e: (i, 0)),
        out_shape=jax.ShapeDtypeStruct((m, k), F32),
        scratch_shapes=[pltpu.VMEM((tm, k), F32), pltpu.VMEM((tm, k), BF16)],
        compiler_params=_params(("arbitrary", "arbitrary"), 48),
        name="moe_ffn",
    )(x, gates, experts[0], experts[1], experts[2], w["ws_gate"], w["ws_up"], w["ws_down"],
      ln_g.reshape(1, k), ln_b.reshape(1, k))


def _moe_weights(w_router, b_router, ws_gate, ws_up, ws_down):
    wr_t = w_router.T
    wr_hi = wr_t.astype(BF16)
    wr_lo = (wr_t - wr_hi.astype(F32)).astype(BF16)
    return dict(wr_hi=wr_hi, wr_lo=wr_lo, b_router=b_router.reshape(N_EXPERTS, 1).astype(F32),
                ws_gate=ws_gate.astype(BF16), ws_up=ws_up.astype(BF16), ws_down=ws_down.astype(BF16))


def _moe_layer(x, w, experts, layer, ln_g, ln_b):
    return _moe_ffn(x, _route(x, w), w, experts, layer, ln_g, ln_b)


def kernel(x_prompt, x_sample, state_mlstm_C, state_mlstm_n, state_mlstm_m, state_mlstm_conv, state_gla_S, state_s5_re, state_s5_im, ln1_g, ln1_b, ln2_g, ln2_b, ml_w_in, ml_conv_w, ml_conv_b, ml_wq, ml_wk, ml_wv, ml_w_i, ml_b_i, ml_w_f, ml_b_f, ml_gn_g, ml_skip, ml_w_out, gla_w_in, gla_w_a2, gla_b_a, gla_gn_g, gla_w_out, s5_w_in, s5_a_re, s5_a_im, s5_log_dt, s5_b_re, s5_b_im, s5_c_re, s5_c_im, s5_d, s5_w_glu, moe_w_router, moe_b_router, moe_w_gate, moe_w_up, moe_w_down, moe_ws_gate, moe_ws_up, moe_ws_down):
    n_p, seq_len, _ = x_prompt.shape
    n_s, sample_len, _ = x_sample.shape
    dims = (n_p, seq_len, n_s, sample_len)
    t_p = n_p * seq_len
    t_s = n_s * sample_len
    assert seq_len % MM_ROWS == 0 and t_s % MM_ROWS == 0 and MM_ROWS % ROW_BLOCK == 0 and ROW_BLOCK % sample_len == 0
    assert sample_len == V7X_SUBLANES and n_p % V7X_SUBLANES == 0 and (t_p + t_s) % MOE_ROWS == 0
    x = jnp.concatenate([x_prompt.reshape(t_p, D_MODEL), x_sample.reshape(t_s, D_MODEL)], axis=0)
    n_ml = len(range(0, DEPTH, N_MIXERS))
    c_all = state_mlstm_C.reshape(-1, ML_HEADS, ML_DH, ML_DH)
    n_all = state_mlstm_n.reshape(-1, ML_HEADS, ML_DH)
    m_all = state_mlstm_m.reshape(-1, ML_HEADS, 1)
    c_out = None
    flat_experts = lambda w: w.astype(BF16).reshape((-1,) + w.shape[2:])
    experts = (flat_experts(moe_w_gate), flat_experts(moe_w_up), flat_experts(moe_w_down))
    p_ml, s_ml, p_gla, s_gla, p_s5, s_s5 = [], [], [], [], [], []
    for i in range(DEPTH):
        j = i // N_MIXERS
        if i % N_MIXERS == 0:
            w = _ml_weights(ml_w_in[j], ml_conv_w[j], ml_conv_b[j], ml_wq[j], ml_wk[j], ml_wv[j], ml_w_i[j], ml_b_i[j],
                            ml_w_f[j], ml_b_f[j], ml_gn_g[j], ml_skip[j], ml_w_out[j])
            x, ps, ss, c_out = _ml_layer(x, w, dims, c_all, n_all, m_all, state_mlstm_conv[j], j, n_ml, c_out,
                                         ln1_g[i], ln1_b[i])
            p_ml.append(ps)
            s_ml.append(ss)
        elif i % N_MIXERS == 1:
            w = _gla_weights(gla_w_in[j], gla_w_a2[j], gla_b_a[j], gla_gn_g[j], gla_w_out[j])
            x, ps, ss = _gla_layer(x, w, dims, state_gla_S[j], ln1_g[i], ln1_b[i])
            p_gla.append(ps)
            s_gla.append(ss)
        else:
            w = _s5_weights(s5_w_in[j], s5_a_re[j], s5_a_im[j], s5_log_dt[j], s5_b_re[j], s5_b_im[j], s5_c_re[j],
                            s5_c_im[j], s5_d[j], s5_w_glu[j])
            x, ps, ss = _s5_layer(x, w, dims, state_s5_re[j], state_s5_im[j], ln1_g[i], ln1_b[i])
            p_s5.append(ps)
            s_s5.append(ss)
        wm = _moe_weights(moe_w_router[i], moe_b_router[i], moe_ws_gate[i], moe_ws_up[i], moe_ws_down[i])
        x = _moe_layer(x, wm, experts, i, ln2_g[i], ln2_b[i])
    stack = lambda items, idx: jnp.stack([it[idx] for it in items])
    return (x[:t_p].reshape(n_p, seq_len, D_MODEL), x[t_p:].reshape(n_s, sample_len, D_MODEL),
            stack(p_ml, 0), stack(p_ml, 1), stack(p_ml, 2), stack(p_ml, 3), jnp.stack(p_gla),
            stack(p_s5, 0), stack(p_s5, 1),
            c_out.reshape(n_ml, n_s, ML_HEADS, ML_DH, ML_DH), stack(s_ml, 0), stack(s_ml, 1), stack(s_ml, 2),
            jnp.stack(s_gla), stack(s_s5, 0), stack(s_s5, 1))
```

```python
import functools

import jax
import jax.numpy as jnp
from jax import lax
from jax.experimental import pallas as pl
from jax.experimental.pallas import tpu as pltpu

F32 = jnp.float32
BF16 = jnp.bfloat16

D_MODEL = 1024
DEPTH = 4
N_MIXERS = 3
ML_INNER = 2 * D_MODEL
ML_HEADS = 4
ML_DH = ML_INNER // ML_HEADS
ML_QKV_BLOCK = 4
ML_CONV = 4
GLA_HEADS = 4
GLA_KT = D_MODEL // 2
GLA_VT = D_MODEL
GLA_DK = GLA_KT // GLA_HEADS
GLA_DV = GLA_VT // GLA_HEADS
GLA_RANK = 16
GLA_TAU = 16.0
S5_GC = 16
S5_GROUPS = D_MODEL // S5_GC
S5_P = 64
S5_STATE = S5_GROUPS * S5_P
N_EXPERTS = 64
TOP_K = 8
EXPERT_FF = 256
ROUTE_SCALE = 2.5
DEEPNORM_ALPHA = (2.0 * DEPTH) ** 0.25
NORM_EPS = 1e-5

V7X_VMEM_BYTES = 64 * 1024 * 1024
V7X_LANES = 128
V7X_SUBLANES = 8
V7X_MXU_DIM = 256

ROW_BLOCK = 256
MM_ROWS = 512
ML_SAMPLE_SEQS = 2
GLA_SAMPLE_SEQS = 4
ML_CHUNK = 256
GLA_CHUNK = 128
S5_CHUNK = 64
S5_TILE_CH = 128
S5_TILE_ST = S5_TILE_CH // S5_GC * S5_P
MOE_ROWS = 1024
MOE_EXPERTS_PER_STEP = 4


def _params(semantics, vmem_mb):
    assert vmem_mb * 1024 * 1024 < V7X_VMEM_BYTES
    return pltpu.CompilerParams(dimension_semantics=semantics, vmem_limit_bytes=vmem_mb * 1024 * 1024)


def _dot(a, b):
    return jnp.dot(a.astype(BF16), b.astype(BF16), preferred_element_type=F32)


def _dot_nt(a, b):
    return lax.dot_general(a.astype(BF16), b.astype(BF16), (((1,), (1,)), ((), ())), preferred_element_type=F32)


def _dot_tn(a, b):
    return lax.dot_general(a.astype(BF16), b.astype(BF16), (((0,), (0,)), ((), ())), preferred_element_type=F32)


def _sigmoid(x):
    return 1.0 / (1.0 + jnp.exp(-x))


def _silu(x):
    return x * _sigmoid(x)


def _log_sigmoid(x):
    return jnp.minimum(x, 0.0) - jnp.log1p(jnp.exp(-jnp.abs(x)))


def _layer_norm(y, g, b):
    mu = jnp.mean(y, axis=-1, keepdims=True)
    d = y - mu
    var = jnp.mean(d * d, axis=-1, keepdims=True)
    return d * lax.rsqrt(var + NORM_EPS) * g + b


def _head_norm(h, g):
    mu = jnp.mean(h, axis=-1, keepdims=True)
    d = h - mu
    var = jnp.mean(d * d, axis=-1, keepdims=True)
    return d * lax.rsqrt(var + NORM_EPS) * g


def _split3(x):
    hi = x.astype(BF16)
    r1 = x - hi.astype(F32)
    mid = r1.astype(BF16)
    lo = (r1 - mid.astype(F32)).astype(BF16)
    return hi, mid, lo


def _mm_body(x_ref, w_ref, o_ref):
    o_ref[...] = _dot(x_ref[...], w_ref[...])


def _mm(x, w, tn):
    m, k = x.shape
    n = w.shape[1]
    tm = MM_ROWS
    return pl.pallas_call(
        _mm_body,
        grid=(n // tn, m // tm),
        in_specs=[pl.BlockSpec((tm, k), lambda j, i: (i, 0)), pl.BlockSpec((k, tn), lambda j, i: (0, j))],
        out_specs=pl.BlockSpec((tm, tn), lambda j, i: (i, j)),
        out_shape=jax.ShapeDtypeStruct((m, n), F32),
        compiler_params=_params(("arbitrary", "arbitrary"), 40),
        name="mm",
    )(x, w)


def _mm_ln_body(ap_ref, as_ref, w_ref, r_ref, g_ref, b_ref, o_ref, *, n_prompt_blocks):
    i = pl.program_id(0)

    def run(a_ref):
        y = _dot(a_ref[...], w_ref[...])
        o_ref[...] = _layer_norm(DEEPNORM_ALPHA * r_ref[...] + y, g_ref[...], b_ref[...])

    pl.when(i < n_prompt_blocks)(lambda: run(ap_ref))
    pl.when(i >= n_prompt_blocks)(lambda: run(as_ref))


def _mm_ln(a_p, a_s, w, resid, g, b):
    k = a_p.shape[1]
    m = a_p.shape[0] + a_s.shape[0]
    n = w.shape[1]
    tm = MM_ROWS
    npb = a_p.shape[0] // tm
    return pl.pallas_call(
        functools.partial(_mm_ln_body, n_prompt_blocks=npb),
        grid=(m // tm,),
        in_specs=[pl.BlockSpec((tm, k), lambda i: (jnp.minimum(i, npb - 1), 0)),
                  pl.BlockSpec((tm, k), lambda i: (jnp.maximum(i - npb, 0), 0)),
                  pl.BlockSpec((k, n), lambda i: (0, 0)),
                  pl.BlockSpec((tm, n), lambda i: (i, 0)), pl.BlockSpec((1, n), lambda i: (0, 0)),
                  pl.BlockSpec((1, n), lambda i: (0, 0))],
        out_specs=pl.BlockSpec((tm, n), lambda i: (i, 0)),
        out_shape=jax.ShapeDtypeStruct((m, n), F32),
        compiler_params=_params(("arbitrary",), 48),
        name="mm_ln",
    )(a_p, a_s, w, resid, g.reshape(1, n), b.reshape(1, n))


def _ml_pre_body(x_ref, prev_ref, p_ref, cw_ref, cb_ref, wq_ref, wk_ref, wv_ref, wgq_ref, wgk_ref, wgv_ref, gb_ref,
                 q_ref, k_ref, v_ref, xc_ref, g_ref, scx, scp, *, n_prompt_blocks, blocks_per_seq, sample_len):
    i = pl.program_id(0)
    rows = x_ref.shape[0]
    is_prompt = i < n_prompt_blocks
    no_prev = jnp.logical_or(i % blocks_per_seq == 0, jnp.logical_not(is_prompt))
    x = x_ref[...]
    scx[0:V7X_SUBLANES, :] = jnp.where(no_prev, 0.0, prev_ref[...])
    scx[V7X_SUBLANES:V7X_SUBLANES + rows, :] = x

    scp[0:rows, :] = p_ref[...]
    scp[rows:rows + V7X_SUBLANES, :] = jnp.zeros((V7X_SUBLANES, x.shape[1]), F32)
    r = lax.broadcasted_iota(jnp.int32, (rows, 1), 0)
    tpos = jnp.where(is_prompt, (i % blocks_per_seq) * rows + r, r % sample_len)
    acc = cb_ref[...] + x * cw_ref[ML_CONV - 1:ML_CONV, :]
    for s in range(1, ML_CONV):
        xs = scx[V7X_SUBLANES - s:V7X_SUBLANES - s + rows, :]
        ps = scp[ML_CONV - 1 - s:ML_CONV - 1 - s + rows, :]
        acc = acc + jnp.where(tpos >= s, xs, ps) * cw_ref[ML_CONV - 1 - s:ML_CONV - s, :]
    xc = _silu(acc)
    xc_ref[...] = xc
    g = jnp.zeros((2 * ML_HEADS, rows), F32)
    for t in range(ML_INNER // V7X_MXU_DIM):
        sl = slice(t * V7X_MXU_DIM, (t + 1) * V7X_MXU_DIM)
        q_t = _dot(xc[:, sl], wq_ref[t])
        k_t = _dot(xc[:, sl], wk_ref[t])
        v_t = _dot(x[:, sl], wv_ref[t])
        q_ref[:, sl] = q_t
        k_ref[:, sl] = k_t * (ML_DH ** -0.5)
        v_ref[:, sl] = v_t
        g = g + _dot_nt(wgq_ref[:, sl], q_t) + _dot_nt(wgk_ref[:, sl], k_t) + _dot_nt(wgv_ref[:, sl], v_t)
    g = g + gb_ref[...]
    gate_row = lax.broadcasted_iota(jnp.int32, g.shape, 0)
    g_ref[0] = jnp.where(gate_row < ML_HEADS, g, _log_sigmoid(g))


def _ml_pre(xmz, conv_rows, w, n_prompt_rows, seq_len, sample_len):
    t_all = xmz.shape[0]
    rb = ROW_BLOCK
    nblk = t_all // rb
    npb = n_prompt_rows // rb
    bps = seq_len // rb
    sub = rb // V7X_SUBLANES
    body = functools.partial(_ml_pre_body, n_prompt_blocks=npb, blocks_per_seq=bps, sample_len=sample_len)
    full2 = lambda shape: pl.BlockSpec(shape, lambda i: (0, 0))
    full3 = lambda shape: pl.BlockSpec(shape, lambda i: (0, 0, 0))
    nt = ML_INNER // V7X_MXU_DIM
    row_spec = pl.BlockSpec((rb, ML_INNER), lambda i: (i, 0))
    return pl.pallas_call(
        body,
        grid=(nblk,),
        in_specs=[row_spec,
                  pl.BlockSpec((V7X_SUBLANES, ML_INNER), lambda i: (jnp.maximum(i * sub - 1, 0), 0)),
                  pl.BlockSpec((rb, ML_INNER), lambda i: (jnp.maximum(i - npb + 1, 0), 0)),
                  full2((ML_CONV, ML_INNER)), full2((1, ML_INNER)),
                  full3((nt, V7X_MXU_DIM, V7X_MXU_DIM)), full3((nt, V7X_MXU_DIM, V7X_MXU_DIM)),
                  full3((nt, V7X_MXU_DIM, V7X_MXU_DIM)),
                  full2((2 * ML_HEADS, ML_INNER)), full2((2 * ML_HEADS, ML_INNER)), full2((2 * ML_HEADS, ML_INNER)),
                  full2((2 * ML_HEADS, 1))],
        out_specs=[row_spec, row_spec, row_spec, row_spec,
                   pl.BlockSpec((1, 2 * ML_HEADS, rb), lambda i: (i, 0, 0))],
        out_shape=[jax.ShapeDtypeStruct((t_all, ML_INNER), F32)] * 4
        + [jax.ShapeDtypeStruct((nblk, 2 * ML_HEADS, rb), F32)],
        scratch_shapes=[pltpu.VMEM((rb + V7X_SUBLANES, ML_INNER), F32), pltpu.VMEM((rb + V7X_SUBLANES, ML_INNER), F32)],
        compiler_params=_params(("arbitrary",), 48),
        name="ml_pre",
    )(xmz, xmz, conv_rows, w["conv_w"], w["conv_b"], w["wq"], w["wk"], w["wv"], w["wgq"], w["wgk"], w["wgv"], w["gb"])


def _ml_masks(L):
    ri = lax.broadcasted_iota(jnp.int32, (L, L), 0)
    ci = lax.broadcasted_iota(jnp.int32, (L, L), 1)
    return ri == ci, ci <= ri, ri <= ci


def _ml_chunk(q, k, v, ip, fl, c_prev, n_prev, m_prev, masks):
    L = q.shape[0]
    eye, tril, triu = masks
    f_col = jnp.sum(jnp.where(eye, fl, 0.0), axis=1, keepdims=True)
    b_col = jnp.sum(jnp.where(tril, fl, 0.0), axis=1, keepdims=True)
    b_row = jnp.sum(jnp.where(triu, f_col, 0.0), axis=0, keepdims=True)
    ib = ip - b_row
    d = jnp.where(tril, b_col + ib, -jnp.inf)
    m_inter = b_col + m_prev
    m_t = jnp.maximum(m_inter, jnp.max(d, axis=1, keepdims=True))
    a_inter = jnp.exp(m_inter - m_t)
    s = _dot_nt(q, k) * jnp.exp(d - m_t)
    num = _dot(s, v) + a_inter * _dot(q, c_prev)
    den = jnp.sum(s, axis=1, keepdims=True) + a_inter * jnp.sum(q * n_prev, axis=1, keepdims=True)
    hc = num * (1.0 / jnp.maximum(jnp.abs(den), jnp.exp(-m_t)))
    m_new = m_t[L - 1:L, :]
    b_last = b_row[:, L - 1:L]
    w_row = jnp.exp(b_last + ib - m_new)
    w_col = jnp.sum(jnp.where(eye, w_row, 0.0), axis=1, keepdims=True)
    a_end = jnp.exp(b_last + m_prev - m_new)
    kw = k * w_col
    n_new = a_end * n_prev + jnp.sum(kw, axis=0, keepdims=True)
    return hc, (a_end, kw, v), n_new, m_new


def _ml_new_c(c_prev, update):
    a_end, kw, v = update
    return a_end * c_prev + _dot_tn(kw, v)


def _ml_gate_out(hc, gn, sk, xc, z):
    return (_head_norm(hc, gn) + sk * xc) * _silu(z)


def _ml_cell_body(q_ref, k_ref, v_ref, xc_ref, z_ref, g_ref, gn_ref, sk_ref, o_ref, co_ref, no_ref, mo_ref, cs, ns, ms):
    c = pl.program_id(1)
    nc = pl.num_programs(1)

    @pl.when(c == 0)
    def _():
        cs[...] = jnp.zeros(cs.shape, F32)
        ns[...] = jnp.zeros(ns.shape, F32)
        ms[...] = jnp.zeros(ms.shape, F32)

    masks = _ml_masks(q_ref.shape[0])
    for h in range(ML_HEADS):
        sl = slice(h * ML_DH, (h + 1) * ML_DH)
        hc, update, n_new, m_new = _ml_chunk(
            q_ref[:, sl], k_ref[:, sl], v_ref[:, sl], g_ref[0, h:h + 1, :], g_ref[0, ML_HEADS + h:ML_HEADS + h + 1, :],
            cs[h], ns[h:h + 1, :], ms[h:h + 1, :], masks)
        cs[h] = _ml_new_c(cs[h], update)
        ns[h:h + 1, :] = n_new
        ms[h:h + 1, :] = m_new
        o_ref[:, sl] = _ml_gate_out(hc, gn_ref[:, sl], sk_ref[:, sl], xc_ref[:, sl], z_ref[:, sl])

    @pl.when(c == nc - 1)
    def _():
        co_ref[0] = cs[...]
        no_ref[0] = ns[...]
        mo_ref[0] = ms[...]


def _ml_cell(q, k, v, xc, xmz, gates, gn, skip, n_seq, seq_len, chunk):
    t_all = n_seq * seq_len
    nc = seq_len // chunk
    row_spec = pl.BlockSpec((chunk, ML_INNER), lambda b, c: (b * nc + c, 0))
    full = pl.BlockSpec((1, ML_INNER), lambda b, c: (0, 0))
    out = pl.pallas_call(
        _ml_cell_body,
        grid=(n_seq, nc),
        in_specs=[row_spec, row_spec, row_spec, row_spec,
                  pl.BlockSpec((chunk, ML_INNER), lambda b, c: (b * nc + c, 1)),
                  pl.BlockSpec((1, 2 * ML_HEADS, chunk), lambda b, c: (b * nc + c, 0, 0)), full, full],
        out_specs=[row_spec,
                   pl.BlockSpec((1, ML_HEADS, ML_DH, ML_DH), lambda b, c: (b, 0, 0, 0)),
                   pl.BlockSpec((1, ML_HEADS, ML_DH), lambda b, c: (b, 0, 0)),
                   pl.BlockSpec((1, ML_HEADS, 1), lambda b, c: (b, 0, 0))],
        out_shape=[jax.ShapeDtypeStruct((t_all, ML_INNER), F32),
                   jax.ShapeDtypeStruct((n_seq, ML_HEADS, ML_DH, ML_DH), F32),
                   jax.ShapeDtypeStruct((n_seq, ML_HEADS, ML_DH), F32),
                   jax.ShapeDtypeStruct((n_seq, ML_HEADS, 1), F32)],
        scratch_shapes=[pltpu.VMEM((ML_HEADS, ML_DH, ML_DH), F32), pltpu.VMEM((ML_HEADS, ML_DH), F32),
                        pltpu.VMEM((ML_HEADS, 1), F32)],
        compiler_params=_params(("arbitrary", "arbitrary"), 56),
        name="ml_cell",
    )(q, k, v, xc, xmz, gates, gn, skip)
    gated, c_new, n_new, m_new = out
    return gated, c_new, n_new, m_new.reshape(n_seq, ML_HEADS)


def _ml_cell_sample_body(*refs):
    q_ref, k_ref, v_ref, xc_ref, z_ref, g_ref, gn_ref, sk_ref, c0_ref, n0_ref, m0_ref = refs[:11]
    o_ref, co_ref, no_ref, mo_ref = refs[-4:]
    seq_len = g_ref.shape[2]
    masks = _ml_masks(seq_len)
    pending = []
    for j in range(g_ref.shape[0]):
        rows = slice(j * seq_len, (j + 1) * seq_len)
        for h in range(ML_HEADS):
            sl = slice(h * ML_DH, (h + 1) * ML_DH)
            hc, update, n_new, m_new = _ml_chunk(
                q_ref[rows, sl], k_ref[rows, sl], v_ref[rows, sl], g_ref[j, h:h + 1, :],
                g_ref[j, ML_HEADS + h:ML_HEADS + h + 1, :], c0_ref[j, h], n0_ref[j, h:h + 1, :], m0_ref[j, h:h + 1, :],
                masks)
            out = _ml_gate_out(hc, gn_ref[:, sl], sk_ref[:, sl], xc_ref[rows, sl], z_ref[rows, sl])
            pending.append((j, h, rows, sl, out, update, n_new, m_new))
    for j, h, rows, sl, out, update, n_new, m_new in pending:
        o_ref[rows, sl] = out
        no_ref[j, h:h + 1, :] = n_new
        mo_ref[j, h:h + 1, :] = m_new
        co_ref[j, h] = _ml_new_c(c0_ref[j, h], update)


def _ml_cell_sample(q, k, v, xc, xmz, gates, gn, skip, n_seq, seq_len, row0, c_all, n_all, m_all, layer, n_layers,
                    c_out_prev):
    ns = ML_SAMPLE_SEQS
    rows = ns * seq_len
    blk0 = row0 // rows
    s0 = layer * n_seq // ns
    row_spec = pl.BlockSpec((rows, ML_INNER), lambda b: (blk0 + b, 0))
    full = pl.BlockSpec((1, ML_INNER), lambda b: (0, 0))
    c_spec = pl.BlockSpec((ns, ML_HEADS, ML_DH, ML_DH), lambda b: (s0 + b, 0, 0, 0))
    in_specs = [row_spec, row_spec, row_spec, row_spec,
                pl.BlockSpec((rows, ML_INNER), lambda b: (blk0 + b, 1)),
                pl.BlockSpec((ns, 2 * ML_HEADS, seq_len), lambda b: (b, 0, 0)), full, full,
                c_spec,
                pl.BlockSpec((ns, ML_HEADS, ML_DH), lambda b: (s0 + b, 0, 0)),
                pl.BlockSpec((ns, ML_HEADS, 1), lambda b: (s0 + b, 0, 0))]
    args = [q, k, v, xc, xmz, gates, gn, skip, c_all, n_all, m_all]
    aliases = {}
    if c_out_prev is not None:
        in_specs.append(pl.BlockSpec(memory_space=pl.ANY))
        args.append(c_out_prev)
        aliases[len(args) - 1] = 1
    return pl.pallas_call(
        _ml_cell_sample_body,
        grid=(n_seq // ns,),
        in_specs=in_specs,
        out_specs=[pl.BlockSpec((rows, ML_INNER), lambda b: (b, 0)), c_spec,
                   pl.BlockSpec((ns, ML_HEADS, ML_DH), lambda b: (b, 0, 0)),
                   pl.BlockSpec((ns, ML_HEADS, 1), lambda b: (b, 0, 0))],
        out_shape=[jax.ShapeDtypeStruct((n_seq * seq_len, ML_INNER), F32),
                   jax.ShapeDtypeStruct((n_layers * n_seq, ML_HEADS, ML_DH, ML_DH), F32),
                   jax.ShapeDtypeStruct((n_seq, ML_HEADS, ML_DH), F32),
                   jax.ShapeDtypeStruct((n_seq, ML_HEADS, 1), F32)],
        input_output_aliases=aliases,
        compiler_params=_params(("arbitrary",), 56),
        name="ml_cell_sample",
    )(*args)


def _ml_weights(w_in, conv_w, conv_b, wq, wk, wv, w_i, b_i, w_f, b_f, gn_g, skip, w_out):
    nt = ML_INNER // V7X_MXU_DIM
    per = V7X_MXU_DIM // ML_QKV_BLOCK
    eye = jnp.eye(per, dtype=F32)

    def block_diag(w):
        wt = w.reshape(nt, per, ML_QKV_BLOCK, ML_QKV_BLOCK)
        return jnp.einsum("tncd,nm->tncmd", wt, eye).reshape(nt, V7X_MXU_DIM, V7X_MXU_DIM).astype(BF16)

    wg = jnp.concatenate([w_i, w_f], axis=1).T.astype(BF16)
    return dict(w_in=w_in.astype(BF16), conv_w=conv_w, conv_b=conv_b.reshape(1, ML_INNER),
                wq=block_diag(wq), wk=block_diag(wk), wv=block_diag(wv),
                wgq=wg[:, :ML_INNER], wgk=wg[:, ML_INNER:2 * ML_INNER], wgv=wg[:, 2 * ML_INNER:],
                gb=jnp.concatenate([b_i, b_f]).reshape(2 * ML_HEADS, 1),
                gn=gn_g.reshape(1, ML_INNER), skip=skip.reshape(1, ML_INNER), w_out=w_out.astype(BF16))


def _ml_layer(x_all, w, dims, c_all, n_all, m_all, state_conv, layer, n_layers, c_out_prev, ln_g, ln_b):
    n_p, seq_len, n_s, sample_len = dims
    t_p = n_p * seq_len
    rb = ROW_BLOCK
    xmz = _mm(x_all, w["w_in"], 2048)
    pad = jnp.pad(state_conv, ((0, 0), (0, sample_len - (ML_CONV - 1)), (0, 0))).reshape(n_s * sample_len, ML_INNER)
    conv_rows = jnp.concatenate([jnp.zeros((rb, ML_INNER), F32), pad], axis=0)
    q, k, v, xc, g3 = _ml_pre(xmz, conv_rows, w, t_p, seq_len, sample_len)
    npb = t_p // rb
    g_p = g3[:npb]
    if ML_CHUNK != rb:
        g_p = g_p.reshape(npb, 2 * ML_HEADS, rb // ML_CHUNK, ML_CHUNK).transpose(0, 2, 1, 3).reshape(-1, 2 * ML_HEADS, ML_CHUNK)
    g_s = g3[npb:].reshape(-1, 2 * ML_HEADS, rb // sample_len, sample_len).transpose(0, 2, 1, 3)
    g_s = g_s.reshape(n_s, 2 * ML_HEADS, sample_len)
    gated_p, pc, pn, pm = _ml_cell(q, k, v, xc, xmz, g_p, w["gn"], w["skip"], n_p, seq_len, ML_CHUNK)
    gated_s, c_out, sn, sm = _ml_cell_sample(q, k, v, xc, xmz, g_s, w["gn"], w["skip"], n_s, sample_len, t_p,
                                             c_all, n_all, m_all, layer, n_layers, c_out_prev)
    h1 = _mm_ln(gated_p, gated_s, w["w_out"], x_all, ln_g, ln_b)
    tail = ML_CONV - 1
    p_conv = jnp.stack([xmz[(b + 1) * seq_len - tail:(b + 1) * seq_len, :ML_INNER] for b in range(n_p)])
    s_conv = xmz[t_p:, :ML_INNER].reshape(n_s, sample_len, ML_INNER)[:, sample_len - tail:]
    return h1, (pc, pn, pm, p_conv), (sn, sm.reshape(n_s, ML_HEADS), s_conv), c_out


def _gla_in_body(x_ref, w_ref, wa_ref, wa2_ref, ba_ref, o_ref, la_ref):
    x = x_ref[...].astype(BF16)
    o_ref[...] = jnp.dot(x, w_ref[...], preferred_element_type=F32)
    a = jnp.dot(x, wa_ref[...], preferred_element_type=F32)
    la_ref[...] = _log_sigmoid(_dot(a, wa2_ref[...]) + ba_ref[...]) / GLA_TAU


def _gla_in(x, w):
    m, k = x.shape
    tm = MM_ROWS
    n1 = 2 * GLA_KT + 2 * GLA_VT
    full = lambda shape: pl.BlockSpec(shape, lambda i: (0, 0))
    return pl.pallas_call(
        _gla_in_body,
        grid=(m // tm,),
        in_specs=[pl.BlockSpec((tm, k), lambda i: (i, 0)), full((k, n1)), full((k, V7X_LANES)),
                  full((V7X_LANES, GLA_KT)), full((1, GLA_KT))],
        out_specs=[pl.BlockSpec((tm, n1), lambda i: (i, 0)), pl.BlockSpec((tm, GLA_KT), lambda i: (i, 0))],
        out_shape=[jax.ShapeDtypeStruct((m, n1), F32), jax.ShapeDtypeStruct((m, GLA_KT), F32)],
        compiler_params=_params(("arbitrary",), 48),
        name="gla_in",
    )(x, w["w_qkvr"], w["w_a"], w["w_a2"], w["b_a"])


def _gla_levels(chunk):
    t = jnp.arange(chunk)[:, None]
    s = jnp.arange(chunk)[None, :]
    mats = [(s <= t)]
    size = chunk
    while size >= 2:
        ref = t - t % size + size // 2 - 1
        mats.append(s <= ref)
        size //= 2
    levels = jnp.concatenate(mats, axis=0).astype(BF16)
    return jnp.pad(levels, ((0, 0), (0, max(V7X_LANES - chunk, 0))))


def _gla_chunk(q, k, v, la, lv, st):
    L = q.shape[0]
    kpad = lv.shape[1]
    parts = jnp.concatenate(_split3(la), axis=1)
    if kpad > L:
        parts = jnp.concatenate([parts, jnp.zeros((kpad - L, parts.shape[1]), BF16)], axis=0)
    cum = jnp.dot(lv, parts, preferred_element_type=F32)
    cum = cum[:, :GLA_DK] + cum[:, GLA_DK:2 * GLA_DK] + cum[:, 2 * GLA_DK:]
    b = cum[0:L]
    ri = lax.broadcasted_iota(jnp.int32, (L, L), 0)
    ci = lax.broadcasted_iota(jnp.int32, (L, L), 1)
    tpos = lax.broadcasted_iota(jnp.int32, (L, 1), 0)
    att = jnp.where(ri == ci, jnp.sum(q * k, axis=1, keepdims=True), 0.0)
    size = L
    lvl = 1
    while size >= 2:
        bref = cum[lvl * L:(lvl + 1) * L]
        upper = (tpos % size) >= (size // 2)
        qs = q * jnp.exp(jnp.where(upper, b - bref, -jnp.inf))
        ks = k * jnp.exp(jnp.where(upper, -jnp.inf, bref - b))
        att = att + jnp.where((ri // size) == (ci // size), _dot_nt(qs, ks), 0.0)
        size //= 2
        lvl += 1
    o = _dot(att, v) + _dot(q * jnp.exp(b), st)
    b_end = b[L - 1:L, :]
    e_end = jnp.exp(b_end)
    rk = lax.broadcasted_iota(jnp.int32, (GLA_DK, GLA_DK), 0)
    ck = lax.broadcasted_iota(jnp.int32, (GLA_DK, GLA_DK), 1)
    e_col = jnp.sum(jnp.where(rk == ck, e_end, 0.0), axis=1, keepdims=True)
    st_new = e_col * st + _dot_tn(k * jnp.exp(b_end - b), v)
    return o, st_new


def _gla_heads(x_ref, la_ref, lv_ref, gn_ref, o_ref, read_state, write_state, rows=slice(None), defer_stores=False):
    lv = lv_ref[...]
    stores = []
    for h in range(GLA_HEADS):
        qs = slice(h * GLA_DK, (h + 1) * GLA_DK)
        ks = slice(GLA_KT + h * GLA_DK, GLA_KT + (h + 1) * GLA_DK)
        vs = slice(2 * GLA_KT + h * GLA_DV, 2 * GLA_KT + (h + 1) * GLA_DV)
        rs = slice(2 * GLA_KT + GLA_VT + h * GLA_DV, 2 * GLA_KT + GLA_VT + (h + 1) * GLA_DV)
        os_ = slice(h * GLA_DV, (h + 1) * GLA_DV)
        o, st_new = _gla_chunk(x_ref[rows, qs] * (GLA_DK ** -0.5), x_ref[rows, ks], x_ref[rows, vs], la_ref[rows, qs],
                               lv, read_state(h))
        out = _silu(x_ref[rows, rs]) * _head_norm(o, gn_ref[:, os_])

        def store(h=h, os_=os_, out=out, st_new=st_new):
            o_ref[rows, os_] = out
            write_state(h, st_new)

        if defer_stores:
            stores.append(store)
        else:
            store()
    return stores


def _gla_cell_body(x_ref, la_ref, lv_ref, gn_ref, o_ref, so_ref, ss):
    c = pl.program_id(1)
    nc = pl.num_programs(1)

    @pl.when(c == 0)
    def _():
        ss[...] = jnp.zeros(ss.shape, F32)

    def write(h, st):
        ss[h] = st

    _gla_heads(x_ref, la_ref, lv_ref, gn_ref, o_ref, lambda h: ss[h], write)

    @pl.when(c == nc - 1)
    def _():
        so_ref[0] = ss[...]


def _gla_cell(qkvr, la, gn, n_seq, seq_len, chunk):
    n1 = qkvr.shape[1]
    nc = seq_len // chunk
    levels = _gla_levels(chunk)
    return pl.pallas_call(
        _gla_cell_body,
        grid=(n_seq, nc),
        in_specs=[pl.BlockSpec((chunk, n1), lambda b, c: (b * nc + c, 0)),
                  pl.BlockSpec((chunk, GLA_KT), lambda b, c: (b * nc + c, 0)),
                  pl.BlockSpec(levels.shape, lambda b, c: (0, 0)),
                  pl.BlockSpec((1, GLA_VT), lambda b, c: (0, 0))],
        out_specs=[pl.BlockSpec((chunk, GLA_VT), lambda b, c: (b * nc + c, 0)),
                   pl.BlockSpec((1, GLA_HEADS, GLA_DK, GLA_DV), lambda b, c: (b, 0, 0, 0))],
        out_shape=[jax.ShapeDtypeStruct((n_seq * seq_len, GLA_VT), F32),
                   jax.ShapeDtypeStruct((n_seq, GLA_HEADS, GLA_DK, GLA_DV), F32)],
        scratch_shapes=[pltpu.VMEM((GLA_HEADS, GLA_DK, GLA_DV), F32)],
        compiler_params=_params(("arbitrary", "arbitrary"), 32),
        name="gla_cell",
    )(qkvr, la, levels, gn)


def _gla_cell_sample_body(x_ref, la_ref, lv_ref, gn_ref, s0_ref, o_ref, so_ref):
    seq_len = x_ref.shape[0] // s0_ref.shape[0]
    stores = []
    for j in range(s0_ref.shape[0]):

        def write(h, st, j=j):
            so_ref[j, h] = st

        stores += _gla_heads(x_ref, la_ref, lv_ref, gn_ref, o_ref, lambda h, j=j: s0_ref[j, h], write,
                             rows=slice(j * seq_len, (j + 1) * seq_len), defer_stores=True)
    for store in stores:
        store()


def _gla_cell_sample(qkvr, la, gn, n_seq, seq_len, row0, state):
    n1 = qkvr.shape[1]
    ns = GLA_SAMPLE_SEQS
    rows = ns * seq_len
    blk0 = row0 // rows
    levels = _gla_levels(seq_len)
    st_spec = pl.BlockSpec((ns, GLA_HEADS, GLA_DK, GLA_DV), lambda b: (b, 0, 0, 0))
    return pl.pallas_call(
        _gla_cell_sample_body,
        grid=(n_seq // ns,),
        in_specs=[pl.BlockSpec((rows, n1), lambda b: (blk0 + b, 0)),
                  pl.BlockSpec((rows, GLA_KT), lambda b: (blk0 + b, 0)),
                  pl.BlockSpec(levels.shape, lambda b: (0, 0)),
                  pl.BlockSpec((1, GLA_VT), lambda b: (0, 0)),
                  st_spec],
        out_specs=[pl.BlockSpec((rows, GLA_VT), lambda b: (b, 0)), st_spec],
        out_shape=[jax.ShapeDtypeStruct((n_seq * seq_len, GLA_VT), F32),
                   jax.ShapeDtypeStruct((n_seq, GLA_HEADS, GLA_DK, GLA_DV), F32)],
        compiler_params=_params(("arbitrary",), 32),
        name="gla_cell_sample",
    )(qkvr, la, levels, gn, state)


def _gla_weights(w_in, w_a2, b_a, gn_g, w_out):
    n1 = 2 * GLA_KT + 2 * GLA_VT
    w_a = jnp.pad(w_in[:, n1:], ((0, 0), (0, V7X_LANES - GLA_RANK)))
    w_a2p = jnp.pad(w_a2, ((0, V7X_LANES - GLA_RANK), (0, 0)))
    return dict(w_qkvr=w_in[:, :n1].astype(BF16), w_a=w_a.astype(BF16), w_a2=w_a2p.astype(BF16),
                b_a=b_a.reshape(1, GLA_KT), gn=gn_g.reshape(1, GLA_VT), w_out=w_out.astype(BF16))


def _gla_layer(x_all, w, dims, state_s, ln_g, ln_b):
    n_p, seq_len, n_s, sample_len = dims
    t_p = n_p * seq_len
    qkvr, la = _gla_in(x_all, w)
    gated_p, p_s = _gla_cell(qkvr, la, w["gn"], n_p, seq_len, min(GLA_CHUNK, seq_len))
    gated_s, s_s = _gla_cell_sample(qkvr, la, w["gn"], n_s, sample_len, t_p, state_s)
    h1 = _mm_ln(gated_p, gated_s, w["w_out"], x_all, ln_g, ln_b)
    return h1, p_s, s_s


def _s5_body(x_ref, sre_ref, sim_ref, win_ref, bt_ref, cre_ref, cim_ref, lre_ref, lim_ref, d_ref, wglu_ref,
             g_ref, b_ref, o_ref, ore_ref, oim_ref, st_re, st_im, car_re, car_im, *, nb, lc):
    c = pl.program_id(1)
    nc = pl.num_programs(1)
    rows = nb * lc
    ntile = D_MODEL // S5_TILE_CH

    @pl.when(c == 0)
    def _():
        car_re[...] = sre_ref[...]
        car_im[...] = sim_ref[...]

    x = x_ref[...].reshape(rows, D_MODEL)
    u = _dot(x, win_ref[...])
    lpt = S5_TILE_ST // V7X_LANES
    for t in range(ntile):
        bu = _dot(u[:, t * S5_TILE_CH:(t + 1) * S5_TILE_CH], bt_ref[t])
        for j in range(lpt):
            st_re[t * lpt + j] = bu[:, j * V7X_LANES:(j + 1) * V7X_LANES]
            st_im[t * lpt + j] = bu[:, S5_TILE_ST + j * V7X_LANES:S5_TILE_ST + (j + 1) * V7X_LANES]

    for gidx in range(nb // V7X_SUBLANES):
        grp = slice(gidx * V7X_SUBLANES, (gidx + 1) * V7X_SUBLANES)
        for t in range(ntile):
            slabs = list(range(t * lpt, (t + 1) * lpt))
            lanes = [slice(j * V7X_LANES, (j + 1) * V7X_LANES) for j in slabs]
            lre = [lre_ref[:, ln] for ln in lanes]
            lim = [lim_ref[:, ln] for ln in lanes]

            def step(tok, carry, slabs=slabs, lre=lre, lim=lim, gidx=gidx):
                sel = pl.ds(pl.multiple_of(tok * nb + gidx * V7X_SUBLANES, V7X_SUBLANES), V7X_SUBLANES)
                out = []
                for n, j in enumerate(slabs):
                    pr, pi = carry[2 * n], carry[2 * n + 1]
                    nr = lre[n] * pr - lim[n] * pi + st_re[j, sel, :]
                    ni = lre[n] * pi + lim[n] * pr + st_im[j, sel, :]
                    st_re[j, sel, :] = nr
                    st_im[j, sel, :] = ni
                    out += [nr, ni]
                return tuple(out)

            init = []
            for ln in lanes:
                init += [car_re[grp, ln], car_im[grp, ln]]
            fin = lax.fori_loop(0, lc, step, tuple(init))
            for n, ln in enumerate(lanes):
                car_re[grp, ln] = fin[2 * n]
                car_im[grp, ln] = fin[2 * n + 1]

    ys = []
    for t in range(ntile):
        sre = jnp.concatenate([st_re[t * lpt + j] for j in range(lpt)], axis=1)
        sim = jnp.concatenate([st_im[t * lpt + j] for j in range(lpt)], axis=1)
        ys.append(_dot(sre, cre_ref[t]) - _dot(sim, cim_ref[t]))
    y = jnp.concatenate(ys, axis=1) + d_ref[...] * u
    y = jax.nn.gelu(y)
    vg = _dot(y, wglu_ref[...])
    mix = vg[:, :D_MODEL] * _sigmoid(vg[:, D_MODEL:])
    o_ref[...] = _layer_norm(DEEPNORM_ALPHA * x + mix, g_ref[...], b_ref[...]).reshape(lc, nb, D_MODEL)

    @pl.when(c == nc - 1)
    def _():
        ore_ref[...] = car_re[...]
        oim_ref[...] = car_im[...]


def _s5_call(x3, s_re, s_im, w, ln_g, ln_b, nb, lc):
    seq_len, n_seq, _ = x3.shape
    ntile = D_MODEL // S5_TILE_CH
    rows = nb * lc
    full2 = lambda shape: pl.BlockSpec(shape, lambda i, c: (0, 0), pipeline_mode=pl.Buffered(1))
    full3 = lambda shape: pl.BlockSpec(shape, lambda i, c: (0, 0, 0), pipeline_mode=pl.Buffered(1))
    st_spec = pl.BlockSpec((nb, S5_STATE), lambda i, c: (i, 0))
    x_spec = pl.BlockSpec((lc, nb, D_MODEL), lambda i, c: (c, i, 0))
    return pl.pallas_call(
        functools.partial(_s5_body, nb=nb, lc=lc),
        grid=(n_seq // nb, seq_len // lc),
        in_specs=[x_spec, st_spec, st_spec,
                  full2((D_MODEL, D_MODEL)), full3((ntile, S5_TILE_CH, 2 * S5_TILE_ST)),
                  full3((ntile, S5_TILE_ST, S5_TILE_CH)), full3((ntile, S5_TILE_ST, S5_TILE_CH)),
                  full2((V7X_SUBLANES, S5_STATE)), full2((V7X_SUBLANES, S5_STATE)), full2((1, D_MODEL)),
                  full2((D_MODEL, 2 * D_MODEL)), full2((1, D_MODEL)), full2((1, D_MODEL))],
        out_specs=[x_spec, st_spec, st_spec],
        out_shape=[jax.ShapeDtypeStruct(x3.shape, F32), jax.ShapeDtypeStruct((n_seq, S5_STATE), F32),
                   jax.ShapeDtypeStruct((n_seq, S5_STATE), F32)],
        scratch_shapes=[pltpu.VMEM((S5_STATE // V7X_LANES, rows, V7X_LANES), F32),
                        pltpu.VMEM((S5_STATE // V7X_LANES, rows, V7X_LANES), F32),
                        pltpu.VMEM((nb, S5_STATE), F32), pltpu.VMEM((nb, S5_STATE), F32)],
        compiler_params=_params(("arbitrary", "arbitrary"), 56),
        name="s5",
    )(x3, s_re, s_im, w["w_in"], w["b_tiles"], w["c_re"], w["c_im"], w["lam_re"], w["lam_im"], w["d"], w["w_glu"],
      ln_g.reshape(1, D_MODEL), ln_b.reshape(1, D_MODEL))


def _s5_weights(w_in, a_re, a_im, log_dt, b_re, b_im, c_re, c_im, d_skip, w_glu):
    lam = lax.complex(a_re.astype(F32), a_im.astype(F32))
    dt = jnp.exp(log_dt.astype(F32))[:, None]
    lam_bar = jnp.exp(lam * dt)
    b_bar = ((lam_bar - 1.0) / lam)[..., None] * lax.complex(b_re.astype(F32), b_im.astype(F32))
    ntile = D_MODEL // S5_TILE_CH
    gpt = S5_TILE_CH // S5_GC
    eye = jnp.eye(gpt, dtype=F32)

    def b_tiles(bb):
        return jnp.einsum("igpc,gh->igchp", bb.reshape(ntile, gpt, S5_P, S5_GC), eye).reshape(ntile, S5_TILE_CH, S5_TILE_ST)

    def c_tiles(cc):
        return jnp.einsum("igcp,gh->igphc", cc.reshape(ntile, gpt, S5_GC, S5_P), eye).reshape(ntile, S5_TILE_ST, S5_TILE_CH)

    bt = jnp.concatenate([b_tiles(b_bar.real), b_tiles(b_bar.imag)], axis=2).astype(BF16)
    bcast = lambda a: jnp.broadcast_to(a.reshape(1, S5_STATE), (V7X_SUBLANES, S5_STATE))
    return dict(w_in=w_in.astype(BF16), b_tiles=bt, c_re=c_tiles(c_re.astype(F32)).astype(BF16),
                c_im=c_tiles(c_im.astype(F32)).astype(BF16), lam_re=bcast(lam_bar.real), lam_im=bcast(lam_bar.imag),
                d=d_skip.reshape(1, D_MODEL).astype(F32), w_glu=w_glu.astype(BF16))


def _s5_layer(x_all, w, dims, state_re, state_im, ln_g, ln_b):
    n_p, seq_len, n_s, sample_len = dims
    t_p = n_p * seq_len
    xp = x_all[:t_p].reshape(n_p, seq_len, D_MODEL).transpose(1, 0, 2)
    xs = x_all[t_p:].reshape(n_s, sample_len, D_MODEL).transpose(1, 0, 2)
    zero = jnp.zeros((n_p, S5_STATE), F32)
    hp, p_re, p_im = _s5_call(xp, zero, zero, w, ln_g, ln_b, n_p, min(S5_CHUNK, seq_len))
    nb_s = min(n_s, ROW_BLOCK // sample_len)
    hs, s_re, s_im = _s5_call(xs, state_re.reshape(n_s, S5_STATE), state_im.reshape(n_s, S5_STATE), w, ln_g, ln_b,
                              nb_s, sample_len)
    h1 = jnp.concatenate([hp.transpose(1, 0, 2).reshape(t_p, D_MODEL),
                          hs.transpose(1, 0, 2).reshape(n_s * sample_len, D_MODEL)], axis=0)
    shp = lambda a, n: a.reshape(n, S5_GROUPS, S5_P)
    return h1, (shp(p_re, n_p), shp(p_im, n_p)), (shp(s_re, n_s), shp(s_im, n_s))


def _route_body(x_ref, wh_ref, wl_ref, br_ref, g_ref):
    x = x_ref[...]
    xh = x.astype(BF16)
    xl = (x - xh.astype(F32)).astype(BF16)
    wh = wh_ref[...]
    logits = _dot_nt(wh, xh) + _dot_nt(wh, xl) + _dot_nt(wl_ref[...], xh)
    s = _sigmoid(logits)
    work = s + br_ref[...]
    row = lax.broadcasted_iota(jnp.int32, s.shape, 0).astype(F32)
    chosen = jnp.zeros(s.shape, jnp.bool_)
    for _ in range(TOP_K):
        mx = jnp.max(work, axis=0, keepdims=True)
        idx = jnp.min(jnp.where(work == mx, row, float(N_EXPERTS)), axis=0, keepdims=True)
        hit = row == idx
        chosen = jnp.logical_or(chosen, hit)
        work = jnp.where(hit, -jnp.inf, work)
    sel = jnp.where(chosen, s, 0.0)
    g_ref[...] = sel / jnp.sum(sel, axis=0, keepdims=True) * ROUTE_SCALE


def _route(x, w):
    m, k = x.shape
    tm = ROW_BLOCK
    full = lambda shape: pl.BlockSpec(shape, lambda i: (0, 0))
    gates_t = pl.pallas_call(
        _route_body,
        grid=(m // tm,),
        in_specs=[pl.BlockSpec((tm, k), lambda i: (i, 0)), full((N_EXPERTS, k)), full((N_EXPERTS, k)),
                  full((N_EXPERTS, 1))],
        out_specs=pl.BlockSpec((N_EXPERTS, tm), lambda i: (0, i)),
        out_shape=jax.ShapeDtypeStruct((N_EXPERTS, m), F32),
        compiler_params=_params(("arbitrary",), 32),
        name="route",
    )(x, w["wr_hi"], w["wr_lo"], w["b_router"])
    return gates_t.T


def _moe_body(x_ref, gd_ref, wg_ref, wu_ref, wd_ref, sg_ref, su_ref, sd_ref, g_ref, b_ref, o_ref, acc, xb_ref):
    e = pl.program_id(1)
    ne = pl.num_programs(1)

    @pl.when(e == 0)
    def _():
        xb0 = x_ref[...].astype(BF16)
        xb_ref[...] = xb0
        hs = _silu(jnp.dot(xb0, sg_ref[...], preferred_element_type=F32)) * jnp.dot(xb0, su_ref[...], preferred_element_type=F32)
        acc[...] = _dot(hs, sd_ref[...])

    xb = xb_ref[...]
    gd = gd_ref[...]
    lane = lax.broadcasted_iota(jnp.int32, gd.shape, 1)
    total = None
    for j in range(MOE_EXPERTS_PER_STEP):
        h = _silu(_dot(xb, wg_ref[j])) * _dot(xb, wu_ref[j])
        gate = jnp.sum(jnp.where(lane == e * MOE_EXPERTS_PER_STEP + j, gd, 0.0), axis=1, keepdims=True)
        y = _dot(h, wd_ref[j]) * gate
        total = y if total is None else total + y
    acc[...] += total

    @pl.when(e == ne - 1)
    def _():
        o_ref[...] = _layer_norm(DEEPNORM_ALPHA * x_ref[...] + acc[...], g_ref[...], b_ref[...])


def _moe_ffn(x, gates, w, experts, layer, ln_g, ln_b):
    m, k = x.shape
    tm = MOE_ROWS
    steps = N_EXPERTS // MOE_EXPERTS_PER_STEP
    full = lambda shape: pl.BlockSpec(shape, lambda i, e: (0, 0))
    expert_map = lambda i, e: (layer * steps + e, 0, 0)
    return pl.pallas_call(
        _moe_body,
        grid=(m // tm, steps),
        in_specs=[pl.BlockSpec((tm, k), lambda i, e: (i, 0)), pl.BlockSpec((tm, N_EXPERTS), lambda i, e: (i, 0)),
                  pl.BlockSpec((MOE_EXPERTS_PER_STEP, k, EXPERT_FF), expert_map),
                  pl.BlockSpec((MOE_EXPERTS_PER_STEP, k, EXPERT_FF), expert_map),
                  pl.BlockSpec((MOE_EXPERTS_PER_STEP, EXPERT_FF, k), expert_map),
                  full((k, w["ws_gate"].shape[1])), full((k, w["ws_up"].shape[1])), full((w["ws_down"].shape[0], k)),
                  full((1, k)), full((1, k))],
        out_specs=pl.BlockSpec((tm, k), lambda i, e: (i, 0)),
        out_shape=jax.ShapeDtypeStruct((m, k), F32),
        scratch_shapes=[pltpu.VMEM((tm, k), F32), pltpu.VMEM((tm, k), BF16)],
        compiler_params=_params(("arbitrary", "arbitrary"), 60),
        name="moe_ffn",
    )(x, gates, experts[0], experts[1], experts[2], w["ws_gate"], w["ws_up"], w["ws_down"],
      ln_g.reshape(1, k), ln_b.reshape(1, k))


def _moe_weights(w_router, b_router, ws_gate, ws_up, ws_down):
    wr_t = w_router.T
    wr_hi = wr_t.astype(BF16)
    wr_lo = (wr_t - wr_hi.astype(F32)).astype(BF16)
    return dict(wr_hi=wr_hi, wr_lo=wr_lo, b_router=b_router.reshape(N_EXPERTS, 1).astype(F32),
                ws_gate=ws_gate.astype(BF16), ws_up=ws_up.astype(BF16), ws_down=ws_down.astype(BF16))


def _moe_layer(x, w, experts, layer, ln_g, ln_b):
    return _moe_ffn(x, _route(x, w), w, experts, layer, ln_g, ln_b)


def kernel(x_prompt, x_sample, state_mlstm_C, state_mlstm_n, state_mlstm_m, state_mlstm_conv, state_gla_S, state_s5_re, state_s5_im, ln1_g, ln1_b, ln2_g, ln2_b, ml_w_in, ml_conv_w, ml_conv_b, ml_wq, ml_wk, ml_wv, ml_w_i, ml_b_i, ml_w_f, ml_b_f, ml_gn_g, ml_skip, ml_w_out, gla_w_in, gla_w_a2, gla_b_a, gla_gn_g, gla_w_out, s5_w_in, s5_a_re, s5_a_im, s5_log_dt, s5_b_re, s5_b_im, s5_c_re, s5_c_im, s5_d, s5_w_glu, moe_w_router, moe_b_router, moe_w_gate, moe_w_up, moe_w_down, moe_ws_gate, moe_ws_up, moe_ws_down):
    n_p, seq_len, _ = x_prompt.shape
    n_s, sample_len, _ = x_sample.shape
    dims = (n_p, seq_len, n_s, sample_len)
    t_p = n_p * seq_len
    t_s = n_s * sample_len
    assert seq_len % MM_ROWS == 0 and t_s % MM_ROWS == 0 and MM_ROWS % ROW_BLOCK == 0 and ROW_BLOCK % sample_len == 0
    assert sample_len == V7X_SUBLANES and n_p % V7X_SUBLANES == 0 and (t_p + t_s) % MOE_ROWS == 0
    x = jnp.concatenate([x_prompt.reshape(t_p, D_MODEL), x_sample.reshape(t_s, D_MODEL)], axis=0)
    n_ml = len(range(0, DEPTH, N_MIXERS))
    c_all = state_mlstm_C.reshape(-1, ML_HEADS, ML_DH, ML_DH)
    n_all = state_mlstm_n.reshape(-1, ML_HEADS, ML_DH)
    m_all = state_mlstm_m.reshape(-1, ML_HEADS, 1)
    c_out = None
    flat_experts = lambda w: w.reshape((-1,) + w.shape[2:])
    experts = (flat_experts(moe_w_gate), flat_experts(moe_w_up), flat_experts(moe_w_down))
    p_ml, s_ml, p_gla, s_gla, p_s5, s_s5 = [], [], [], [], [], []
    for i in range(DEPTH):
        j = i // N_MIXERS
        if i % N_MIXERS == 0:
            w = _ml_weights(ml_w_in[j], ml_conv_w[j], ml_conv_b[j], ml_wq[j], ml_wk[j], ml_wv[j], ml_w_i[j], ml_b_i[j],
                            ml_w_f[j], ml_b_f[j], ml_gn_g[j], ml_skip[j], ml_w_out[j])
            x, ps, ss, c_out = _ml_layer(x, w, dims, c_all, n_all, m_all, state_mlstm_conv[j], j, n_ml, c_out,
                                         ln1_g[i], ln1_b[i])
            p_ml.append(ps)
            s_ml.append(ss)
        elif i % N_MIXERS == 1:
            w = _gla_weights(gla_w_in[j], gla_w_a2[j], gla_b_a[j], gla_gn_g[j], gla_w_out[j])
            x, ps, ss = _gla_layer(x, w, dims, state_gla_S[j], ln1_g[i], ln1_b[i])
            p_gla.append(ps)
            s_gla.append(ss)
        else:
            w = _s5_weights(s5_w_in[j], s5_a_re[j], s5_a_im[j], s5_log_dt[j], s5_b_re[j], s5_b_im[j], s5_c_re[j],
                            s5_c_im[j], s5_d[j], s5_w_glu[j])
            x, ps, ss = _s5_layer(x, w, dims, state_s5_re[j], state_s5_im[j], ln1_g[i], ln1_b[i])
            p_s5.append(ps)
            s_s5.append(ss)
        wm = _moe_weights(moe_w_router[i], moe_b_router[i], moe_ws_gate[i], moe_ws_up[i], moe_ws_down[i])
        x = _moe_layer(x, wm, experts, i, ln2_g[i], ln2_b[i])
    stack = lambda items, idx: jnp.stack([it[idx] for it in items])
    return (x[:t_p].reshape(n_p, seq_len, D_MODEL), x[t_p:].reshape(n_s, sample_len, D_MODEL),
            stack(p_ml, 0), stack(p_ml, 1), stack(p_ml, 2), stack(p_ml, 3), jnp.stack(p_gla),
            stack(p_s5, 0), stack(p_s5, 1),
            c_out.reshape(n_ml, n_s, ML_HEADS, ML_DH, ML_DH), stack(s_ml, 0), stack(s_ml, 1), stack(s_ml, 2),
            jnp.stack(s_gla), stack(s_s5, 0), stack(s_s5, 1))
```

```python
import functools

import jax
import jax.numpy as jnp
from jax import lax
from jax.experimental import pallas as pl
from jax.experimental.pallas import tpu as pltpu

F32 = jnp.float32
BF16 = jnp.bfloat16

D_MODEL = 1024
DEPTH = 4
N_MIXERS = 3
ML_INNER = 2 * D_MODEL
ML_HEADS = 4
ML_DH = ML_INNER // ML_HEADS
ML_QKV_BLOCK = 4
ML_CONV = 4
GLA_HEADS = 4
GLA_KT = D_MODEL // 2
GLA_VT = D_MODEL
GLA_DK = GLA_KT // GLA_HEADS
GLA_DV = GLA_VT // GLA_HEADS
GLA_RANK = 16
GLA_TAU = 16.0
S5_GC = 16
S5_GROUPS = D_MODEL // S5_GC
S5_P = 64
S5_STATE = S5_GROUPS * S5_P
N_EXPERTS = 64
TOP_K = 8
EXPERT_FF = 256
ROUTE_SCALE = 2.5
DEEPNORM_ALPHA = (2.0 * DEPTH) ** 0.25
NORM_EPS = 1e-5

V7X_VMEM_BYTES = 64 * 1024 * 1024
V7X_LANES = 128
V7X_SUBLANES = 8
V7X_MXU_DIM = 256

ROW_BLOCK = 256
MM_ROWS = 512
ML_SAMPLE_SEQS = 2
GLA_SAMPLE_SEQS = 4
ML_CHUNK = 256
GLA_CHUNK = 128
S5_CHUNK = 64
S5_TILE_CH = 128
S5_TILE_ST = S5_TILE_CH // S5_GC * S5_P
MOE_ROWS = 1024
MOE_EXPERTS_PER_STEP = 4


def _params(semantics, vmem_mb):
    assert vmem_mb * 1024 * 1024 < V7X_VMEM_BYTES
    return pltpu.CompilerParams(dimension_semantics=semantics, vmem_limit_bytes=vmem_mb * 1024 * 1024)


def _dot(a, b):
    return jnp.dot(a.astype(BF16), b.astype(BF16), preferred_element_type=F32)


def _dot_nt(a, b):
    return lax.dot_general(a.astype(BF16), b.astype(BF16), (((1,), (1,)), ((), ())), preferred_element_type=F32)


def _dot_tn(a, b):
    return lax.dot_general(a.astype(BF16), b.astype(BF16), (((0,), (0,)), ((), ())), preferred_element_type=F32)


def _sigmoid(x):
    return 1.0 / (1.0 + jnp.exp(-x))


def _silu(x):
    return x * _sigmoid(x)


def _log_sigmoid(x):
    return jnp.minimum(x, 0.0) - jnp.log1p(jnp.exp(-jnp.abs(x)))


def _layer_norm(y, g, b):
    mu = jnp.mean(y, axis=-1, keepdims=True)
    d = y - mu
    var = jnp.mean(d * d, axis=-1, keepdims=True)
    return d * lax.rsqrt(var + NORM_EPS) * g + b


def _head_norm(h, g):
    mu = jnp.mean(h, axis=-1, keepdims=True)
    d = h - mu
    var = jnp.mean(d * d, axis=-1, keepdims=True)
    return d * lax.rsqrt(var + NORM_EPS) * g


def _split3(x):
    hi = x.astype(BF16)
    r1 = x - hi.astype(F32)
    mid = r1.astype(BF16)
    lo = (r1 - mid.astype(F32)).astype(BF16)
    return hi, mid, lo


def _mm_body(x_ref, w_ref, o_ref):
    o_ref[...] = _dot(x_ref[...], w_ref[...])


def _mm(x, w, tn):
    m, k = x.shape
    n = w.shape[1]
    tm = MM_ROWS
    return pl.pallas_call(
        _mm_body,
        grid=(n // tn, m // tm),
        in_specs=[pl.BlockSpec((tm, k), lambda j, i: (i, 0)), pl.BlockSpec((k, tn), lambda j, i: (0, j))],
        out_specs=pl.BlockSpec((tm, tn), lambda j, i: (i, j)),
        out_shape=jax.ShapeDtypeStruct((m, n), F32),
        compiler_params=_params(("arbitrary", "arbitrary"), 40),
        name="mm",
    )(x, w)


def _mm_ln_body(ap_ref, as_ref, w_ref, r_ref, g_ref, b_ref, o_ref, *, n_prompt_blocks):
    i = pl.program_id(0)

    def run(a_ref):
        y = _dot(a_ref[...], w_ref[...])
        o_ref[...] = _layer_norm(DEEPNORM_ALPHA * r_ref[...] + y, g_ref[...], b_ref[...])

    pl.when(i < n_prompt_blocks)(lambda: run(ap_ref))
    pl.when(i >= n_prompt_blocks)(lambda: run(as_ref))


def _mm_ln(a_p, a_s, w, resid, g, b):
    k = a_p.shape[1]
    m = a_p.shape[0] + a_s.shape[0]
    n = w.shape[1]
    tm = MM_ROWS
    npb = a_p.shape[0] // tm
    return pl.pallas_call(
        functools.partial(_mm_ln_body, n_prompt_blocks=npb),
        grid=(m // tm,),
        in_specs=[pl.BlockSpec((tm, k), lambda i: (jnp.minimum(i, npb - 1), 0)),
                  pl.BlockSpec((tm, k), lambda i: (jnp.maximum(i - npb, 0), 0)),
                  pl.BlockSpec((k, n), lambda i: (0, 0)),
                  pl.BlockSpec((tm, n), lambda i: (i, 0)), pl.BlockSpec((1, n), lambda i: (0, 0)),
                  pl.BlockSpec((1, n), lambda i: (0, 0))],
        out_specs=pl.BlockSpec((tm, n), lambda i: (i, 0)),
        out_shape=jax.ShapeDtypeStruct((m, n), F32),
        compiler_params=_params(("arbitrary",), 48),
        name="mm_ln",
    )(a_p, a_s, w, resid, g.reshape(1, n), b.reshape(1, n))


def _ml_pre_body(x_ref, prev_ref, p_ref, cw_ref, cb_ref, wq_ref, wk_ref, wv_ref, wgq_ref, wgk_ref, wgv_ref, gb_ref,
                 q_ref, k_ref, v_ref, xc_ref, g_ref, scx, scp, *, n_prompt_blocks, blocks_per_seq, sample_len):
    i = pl.program_id(0)
    rows = x_ref.shape[0]
    is_prompt = i < n_prompt_blocks
    no_prev = jnp.logical_or(i % blocks_per_seq == 0, jnp.logical_not(is_prompt))
    x = x_ref[...]
    scx[0:V7X_SUBLANES, :] = jnp.where(no_prev, 0.0, prev_ref[...])
    scx[V7X_SUBLANES:V7X_SUBLANES + rows, :] = x

    scp[0:rows, :] = p_ref[...]
    scp[rows:rows + V7X_SUBLANES, :] = jnp.zeros((V7X_SUBLANES, x.shape[1]), F32)
    r = lax.broadcasted_iota(jnp.int32, (rows, 1), 0)
    tpos = jnp.where(is_prompt, (i % blocks_per_seq) * rows + r, r % sample_len)
    acc = cb_ref[...] + x * cw_ref[ML_CONV - 1:ML_CONV, :]
    for s in range(1, ML_CONV):
        xs = scx[V7X_SUBLANES - s:V7X_SUBLANES - s + rows, :]
        ps = scp[ML_CONV - 1 - s:ML_CONV - 1 - s + rows, :]
        acc = acc + jnp.where(tpos >= s, xs, ps) * cw_ref[ML_CONV - 1 - s:ML_CONV - s, :]
    xc = _silu(acc)
    xc_ref[...] = xc
    g = jnp.zeros((2 * ML_HEADS, rows), F32)
    for t in range(ML_INNER // V7X_MXU_DIM):
        sl = slice(t * V7X_MXU_DIM, (t + 1) * V7X_MXU_DIM)
        q_t = _dot(xc[:, sl], wq_ref[t])
        k_t = _dot(xc[:, sl], wk_ref[t])
        v_t = _dot(x[:, sl], wv_ref[t])
        q_ref[:, sl] = q_t
        k_ref[:, sl] = k_t * (ML_DH ** -0.5)
        v_ref[:, sl] = v_t
        g = g + _dot_nt(wgq_ref[:, sl], q_t) + _dot_nt(wgk_ref[:, sl], k_t) + _dot_nt(wgv_ref[:, sl], v_t)
    g = g + gb_ref[...]
    gate_row = lax.broadcasted_iota(jnp.int32, g.shape, 0)
    g_ref[0] = jnp.where(gate_row < ML_HEADS, g, _log_sigmoid(g))


def _ml_pre(xmz, conv_rows, w, n_prompt_rows, seq_len, sample_len):
    t_all = xmz.shape[0]
    rb = ROW_BLOCK
    nblk = t_all // rb
    npb = n_prompt_rows // rb
    bps = seq_len // rb
    sub = rb // V7X_SUBLANES
    body = functools.partial(_ml_pre_body, n_prompt_blocks=npb, blocks_per_seq=bps, sample_len=sample_len)
    full2 = lambda shape: pl.BlockSpec(shape, lambda i: (0, 0))
    full3 = lambda shape: pl.BlockSpec(shape, lambda i: (0, 0, 0))
    nt = ML_INNER // V7X_MXU_DIM
    row_spec = pl.BlockSpec((rb, ML_INNER), lambda i: (i, 0))
    return pl.pallas_call(
        body,
        grid=(nblk,),
        in_specs=[row_spec,
                  pl.BlockSpec((V7X_SUBLANES, ML_INNER), lambda i: (jnp.maximum(i * sub - 1, 0), 0)),
                  pl.BlockSpec((rb, ML_INNER), lambda i: (jnp.maximum(i - npb + 1, 0), 0)),
                  full2((ML_CONV, ML_INNER)), full2((1, ML_INNER)),
                  full3((nt, V7X_MXU_DIM, V7X_MXU_DIM)), full3((nt, V7X_MXU_DIM, V7X_MXU_DIM)),
                  full3((nt, V7X_MXU_DIM, V7X_MXU_DIM)),
                  full2((2 * ML_HEADS, ML_INNER)), full2((2 * ML_HEADS, ML_INNER)), full2((2 * ML_HEADS, ML_INNER)),
                  full2((2 * ML_HEADS, 1))],
        out_specs=[row_spec, row_spec, row_spec, row_spec,
                   pl.BlockSpec((1, 2 * ML_HEADS, rb), lambda i: (i, 0, 0))],
        out_shape=[jax.ShapeDtypeStruct((t_all, ML_INNER), F32)] * 4
        + [jax.ShapeDtypeStruct((nblk, 2 * ML_HEADS, rb), F32)],
        scratch_shapes=[pltpu.VMEM((rb + V7X_SUBLANES, ML_INNER), F32), pltpu.VMEM((rb + V7X_SUBLANES, ML_INNER), F32)],
        compiler_params=_params(("arbitrary",), 48),
        name="ml_pre",
    )(xmz, xmz, conv_rows, w["conv_w"], w["conv_b"], w["wq"], w["wk"], w["wv"], w["wgq"], w["wgk"], w["wgv"], w["gb"])


def _ml_masks(L):
    ri = lax.broadcasted_iota(jnp.int32, (L, L), 0)
    ci = lax.broadcasted_iota(jnp.int32, (L, L), 1)
    return ri == ci, ci <= ri, ri <= ci


def _ml_chunk(q, k, v, ip, fl, c_prev, n_prev, m_prev, masks):
    L = q.shape[0]
    eye, tril, triu = masks
    f_col = jnp.sum(jnp.where(eye, fl, 0.0), axis=1, keepdims=True)
    b_col = jnp.sum(jnp.where(tril, fl, 0.0), axis=1, keepdims=True)
    b_row = jnp.sum(jnp.where(triu, f_col, 0.0), axis=0, keepdims=True)
    ib = ip - b_row
    d = jnp.where(tril, b_col + ib, -jnp.inf)
    m_inter = b_col + m_prev
    m_t = jnp.maximum(m_inter, jnp.max(d, axis=1, keepdims=True))
    a_inter = jnp.exp(m_inter - m_t)
    s = _dot_nt(q, k) * jnp.exp(d - m_t)
    num = _dot(s, v) + a_inter * _dot(q, c_prev)
    den = jnp.sum(s, axis=1, keepdims=True) + a_inter * jnp.sum(q * n_prev, axis=1, keepdims=True)
    hc = num * (1.0 / jnp.maximum(jnp.abs(den), jnp.exp(-m_t)))
    m_new = m_t[L - 1:L, :]
    b_last = b_row[:, L - 1:L]
    w_row = jnp.exp(b_last + ib - m_new)
    w_col = jnp.sum(jnp.where(eye, w_row, 0.0), axis=1, keepdims=True)
    a_end = jnp.exp(b_last + m_prev - m_new)
    kw = k * w_col
    n_new = a_end * n_prev + jnp.sum(kw, axis=0, keepdims=True)
    return hc, (a_end, kw, v), n_new, m_new


def _ml_new_c(c_prev, update):
    a_end, kw, v = update
    return a_end * c_prev + _dot_tn(kw, v)


def _ml_gate_out(hc, gn, sk, xc, z):
    return (_head_norm(hc, gn) + sk * xc) * _silu(z)


def _ml_cell_body(q_ref, k_ref, v_ref, xc_ref, z_ref, g_ref, gn_ref, sk_ref, o_ref, co_ref, no_ref, mo_ref, cs, ns, ms):
    c = pl.program_id(1)
    nc = pl.num_programs(1)

    @pl.when(c == 0)
    def _():
        cs[...] = jnp.zeros(cs.shape, F32)
        ns[...] = jnp.zeros(ns.shape, F32)
        ms[...] = jnp.zeros(ms.shape, F32)

    masks = _ml_masks(q_ref.shape[0])
    for h in range(ML_HEADS):
        sl = slice(h * ML_DH, (h + 1) * ML_DH)
        hc, update, n_new, m_new = _ml_chunk(
            q_ref[:, sl], k_ref[:, sl], v_ref[:, sl], g_ref[0, h:h + 1, :], g_ref[0, ML_HEADS + h:ML_HEADS + h + 1, :],
            cs[h], ns[h:h + 1, :], ms[h:h + 1, :], masks)
        cs[h] = _ml_new_c(cs[h], update)
        ns[h:h + 1, :] = n_new
        ms[h:h + 1, :] = m_new
        o_ref[:, sl] = _ml_gate_out(hc, gn_ref[:, sl], sk_ref[:, sl], xc_ref[:, sl], z_ref[:, sl])

    @pl.when(c == nc - 1)
    def _():
        co_ref[0] = cs[...]
        no_ref[0] = ns[...]
        mo_ref[0] = ms[...]


def _ml_cell(q, k, v, xc, xmz, gates, gn, skip, n_seq, seq_len, chunk):
    t_all = n_seq * seq_len
    nc = seq_len // chunk
    row_spec = pl.BlockSpec((chunk, ML_INNER), lambda b, c: (b * nc + c, 0))
    full = pl.BlockSpec((1, ML_INNER), lambda b, c: (0, 0))
    out = pl.pallas_call(
        _ml_cell_body,
        grid=(n_seq, nc),
        in_specs=[row_spec, row_spec, row_spec, row_spec,
                  pl.BlockSpec((chunk, ML_INNER), lambda b, c: (b * nc + c, 1)),
                  pl.BlockSpec((1, 2 * ML_HEADS, chunk), lambda b, c: (b * nc + c, 0, 0)), full, full],
        out_specs=[row_spec,
                   pl.BlockSpec((1, ML_HEADS, ML_DH, ML_DH), lambda b, c: (b, 0, 0, 0)),
                   pl.BlockSpec((1, ML_HEADS, ML_DH), lambda b, c: (b, 0, 0)),
                   pl.BlockSpec((1, ML_HEADS, 1), lambda b, c: (b, 0, 0))],
        out_shape=[jax.ShapeDtypeStruct((t_all, ML_INNER), F32),
                   jax.ShapeDtypeStruct((n_seq, ML_HEADS, ML_DH, ML_DH), F32),
                   jax.ShapeDtypeStruct((n_seq, ML_HEADS, ML_DH), F32),
                   jax.ShapeDtypeStruct((n_seq, ML_HEADS, 1), F32)],
        scratch_shapes=[pltpu.VMEM((ML_HEADS, ML_DH, ML_DH), F32), pltpu.VMEM((ML_HEADS, ML_DH), F32),
                        pltpu.VMEM((ML_HEADS, 1), F32)],
        compiler_params=_params(("arbitrary", "arbitrary"), 56),
        name="ml_cell",
    )(q, k, v, xc, xmz, gates, gn, skip)
    gated, c_new, n_new, m_new = out
    return gated, c_new, n_new, m_new.reshape(n_seq, ML_HEADS)


def _ml_cell_sample_body(*refs):
    q_ref, k_ref, v_ref, xc_ref, z_ref, g_ref, gn_ref, sk_ref, c0_ref, n0_ref, m0_ref = refs[:11]
    o_ref, co_ref, no_ref, mo_ref = refs[-4:]
    seq_len = g_ref.shape[2]
    masks = _ml_masks(seq_len)
    pending = []
    for j in range(g_ref.shape[0]):
        rows = slice(j * seq_len, (j + 1) * seq_len)
        for h in range(ML_HEADS):
            sl = slice(h * ML_DH, (h + 1) * ML_DH)
            hc, update, n_new, m_new = _ml_chunk(
                q_ref[rows, sl], k_ref[rows, sl], v_ref[rows, sl], g_ref[j, h:h + 1, :],
                g_ref[j, ML_HEADS + h:ML_HEADS + h + 1, :], c0_ref[j, h], n0_ref[j, h:h + 1, :], m0_ref[j, h:h + 1, :],
                masks)
            out = _ml_gate_out(hc, gn_ref[:, sl], sk_ref[:, sl], xc_ref[rows, sl], z_ref[rows, sl])
            pending.append((j, h, rows, sl, out, update, n_new, m_new))
    for j, h, rows, sl, out, update, n_new, m_new in pending:
        o_ref[rows, sl] = out
        no_ref[j, h:h + 1, :] = n_new
        mo_ref[j, h:h + 1, :] = m_new
        co_ref[j, h] = _ml_new_c(c0_ref[j, h], update)


def _ml_cell_sample(q, k, v, xc, xmz, gates, gn, skip, n_seq, seq_len, row0, c_all, n_all, m_all, layer, n_layers,
                    c_out_prev):
    ns = ML_SAMPLE_SEQS
    rows = ns * seq_len
    blk0 = row0 // rows
    s0 = layer * n_seq // ns
    row_spec = pl.BlockSpec((rows, ML_INNER), lambda b: (blk0 + b, 0))
    full = pl.BlockSpec((1, ML_INNER), lambda b: (0, 0))
    c_spec = pl.BlockSpec((ns, ML_HEADS, ML_DH, ML_DH), lambda b: (s0 + b, 0, 0, 0))
    in_specs = [row_spec, row_spec, row_spec, row_spec,
                pl.BlockSpec((rows, ML_INNER), lambda b: (blk0 + b, 1)),
                pl.BlockSpec((ns, 2 * ML_HEADS, seq_len), lambda b: (b, 0, 0)), full, full,
                c_spec,
                pl.BlockSpec((ns, ML_HEADS, ML_DH), lambda b: (s0 + b, 0, 0)),
                pl.BlockSpec((ns, ML_HEADS, 1), lambda b: (s0 + b, 0, 0))]
    args = [q, k, v, xc, xmz, gates, gn, skip, c_all, n_all, m_all]
    aliases = {}
    if c_out_prev is not None:
        in_specs.append(pl.BlockSpec(memory_space=pl.ANY))
        args.append(c_out_prev)
        aliases[len(args) - 1] = 1
    return pl.pallas_call(
        _ml_cell_sample_body,
        grid=(n_seq // ns,),
        in_specs=in_specs,
        out_specs=[pl.BlockSpec((rows, ML_INNER), lambda b: (b, 0)), c_spec,
                   pl.BlockSpec((ns, ML_HEADS, ML_DH), lambda b: (b, 0, 0)),
                   pl.BlockSpec((ns, ML_HEADS, 1), lambda b: (b, 0, 0))],
        out_shape=[jax.ShapeDtypeStruct((n_seq * seq_len, ML_INNER), F32),
                   jax.ShapeDtypeStruct((n_layers * n_seq, ML_HEADS, ML_DH, ML_DH), F32),
                   jax.ShapeDtypeStruct((n_seq, ML_HEADS, ML_DH), F32),
                   jax.ShapeDtypeStruct((n_seq, ML_HEADS, 1), F32)],
        input_output_aliases=aliases,
        compiler_params=_params(("arbitrary",), 56),
        name="ml_cell_sample",
    )(*args)


def _ml_weights(w_in, conv_w, conv_b, wq, wk, wv, w_i, b_i, w_f, b_f, gn_g, skip, w_out):
    nt = ML_INNER // V7X_MXU_DIM
    per = V7X_MXU_DIM // ML_QKV_BLOCK
    pos = jnp.arange(V7X_MXU_DIM) // ML_QKV_BLOCK
    same_block = pos[:, None] == pos[None, :]

    def block_diag(w):
        rows = w.reshape(nt, V7X_MXU_DIM, ML_QKV_BLOCK)
        return jnp.where(same_block, jnp.tile(rows, (1, 1, per)), 0.0).astype(BF16)

    wg = jnp.concatenate([w_i, w_f], axis=1).T.astype(BF16)
    return dict(w_in=w_in.astype(BF16), conv_w=conv_w, conv_b=conv_b.reshape(1, ML_INNER),
                wq=block_diag(wq), wk=block_diag(wk), wv=block_diag(wv),
                wgq=wg[:, :ML_INNER], wgk=wg[:, ML_INNER:2 * ML_INNER], wgv=wg[:, 2 * ML_INNER:],
                gb=jnp.concatenate([b_i, b_f]).reshape(2 * ML_HEADS, 1),
                gn=gn_g.reshape(1, ML_INNER), skip=skip.reshape(1, ML_INNER), w_out=w_out.astype(BF16))


def _ml_layer(x_all, w, dims, c_all, n_all, m_all, state_conv, layer, n_layers, c_out_prev, ln_g, ln_b):
    n_p, seq_len, n_s, sample_len = dims
    t_p = n_p * seq_len
    rb = ROW_BLOCK
    xmz = _mm(x_all, w["w_in"], 2048)
    pad = jnp.pad(state_conv, ((0, 0), (0, sample_len - (ML_CONV - 1)), (0, 0))).reshape(n_s * sample_len, ML_INNER)
    conv_rows = jnp.concatenate([jnp.zeros((rb, ML_INNER), F32), pad], axis=0)
    q, k, v, xc, g3 = _ml_pre(xmz, conv_rows, w, t_p, seq_len, sample_len)
    npb = t_p // rb
    g_p = g3[:npb]
    if ML_CHUNK != rb:
        g_p = g_p.reshape(npb, 2 * ML_HEADS, rb // ML_CHUNK, ML_CHUNK).transpose(0, 2, 1, 3).reshape(-1, 2 * ML_HEADS, ML_CHUNK)
    g_s = g3[npb:].reshape(-1, 2 * ML_HEADS, rb // sample_len, sample_len).transpose(0, 2, 1, 3)
    g_s = g_s.reshape(n_s, 2 * ML_HEADS, sample_len)
    gated_p, pc, pn, pm = _ml_cell(q, k, v, xc, xmz, g_p, w["gn"], w["skip"], n_p, seq_len, ML_CHUNK)
    gated_s, c_out, sn, sm = _ml_cell_sample(q, k, v, xc, xmz, g_s, w["gn"], w["skip"], n_s, sample_len, t_p,
                                             c_all, n_all, m_all, layer, n_layers, c_out_prev)
    h1 = _mm_ln(gated_p, gated_s, w["w_out"], x_all, ln_g, ln_b)
    tail = ML_CONV - 1
    p_conv = jnp.stack([xmz[(b + 1) * seq_len - tail:(b + 1) * seq_len, :ML_INNER] for b in range(n_p)])
    s_conv = xmz[t_p:, :ML_INNER].reshape(n_s, sample_len, ML_INNER)[:, sample_len - tail:]
    return h1, (pc, pn, pm, p_conv), (sn, sm.reshape(n_s, ML_HEADS), s_conv), c_out


def _gla_in_body(x_ref, w_ref, wa_ref, wa2_ref, ba_ref, o_ref, la_ref):
    x = x_ref[...].astype(BF16)
    o_ref[...] = jnp.dot(x, w_ref[...], preferred_element_type=F32)
    a = jnp.dot(x, wa_ref[...], preferred_element_type=F32)
    la_ref[...] = _log_sigmoid(_dot(a, wa2_ref[...]) + ba_ref[...]) / GLA_TAU


def _gla_in(x, w):
    m, k = x.shape
    tm = MM_ROWS
    n1 = 2 * GLA_KT + 2 * GLA_VT
    full = lambda shape: pl.BlockSpec(shape, lambda i: (0, 0))
    return pl.pallas_call(
        _gla_in_body,
        grid=(m // tm,),
        in_specs=[pl.BlockSpec((tm, k), lambda i: (i, 0)), full((k, n1)), full((k, V7X_LANES)),
                  full((V7X_LANES, GLA_KT)), full((1, GLA_KT))],
        out_specs=[pl.BlockSpec((tm, n1), lambda i: (i, 0)), pl.BlockSpec((tm, GLA_KT), lambda i: (i, 0))],
        out_shape=[jax.ShapeDtypeStruct((m, n1), F32), jax.ShapeDtypeStruct((m, GLA_KT), F32)],
        compiler_params=_params(("arbitrary",), 48),
        name="gla_in",
    )(x, w["w_qkvr"], w["w_a"], w["w_a2"], w["b_a"])


def _gla_levels(chunk):
    t = jnp.arange(chunk)[:, None]
    s = jnp.arange(chunk)[None, :]
    mats = [(s <= t)]
    size = chunk
    while size >= 2:
        ref = t - t % size + size // 2 - 1
        mats.append(s <= ref)
        size //= 2
    levels = jnp.concatenate(mats, axis=0).astype(BF16)
    return jnp.pad(levels, ((0, 0), (0, max(V7X_LANES - chunk, 0))))


def _gla_chunk(q, k, v, la, lv, st):
    L = q.shape[0]
    kpad = lv.shape[1]
    parts = jnp.concatenate(_split3(la), axis=1)
    if kpad > L:
        parts = jnp.concatenate([parts, jnp.zeros((kpad - L, parts.shape[1]), BF16)], axis=0)
    cum = jnp.dot(lv, parts, preferred_element_type=F32)
    cum = cum[:, :GLA_DK] + cum[:, GLA_DK:2 * GLA_DK] + cum[:, 2 * GLA_DK:]
    b = cum[0:L]
    ri = lax.broadcasted_iota(jnp.int32, (L, L), 0)
    ci = lax.broadcasted_iota(jnp.int32, (L, L), 1)
    tpos = lax.broadcasted_iota(jnp.int32, (L, 1), 0)
    att = jnp.where(ri == ci, jnp.sum(q * k, axis=1, keepdims=True), 0.0)
    size = L
    lvl = 1
    while size >= 2:
        bref = cum[lvl * L:(lvl + 1) * L]
        upper = (tpos % size) >= (size // 2)
        qs = q * jnp.exp(jnp.where(upper, b - bref, -jnp.inf))
        ks = k * jnp.exp(jnp.where(upper, -jnp.inf, bref - b))
        att = att + jnp.where((ri // size) == (ci // size), _dot_nt(qs, ks), 0.0)
        size //= 2
        lvl += 1
    o = _dot(att, v) + _dot(q * jnp.exp(b), st)
    b_end = b[L - 1:L, :]
    e_end = jnp.exp(b_end)
    rk = lax.broadcasted_iota(jnp.int32, (GLA_DK, GLA_DK), 0)
    ck = lax.broadcasted_iota(jnp.int32, (GLA_DK, GLA_DK), 1)
    e_col = jnp.sum(jnp.where(rk == ck, e_end, 0.0), axis=1, keepdims=True)
    st_new = e_col * st + _dot_tn(k * jnp.exp(b_end - b), v)
    return o, st_new


def _gla_heads(x_ref, la_ref, lv_ref, gn_ref, o_ref, read_state, write_state, rows=slice(None), defer_stores=False):
    lv = lv_ref[...]
    stores = []
    for h in range(GLA_HEADS):
        qs = slice(h * GLA_DK, (h + 1) * GLA_DK)
        ks = slice(GLA_KT + h * GLA_DK, GLA_KT + (h + 1) * GLA_DK)
        vs = slice(2 * GLA_KT + h * GLA_DV, 2 * GLA_KT + (h + 1) * GLA_DV)
        rs = slice(2 * GLA_KT + GLA_VT + h * GLA_DV, 2 * GLA_KT + GLA_VT + (h + 1) * GLA_DV)
        os_ = slice(h * GLA_DV, (h + 1) * GLA_DV)
        o, st_new = _gla_chunk(x_ref[rows, qs] * (GLA_DK ** -0.5), x_ref[rows, ks], x_ref[rows, vs], la_ref[rows, qs],
                               lv, read_state(h))
        out = _silu(x_ref[rows, rs]) * _head_norm(o, gn_ref[:, os_])

        def store(h=h, os_=os_, out=out, st_new=st_new):
            o_ref[rows, os_] = out
            write_state(h, st_new)

        if defer_stores:
            stores.append(store)
        else:
            store()
    return stores


def _gla_cell_body(x_ref, la_ref, lv_ref, gn_ref, o_ref, so_ref, ss):
    c = pl.program_id(1)
    nc = pl.num_programs(1)

    @pl.when(c == 0)
    def _():
        ss[...] = jnp.zeros(ss.shape, F32)

    def write(h, st):
        ss[h] = st

    _gla_heads(x_ref, la_ref, lv_ref, gn_ref, o_ref, lambda h: ss[h], write)

    @pl.when(c == nc - 1)
    def _():
        so_ref[0] = ss[...]


def _gla_cell(qkvr, la, gn, n_seq, seq_len, chunk):
    n1 = qkvr.shape[1]
    nc = seq_len // chunk
    levels = _gla_levels(chunk)
    return pl.pallas_call(
        _gla_cell_body,
        grid=(n_seq, nc),
        in_specs=[pl.BlockSpec((chunk, n1), lambda b, c: (b * nc + c, 0)),
                  pl.BlockSpec((chunk, GLA_KT), lambda b, c: (b * nc + c, 0)),
                  pl.BlockSpec(levels.shape, lambda b, c: (0, 0)),
                  pl.BlockSpec((1, GLA_VT), lambda b, c: (0, 0))],
        out_specs=[pl.BlockSpec((chunk, GLA_VT), lambda b, c: (b * nc + c, 0)),
                   pl.BlockSpec((1, GLA_HEADS, GLA_DK, GLA_DV), lambda b, c: (b, 0, 0, 0))],
        out_shape=[jax.ShapeDtypeStruct((n_seq * seq_len, GLA_VT), F32),
                   jax.ShapeDtypeStruct((n_seq, GLA_HEADS, GLA_DK, GLA_DV), F32)],
        scratch_shapes=[pltpu.VMEM((GLA_HEADS, GLA_DK, GLA_DV), F32)],
        compiler_params=_params(("arbitrary", "arbitrary"), 32),
        name="gla_cell",
    )(qkvr, la, levels, gn)


def _gla_cell_sample_body(x_ref, la_ref, lv_ref, gn_ref, s0_ref, o_ref, so_ref):
    seq_len = x_ref.shape[0] // s0_ref.shape[0]
    stores = []
    for j in range(s0_ref.shape[0]):

        def write(h, st, j=j):
            so_ref[j, h] = st

        stores += _gla_heads(x_ref, la_ref, lv_ref, gn_ref, o_ref, lambda h, j=j: s0_ref[j, h], write,
                             rows=slice(j * seq_len, (j + 1) * seq_len), defer_stores=True)
    for store in stores:
        store()


def _gla_cell_sample(qkvr, la, gn, n_seq, seq_len, row0, state):
    n1 = qkvr.shape[1]
    ns = GLA_SAMPLE_SEQS
    rows = ns * seq_len
    blk0 = row0 // rows
    levels = _gla_levels(seq_len)
    st_spec = pl.BlockSpec((ns, GLA_HEADS, GLA_DK, GLA_DV), lambda b: (b, 0, 0, 0))
    return pl.pallas_call(
        _gla_cell_sample_body,
        grid=(n_seq // ns,),
        in_specs=[pl.BlockSpec((rows, n1), lambda b: (blk0 + b, 0)),
                  pl.BlockSpec((rows, GLA_KT), lambda b: (blk0 + b, 0)),
                  pl.BlockSpec(levels.shape, lambda b: (0, 0)),
                  pl.BlockSpec((1, GLA_VT), lambda b: (0, 0)),
                  st_spec],
        out_specs=[pl.BlockSpec((rows, GLA_VT), lambda b: (b, 0)), st_spec],
        out_shape=[jax.ShapeDtypeStruct((n_seq * seq_len, GLA_VT), F32),
                   jax.ShapeDtypeStruct((n_seq, GLA_HEADS, GLA_DK, GLA_DV), F32)],
        compiler_params=_params(("arbitrary",), 32),
        name="gla_cell_sample",
    )(qkvr, la, levels, gn, state)


def _gla_weights(w_in, w_a2, b_a, gn_g, w_out):
    n1 = 2 * GLA_KT + 2 * GLA_VT
    w_a = jnp.pad(w_in[:, n1:], ((0, 0), (0, V7X_LANES - GLA_RANK)))
    w_a2p = jnp.pad(w_a2, ((0, V7X_LANES - GLA_RANK), (0, 0)))
    return dict(w_qkvr=w_in[:, :n1].astype(BF16), w_a=w_a.astype(BF16), w_a2=w_a2p.astype(BF16),
                b_a=b_a.reshape(1, GLA_KT), gn=gn_g.reshape(1, GLA_VT), w_out=w_out.astype(BF16))


def _gla_layer(x_all, w, dims, state_s, ln_g, ln_b):
    n_p, seq_len, n_s, sample_len = dims
    t_p = n_p * seq_len
    qkvr, la = _gla_in(x_all, w)
    gated_p, p_s = _gla_cell(qkvr, la, w["gn"], n_p, seq_len, min(GLA_CHUNK, seq_len))
    gated_s, s_s = _gla_cell_sample(qkvr, la, w["gn"], n_s, sample_len, t_p, state_s)
    h1 = _mm_ln(gated_p, gated_s, w["w_out"], x_all, ln_g, ln_b)
    return h1, p_s, s_s


def _s5_body(x_ref, sre_ref, sim_ref, win_ref, bt_ref, cre_ref, cim_ref, lre_ref, lim_ref, d_ref, wglu_ref,
             g_ref, b_ref, o_ref, ore_ref, oim_ref, st_re, st_im, car_re, car_im, *, nb, lc):
    c = pl.program_id(1)
    nc = pl.num_programs(1)
    rows = nb * lc
    ntile = D_MODEL // S5_TILE_CH

    @pl.when(c == 0)
    def _():
        car_re[...] = sre_ref[...]
        car_im[...] = sim_ref[...]

    x = x_ref[...].reshape(rows, D_MODEL)
    u = _dot(x, win_ref[...])
    lpt = S5_TILE_ST // V7X_LANES
    for t in range(ntile):
        bu = _dot(u[:, t * S5_TILE_CH:(t + 1) * S5_TILE_CH], bt_ref[t])
        for j in range(lpt):
            st_re[t * lpt + j] = bu[:, j * V7X_LANES:(j + 1) * V7X_LANES]
            st_im[t * lpt + j] = bu[:, S5_TILE_ST + j * V7X_LANES:S5_TILE_ST + (j + 1) * V7X_LANES]

    for gidx in range(nb // V7X_SUBLANES):
        grp = slice(gidx * V7X_SUBLANES, (gidx + 1) * V7X_SUBLANES)
        for t in range(ntile):
            slabs = list(range(t * lpt, (t + 1) * lpt))
            lanes = [slice(j * V7X_LANES, (j + 1) * V7X_LANES) for j in slabs]
            lre = [lre_ref[:, ln] for ln in lanes]
            lim = [lim_ref[:, ln] for ln in lanes]

            def step(tok, carry, slabs=slabs, lre=lre, lim=lim, gidx=gidx):
                sel = pl.ds(pl.multiple_of(tok * nb + gidx * V7X_SUBLANES, V7X_SUBLANES), V7X_SUBLANES)
                out = []
                for n, j in enumerate(slabs):
                    pr, pi = carry[2 * n], carry[2 * n + 1]
                    nr = lre[n] * pr - lim[n] * pi + st_re[j, sel, :]
                    ni = lre[n] * pi + lim[n] * pr + st_im[j, sel, :]
                    st_re[j, sel, :] = nr
                    st_im[j, sel, :] = ni
                    out += [nr, ni]
                return tuple(out)

            init = []
            for ln in lanes:
                init += [car_re[grp, ln], car_im[grp, ln]]
            fin = lax.fori_loop(0, lc, step, tuple(init))
            for n, ln in enumerate(lanes):
                car_re[grp, ln] = fin[2 * n]
                car_im[grp, ln] = fin[2 * n + 1]

    ys = []
    for t in range(ntile):
        sre = jnp.concatenate([st_re[t * lpt + j] for j in range(lpt)], axis=1)
        sim = jnp.concatenate([st_im[t * lpt + j] for j in range(lpt)], axis=1)
        ys.append(_dot(sre, cre_ref[t]) - _dot(sim, cim_ref[t]))
    y = jnp.concatenate(ys, axis=1) + d_ref[...] * u
    y = jax.nn.gelu(y)
    vg = _dot(y, wglu_ref[...])
    mix = vg[:, :D_MODEL] * _sigmoid(vg[:, D_MODEL:])
    o_ref[...] = _layer_norm(DEEPNORM_ALPHA * x + mix, g_ref[...], b_ref[...]).reshape(lc, nb, D_MODEL)

    @pl.when(c == nc - 1)
    def _():
        ore_ref[...] = car_re[...]
        oim_ref[...] = car_im[...]


def _s5_call(x3, s_re, s_im, w, ln_g, ln_b, nb, lc):
    seq_len, n_seq, _ = x3.shape
    ntile = D_MODEL // S5_TILE_CH
    rows = nb * lc
    full2 = lambda shape: pl.BlockSpec(shape, lambda i, c: (0, 0), pipeline_mode=pl.Buffered(1))
    full3 = lambda shape: pl.BlockSpec(shape, lambda i, c: (0, 0, 0), pipeline_mode=pl.Buffered(1))
    st_spec = pl.BlockSpec((nb, S5_STATE), lambda i, c: (i, 0))
    x_spec = pl.BlockSpec((lc, nb, D_MODEL), lambda i, c: (c, i, 0))
    return pl.pallas_call(
        functools.partial(_s5_body, nb=nb, lc=lc),
        grid=(n_seq // nb, seq_len // lc),
        in_specs=[x_spec, st_spec, st_spec,
                  full2((D_MODEL, D_MODEL)), full3((ntile, S5_TILE_CH, 2 * S5_TILE_ST)),
                  full3((ntile, S5_TILE_ST, S5_TILE_CH)), full3((ntile, S5_TILE_ST, S5_TILE_CH)),
                  full2((V7X_SUBLANES, S5_STATE)), full2((V7X_SUBLANES, S5_STATE)), full2((1, D_MODEL)),
                  full2((D_MODEL, 2 * D_MODEL)), full2((1, D_MODEL)), full2((1, D_MODEL))],
        out_specs=[x_spec, st_spec, st_spec],
        out_shape=[jax.ShapeDtypeStruct(x3.shape, F32), jax.ShapeDtypeStruct((n_seq, S5_STATE), F32),
                   jax.ShapeDtypeStruct((n_seq, S5_STATE), F32)],
        scratch_shapes=[pltpu.VMEM((S5_STATE // V7X_LANES, rows, V7X_LANES), F32),
                        pltpu.VMEM((S5_STATE // V7X_LANES, rows, V7X_LANES), F32),
                        pltpu.VMEM((nb, S5_STATE), F32), pltpu.VMEM((nb, S5_STATE), F32)],
        compiler_params=_params(("arbitrary", "arbitrary"), 56),
        name="s5",
    )(x3, s_re, s_im, w["w_in"], w["b_tiles"], w["c_re"], w["c_im"], w["lam_re"], w["lam_im"], w["d"], w["w_glu"],
      ln_g.reshape(1, D_MODEL), ln_b.reshape(1, D_MODEL))


def _s5_weights(w_in, a_re, a_im, log_dt, b_re, b_im, c_re, c_im, d_skip, w_glu):
    lam = lax.complex(a_re.astype(F32), a_im.astype(F32))
    dt = jnp.exp(log_dt.astype(F32))[:, None]
    lam_bar = jnp.exp(lam * dt)
    b_bar = ((lam_bar - 1.0) / lam)[..., None] * lax.complex(b_re.astype(F32), b_im.astype(F32))
    ntile = D_MODEL // S5_TILE_CH
    gpt = S5_TILE_CH // S5_GC
    eye = jnp.eye(gpt, dtype=F32)

    def b_tiles(bb):
        return jnp.einsum("igpc,gh->igchp", bb.reshape(ntile, gpt, S5_P, S5_GC), eye).reshape(ntile, S5_TILE_CH, S5_TILE_ST)

    def c_tiles(cc):
        return jnp.einsum("igcp,gh->igphc", cc.reshape(ntile, gpt, S5_GC, S5_P), eye).reshape(ntile, S5_TILE_ST, S5_TILE_CH)

    bt = jnp.concatenate([b_tiles(b_bar.real), b_tiles(b_bar.imag)], axis=2).astype(BF16)
    bcast = lambda a: jnp.broadcast_to(a.reshape(1, S5_STATE), (V7X_SUBLANES, S5_STATE))
    return dict(w_in=w_in.astype(BF16), b_tiles=bt, c_re=c_tiles(c_re.astype(F32)).astype(BF16),
                c_im=c_tiles(c_im.astype(F32)).astype(BF16), lam_re=bcast(lam_bar.real), lam_im=bcast(lam_bar.imag),
                d=d_skip.reshape(1, D_MODEL).astype(F32), w_glu=w_glu.astype(BF16))


def _s5_layer(x_all, w, dims, state_re, state_im, ln_g, ln_b):
    n_p, seq_len, n_s, sample_len = dims
    t_p = n_p * seq_len
    xp = x_all[:t_p].reshape(n_p, seq_len, D_MODEL).transpose(1, 0, 2)
    xs = x_all[t_p:].reshape(n_s, sample_len, D_MODEL).transpose(1, 0, 2)
    zero = jnp.zeros((n_p, S5_STATE), F32)
    hp, p_re, p_im = _s5_call(xp, zero, zero, w, ln_g, ln_b, n_p, min(S5_CHUNK, seq_len))
    nb_s = min(n_s, ROW_BLOCK // sample_len)
    hs, s_re, s_im = _s5_call(xs, state_re.reshape(n_s, S5_STATE), state_im.reshape(n_s, S5_STATE), w, ln_g, ln_b,
                              nb_s, sample_len)
    h1 = jnp.concatenate([hp.transpose(1, 0, 2).reshape(t_p, D_MODEL),
                          hs.transpose(1, 0, 2).reshape(n_s * sample_len, D_MODEL)], axis=0)
    shp = lambda a, n: a.reshape(n, S5_GROUPS, S5_P)
    return h1, (shp(p_re, n_p), shp(p_im, n_p)), (shp(s_re, n_s), shp(s_im, n_s))


def _route_body(x_ref, wh_ref, wl_ref, br_ref, g_ref):
    x = x_ref[...]
    xh = x.astype(BF16)
    xl = (x - xh.astype(F32)).astype(BF16)
    wh = wh_ref[...]
    logits = _dot_nt(wh, xh) + _dot_nt(wh, xl) + _dot_nt(wl_ref[...], xh)
    s = _sigmoid(logits)
    work = s + br_ref[...]
    row = lax.broadcasted_iota(jnp.int32, s.shape, 0).astype(F32)
    chosen = jnp.zeros(s.shape, jnp.bool_)
    for _ in range(TOP_K):
        mx = jnp.max(work, axis=0, keepdims=True)
        idx = jnp.min(jnp.where(work == mx, row, float(N_EXPERTS)), axis=0, keepdims=True)
        hit = row == idx
        chosen = jnp.logical_or(chosen, hit)
        work = jnp.where(hit, -jnp.inf, work)
    sel = jnp.where(chosen, s, 0.0)
    g_ref[...] = sel / jnp.sum(sel, axis=0, keepdims=True) * ROUTE_SCALE


def _route(x, w):
    m, k = x.shape
    tm = MM_ROWS
    full = lambda shape: pl.BlockSpec(shape, lambda i: (0, 0))
    gates_t = pl.pallas_call(
        _route_body,
        grid=(m // tm,),
        in_specs=[pl.BlockSpec((tm, k), lambda i: (i, 0)), full((N_EXPERTS, k)), full((N_EXPERTS, k)),
                  full((N_EXPERTS, 1))],
        out_specs=pl.BlockSpec((N_EXPERTS, tm), lambda i: (0, i)),
        out_shape=jax.ShapeDtypeStruct((N_EXPERTS, m), F32),
        compiler_params=_params(("arbitrary",), 32),
        name="route",
    )(x, w["wr_hi"], w["wr_lo"], w["b_router"])
    return gates_t.T


def _moe_body(x_ref, gd_ref, wg_ref, wu_ref, wd_ref, sg_ref, su_ref, sd_ref, g_ref, b_ref, o_ref, acc, xb_ref):
    e = pl.program_id(1)
    ne = pl.num_programs(1)

    @pl.when(e == 0)
    def _():
        xb0 = x_ref[...].astype(BF16)
        xb_ref[...] = xb0
        hs = _silu(jnp.dot(xb0, sg_ref[...], preferred_element_type=F32)) * jnp.dot(xb0, su_ref[...], preferred_element_type=F32)
        acc[...] = _dot(hs, sd_ref[...])

    xb = xb_ref[...]
    gd = gd_ref[...]
    lane = lax.broadcasted_iota(jnp.int32, gd.shape, 1)
    total = None
    for j in range(MOE_EXPERTS_PER_STEP):
        h = _silu(_dot(xb, wg_ref[j])) * _dot(xb, wu_ref[j])
        gate = jnp.sum(jnp.where(lane == e * MOE_EXPERTS_PER_STEP + j, gd, 0.0), axis=1, keepdims=True)
        y = _dot(h, wd_ref[j]) * gate
        total = y if total is None else total + y
    acc[...] += total

    @pl.when(e == ne - 1)
    def _():
        o_ref[...] = _layer_norm(DEEPNORM_ALPHA * x_ref[...] + acc[...], g_ref[...], b_ref[...])


def _moe_ffn(x, gates, w, experts, layer, ln_g, ln_b):
    m, k = x.shape
    tm = MOE_ROWS
    steps = N_EXPERTS // MOE_EXPERTS_PER_STEP
    full = lambda shape: pl.BlockSpec(shape, lambda i, e: (0, 0))
    expert_map = lambda i, e: (layer * steps + e, 0, 0)
    return pl.pallas_call(
        _moe_body,
        grid=(m // tm, steps),
        in_specs=[pl.BlockSpec((tm, k), lambda i, e: (i, 0)), pl.BlockSpec((tm, N_EXPERTS), lambda i, e: (i, 0)),
                  pl.BlockSpec((MOE_EXPERTS_PER_STEP, k, EXPERT_FF), expert_map),
                  pl.BlockSpec((MOE_EXPERTS_PER_STEP, k, EXPERT_FF), expert_map),
                  pl.BlockSpec((MOE_EXPERTS_PER_STEP, EXPERT_FF, k), expert_map),
                  full((k, w["ws_gate"].shape[1])), full((k, w["ws_up"].shape[1])), full((w["ws_down"].shape[0], k)),
                  full((1, k)), full((1, k))],
        out_specs=pl.BlockSpec((tm, k), lambda i, e: (i, 0)),
        out_shape=jax.ShapeDtypeStruct((m, k), F32),
        scratch_shapes=[pltpu.VMEM((tm, k), F32), pltpu.VMEM((tm, k), BF16)],
        compiler_params=_params(("arbitrary", "arbitrary"), 60),
        name="moe_ffn",
    )(x, gates, experts[0], experts[1], experts[2], w["ws_gate"], w["ws_up"], w["ws_down"],
      ln_g.reshape(1, k), ln_b.reshape(1, k))


def _moe_weights(w_router, b_router, ws_gate, ws_up, ws_down):
    wr_t = w_router.T
    wr_hi = wr_t.astype(BF16)
    wr_lo = (wr_t - wr_hi.astype(F32)).astype(BF16)
    return dict(wr_hi=wr_hi, wr_lo=wr_lo, b_router=b_router.reshape(N_EXPERTS, 1).astype(F32),
                ws_gate=ws_gate.astype(BF16), ws_up=ws_up.astype(BF16), ws_down=ws_down.astype(BF16))


def _moe_layer(x, w, experts, layer, ln_g, ln_b):
    return _moe_ffn(x, _route(x, w), w, experts, layer, ln_g, ln_b)


def kernel(x_prompt, x_sample, state_mlstm_C, state_mlstm_n, state_mlstm_m, state_mlstm_conv, state_gla_S, state_s5_re, state_s5_im, ln1_g, ln1_b, ln2_g, ln2_b, ml_w_in, ml_conv_w, ml_conv_b, ml_wq, ml_wk, ml_wv, ml_w_i, ml_b_i, ml_w_f, ml_b_f, ml_gn_g, ml_skip, ml_w_out, gla_w_in, gla_w_a2, gla_b_a, gla_gn_g, gla_w_out, s5_w_in, s5_a_re, s5_a_im, s5_log_dt, s5_b_re, s5_b_im, s5_c_re, s5_c_im, s5_d, s5_w_glu, moe_w_router, moe_b_router, moe_w_gate, moe_w_up, moe_w_down, moe_ws_gate, moe_ws_up, moe_ws_down):
    n_p, seq_len, _ = x_prompt.shape
    n_s, sample_len, _ = x_sample.shape
    dims = (n_p, seq_len, n_s, sample_len)
    t_p = n_p * seq_len
    t_s = n_s * sample_len
    assert seq_len % MM_ROWS == 0 and t_s % MM_ROWS == 0 and MM_ROWS % ROW_BLOCK == 0 and ROW_BLOCK % sample_len == 0
    assert sample_len == V7X_SUBLANES and n_p % V7X_SUBLANES == 0 and (t_p + t_s) % MOE_ROWS == 0
    x = jnp.concatenate([x_prompt.reshape(t_p, D_MODEL), x_sample.reshape(t_s, D_MODEL)], axis=0)
    n_ml = len(range(0, DEPTH, N_MIXERS))
    c_all = state_mlstm_C.reshape(-1, ML_HEADS, ML_DH, ML_DH)
    n_all = state_mlstm_n.reshape(-1, ML_HEADS, ML_DH)
    m_all = state_mlstm_m.reshape(-1, ML_HEADS, 1)
    c_out = None
    flat_experts = lambda w: w.reshape((-1,) + w.shape[2:])
    experts = (flat_experts(moe_w_gate), flat_experts(moe_w_up), flat_experts(moe_w_down))
    p_ml, s_ml, p_gla, s_gla, p_s5, s_s5 = [], [], [], [], [], []
    for i in range(DEPTH):
        j = i // N_MIXERS
        if i % N_MIXERS == 0:
            w = _ml_weights(ml_w_in[j], ml_conv_w[j], ml_conv_b[j], ml_wq[j], ml_wk[j], ml_wv[j], ml_w_i[j], ml_b_i[j],
                            ml_w_f[j], ml_b_f[j], ml_gn_g[j], ml_skip[j], ml_w_out[j])
            x, ps, ss, c_out = _ml_layer(x, w, dims, c_all, n_all, m_all, state_mlstm_conv[j], j, n_ml, c_out,
                                         ln1_g[i], ln1_b[i])
            p_ml.append(ps)
            s_ml.append(ss)
        elif i % N_MIXERS == 1:
            w = _gla_weights(gla_w_in[j], gla_w_a2[j], gla_b_a[j], gla_gn_g[j], gla_w_out[j])
            x, ps, ss = _gla_layer(x, w, dims, state_gla_S[j], ln1_g[i], ln1_b[i])
            p_gla.append(ps)
            s_gla.append(ss)
        else:
            w = _s5_weights(s5_w_in[j], s5_a_re[j], s5_a_im[j], s5_log_dt[j], s5_b_re[j], s5_b_im[j], s5_c_re[j],
                            s5_c_im[j], s5_d[j], s5_w_glu[j])
            x, ps, ss = _s5_layer(x, w, dims, state_s5_re[j], state_s5_im[j], ln1_g[i], ln1_b[i])
            p_s5.append(ps)
            s_s5.append(ss)
        wm = _moe_weights(moe_w_router[i], moe_b_router[i], moe_ws_gate[i], moe_ws_up[i], moe_ws_down[i])
        x = _moe_layer(x, wm, experts, i, ln2_g[i], ln2_b[i])
    stack = lambda items, idx: jnp.stack([it[idx] for it in items])
    return (x[:t_p].reshape(n_p, seq_len, D_MODEL), x[t_p:].reshape(n_s, sample_len, D_MODEL),
            stack(p_ml, 0), stack(p_ml, 1), stack(p_ml, 2), stack(p_ml, 3), jnp.stack(p_gla),
            stack(p_s5, 0), stack(p_s5, 1),
            c_out.reshape(n_ml, n_s, ML_HEADS, ML_DH, ML_DH), stack(s_ml, 0), stack(s_ml, 1), stack(s_ml, 2),
            jnp.stack(s_gla), stack(s_s5, 0), stack(s_s5, 1))
```
